```python
import math
import jax, jax.numpy as jnp
from jax import lax
import numpy as np

D_MODEL = 1024
BATCH = 2
SEQ = 16384
DEPTH = 1
DEC_BATCH = 16
DEC_SEQ = 4096
PAST_LEN = 128

N_HEADS_A = 8
DQK_A = 64
DV_A = 2 * DQK_A
ROT_DIM = DQK_A // 4
ROPE_THETA = 500000.0
Q_BLOCK = 128
N_HEADS_M = 4
DQK_M = 128
DV_M = 256
CHUNK = 128
D_FF = 2816
CONV_W = 3
ALPHA = (2.0 * DEPTH) ** 0.25
BETA = (8.0 * DEPTH) ** -0.25
LN_EPS = 1e-5
ADA_SCALE = 0.5

W_AQ = N_HEADS_A * 2 * DQK_A
W_AK = N_HEADS_A * 2 * DQK_A
W_AV = N_HEADS_A * DV_A
W_MQ = N_HEADS_M * DQK_M
W_MK = N_HEADS_M * DQK_M
W_MV = N_HEADS_M * DV_M
W_MO = N_HEADS_M * DV_M
W_MG = 2 * 2 * N_HEADS_M
W_BG = 2 * D_MODEL
SEG_WIDTHS = (W_AQ, W_AK, W_AV, W_MQ, W_MK, W_MV, W_MO, W_MG, W_BG)
SPLITS = [int(s) for s in np.cumsum(SEG_WIDTHS)[:-1]]
D_IN = int(sum(SEG_WIDTHS))

kernel_name = "hybrid_diffattn_mlstm_encoder"


def layer_norm(x, g=None, b=None):
    xf = x.astype(jnp.float32)
    mu = jnp.mean(xf, axis=-1, keepdims=True)
    var = jnp.mean(jnp.square(xf - mu), axis=-1, keepdims=True)
    y = (xf - mu) * lax.rsqrt(var + LN_EPS)
    if g is not None:
        y = y * g.astype(jnp.float32) + b.astype(jnp.float32)
    return y.astype(x.dtype)


def head_norm(x, g):
    xf = x.astype(jnp.float32)
    mu = jnp.mean(xf, axis=-1, keepdims=True)
    var = jnp.mean(jnp.square(xf - mu), axis=-1, keepdims=True)
    return ((xf - mu) * lax.rsqrt(var + LN_EPS) * g.astype(jnp.float32)).astype(x.dtype)


def rms_norm(x, g):
    xf = x.astype(jnp.float32)
    y = xf * lax.rsqrt(jnp.mean(jnp.square(xf), axis=-1, keepdims=True) + LN_EPS)
    return (y * g.astype(jnp.float32)).astype(x.dtype)


def rope_tables(seq):
    inv = ROPE_THETA ** (-jnp.arange(0, ROT_DIM, 2, dtype=jnp.float32) / ROT_DIM)
    ang = jnp.arange(seq, dtype=jnp.float32)[:, None] * inv[None, :]
    return jnp.cos(ang), jnp.sin(ang)


def apply_partial_rope(x, cos, sin):
    half = ROT_DIM // 2
    x1, x2, rest = x[..., :half], x[..., half:ROT_DIM], x[..., ROT_DIM:]
    c = cos[:, None, None, :].astype(x.dtype)
    s = sin[:, None, None, :].astype(x.dtype)
    return jnp.concatenate([x1 * c - x2 * s, x2 * c + x1 * s, rest], axis=-1)


def diff_attention(q, k, v, lam, subln_g, lam_init):
    B, S = q.shape[0], q.shape[1]
    cos, sin = rope_tables(S)
    q = apply_partial_rope(q, cos, sin) * (DQK_A ** -0.5)
    k = apply_partial_rope(k, cos, sin)
    kt = k.transpose(0, 2, 3, 1, 4)
    vt = v.transpose(0, 2, 1, 3)
    nb = S // Q_BLOCK
    qb = q.reshape(B, nb, Q_BLOCK, N_HEADS_A, 2, DQK_A).transpose(1, 0, 3, 4, 2, 5)

    def block(qi):
        s = jnp.einsum('bhiqd,bhikd->bhiqk', qi, kt).astype(jnp.float32)
        p = jax.nn.softmax(s, axis=-1)
        a = p[:, :, 0] - lam * p[:, :, 1]
        return jnp.einsum('bhqk,bhkd->bhqd', a.astype(vt.dtype), vt)

    o = lax.map(block, qb)
    o = o.transpose(1, 0, 3, 2, 4).reshape(B, S, N_HEADS_A, DV_A)
    o = rms_norm(o, subln_g) * (1.0 - lam_init)
    return o.reshape(B, S, N_HEADS_A * DV_A)


def mlstm_dir(q, k, v, ig, lf):
    B, H, S, dk = q.shape
    dv = v.shape[-1]
    nc = S // CHUNK

    def chunks(a):
        return jnp.moveaxis(a.reshape(B, H, nc, CHUNK, *a.shape[3:]), 2, 0)

    tri = jnp.tril(jnp.ones((CHUNK, CHUNK), dtype=bool))

    def step(carry, xs):
        C, n, m = carry
        qc, kc, vc, ic, fc = xs
        b = jnp.cumsum(fc, axis=-1)
        d = b[..., :, None] - b[..., None, :] + ic[..., None, :]
        d = jnp.where(tri, d, -jnp.inf)
        inter = b + m[..., None]
        m_t = jnp.maximum(inter, jnp.max(d, axis=-1))
        w = jnp.exp(d - m_t[..., None])
        w_inter = jnp.exp(inter - m_t)
        qk = jnp.einsum('bhtd,bhsd->bhts', qc, kc) * w
        num = jnp.einsum('bhts,bhsv->bhtv', qk, vc) + w_inter[..., None] * jnp.einsum('bhvd,bhtd->bhtv', C, qc)
        den = jnp.sum(qk, axis=-1) + w_inter * jnp.einsum('bhd,bhtd->bht', n, qc)
        h = num / jnp.maximum(jnp.abs(den), jnp.exp(-m_t))[..., None]
        m_new = m_t[..., -1]
        ws = jnp.exp(b[..., -1:] - b + ic - m_new[..., None])
        decay = jnp.exp(b[..., -1] + m - m_new)
        C = decay[..., None, None] * C + jnp.einsum('bhs,bhsv,bhsd->bhvd', ws, vc, kc)
        n = decay[..., None] * n + jnp.einsum('bhs,bhsd->bhd', ws, kc)
        return (C, n, m_new), h

    init = (jnp.zeros((B, H, dv, dk), jnp.float32),
            jnp.zeros((B, H, dk), jnp.float32),
            jnp.zeros((B, H), jnp.float32))
    _, hs = lax.scan(step, init, (chunks(q), chunks(k), chunks(v), chunks(ig), chunks(lf)))
    return jnp.moveaxis(hs, 0, 2).reshape(B, H, S, dv)


def mlstm_bidir(mq, mk, mv, mo, mg, b_mgate, mnorm_g):
    B, S = mq.shape[0], mq.shape[1]
    q = mq.reshape(B, S, N_HEADS_M, DQK_M).transpose(0, 2, 1, 3).astype(jnp.float32)
    k = mk.reshape(B, S, N_HEADS_M, DQK_M).transpose(0, 2, 1, 3).astype(jnp.float32) * (DQK_M ** -0.5)
    v = mv.reshape(B, S, N_HEADS_M, DV_M).transpose(0, 2, 1, 3).astype(jnp.float32)
    g = mg.reshape(B, S, 2, 2, N_HEADS_M).astype(jnp.float32) + b_mgate.astype(jnp.float32)
    g = g.transpose(2, 3, 0, 4, 1)
    fwd = mlstm_dir(q, k, v, g[0, 0], jax.nn.log_sigmoid(g[0, 1]))
    flip = lambda a: jnp.flip(a, axis=2)
    bwd = flip(mlstm_dir(flip(q), flip(k), flip(v), flip(g[1, 0]), jax.nn.log_sigmoid(flip(g[1, 1]))))
    h = (fwd + bwd).transpose(0, 2, 1, 3)
    h = head_norm(h, mnorm_g).reshape(B, S, N_HEADS_M * DV_M).astype(mo.dtype)
    return jax.nn.sigmoid(mo) * h


def dwconv3(u, w, b):
    up = jnp.pad(u, ((0, 0), (1, 1), (0, 0)))
    return up[:, :-2] * w[0] + up[:, 1:-1] * w[1] + up[:, 2:] * w[2] + b


def encoder_layer(x, c, l, w_ada, b_ada, w_in, lambda_qk, subln_g, b_mgate, mnorm_g,
                  w_pa, w_pm, w_out, ln1_g, ln1_b, w_up, conv_w, conv_b, w_down, ln2_g, ln2_b):
    B, S = x.shape[0], x.shape[1]
    lam_init = 0.8 - 0.6 * math.exp(-0.3 * l)
    ada = jax.nn.silu(c) @ w_ada + b_ada
    sh1, sc1, g1, sh2, sc2, g2 = jnp.split(ada[:, None, :], 6, axis=-1)

    h = layer_norm(x) * (1.0 + sc1) + sh1
    proj = h @ w_in
    aq, ak, av, mq, mk, mv, mo, mg, bg = jnp.split(proj, SPLITS, axis=-1)
    lqk = lambda_qk.astype(jnp.float32)
    lam = jnp.exp(jnp.sum(lqk[0] * lqk[1])) - jnp.exp(jnp.sum(lqk[2] * lqk[3])) + lam_init
    ya = diff_attention(aq.reshape(B, S, N_HEADS_A, 2, DQK_A), ak.reshape(B, S, N_HEADS_A, 2, DQK_A),
                        av.reshape(B, S, N_HEADS_A, DV_A), lam, subln_g, lam_init)
    ym = mlstm_bidir(mq, mk, mv, mo, mg, b_mgate, mnorm_g)
    ga, gm = jnp.split(jax.nn.sigmoid(bg), 2, axis=-1)
    mix = (ga * (ya @ w_pa) + gm * (ym @ w_pm)) @ w_out
    x = layer_norm(ALPHA * x + g1 * mix, ln1_g, ln1_b)

    h = layer_norm(x) * (1.0 + sc2) + sh2
    u = h @ w_up
    ug, uv = jnp.split(u, 2, axis=-1)
    ug = dwconv3(ug, conv_w, conv_b)
    y = (jax.nn.gelu(ug) * uv) @ w_down
    return layer_norm(ALPHA * x + g2 * y, ln2_g, ln2_b)


def run_trunk(x, c, w_ada, b_ada, w_in, lambda_qk, subln_g, b_mgate, mnorm_g,
              w_pa, w_pm, w_out, ln1_g, ln1_b, w_up, conv_w, conv_b, w_down, ln2_g, ln2_b):
    for l in range(DEPTH):
        x = encoder_layer(x, c, l, w_ada[l], b_ada[l], w_in[l], lambda_qk[l], subln_g[l], b_mgate[l],
                          mnorm_g[l], w_pa[l], w_pm[l], w_out[l], ln1_g[l], ln1_b[l], w_up[l],
                          conv_w[l], conv_b[l], w_down[l], ln2_g[l], ln2_b[l])
    return x


def setup_inputs(seed: int = 0) -> dict:
    key = jax.random.key(seed)
    ks = jax.random.split(key, 24)
    f32 = jnp.float32
    D = D_MODEL
    nrm = lambda k, shape, s: jax.random.normal(k, shape, f32) * s
    col_scale = jnp.concatenate([
        jnp.full((W_AQ + W_AK,), 1.0, f32), jnp.full((W_AV,), BETA, f32),
        jnp.full((W_MQ + W_MK,), 1.0, f32), jnp.full((W_MV,), BETA, f32),
        jnp.full((W_MO + W_MG + W_BG,), 1.0, f32)])
    b_mgate = nrm(ks[7], (DEPTH, 2, 2, N_HEADS_M), 0.1)
    f_bias = 3.0 + 3.0 * jax.random.uniform(ks[8], (DEPTH, 2, N_HEADS_M), f32)
    b_mgate = b_mgate.at[:, :, 1].add(f_bias)
    return {
        "x_prompt": nrm(ks[0], (BATCH, SEQ, D), 1.0),
        "x_sample": nrm(ks[1], (DEC_BATCH, DEC_SEQ, D), 1.0),
        "c_prompt": nrm(ks[2], (BATCH, D), 1.0),
        "c_sample": nrm(ks[3], (DEC_BATCH, D), 1.0),
        "w_ada": nrm(ks[4], (DEPTH, D, 6 * D), ADA_SCALE * D ** -0.5),
        "b_ada": nrm(ks[5], (DEPTH, 6 * D), 0.02),
        "w_in": nrm(ks[6], (DEPTH, D, D_IN), D ** -0.5) * col_scale,
        "lambda_qk": nrm(ks[9], (DEPTH, 4, DQK_A), 0.1),
        "subln_g": 1.0 + nrm(ks[10], (DEPTH, DV_A), 0.02),
        "b_mgate": b_mgate,
        "mnorm_g": 1.0 + nrm(ks[11], (DEPTH, N_HEADS_M, DV_M), 0.02),
        "w_pa": nrm(ks[12], (DEPTH, N_HEADS_A * DV_A, D), BETA * (N_HEADS_A * DV_A) ** -0.5),
        "w_pm": nrm(ks[13], (DEPTH, N_HEADS_M * DV_M, D), BETA * (N_HEADS_M * DV_M) ** -0.5),
        "w_out": nrm(ks[14], (DEPTH, D, D), BETA * D ** -0.5),
        "ln1_g": 1.0 + nrm(ks[15], (DEPTH, D), 0.02),
        "ln1_b": nrm(ks[16], (DEPTH, D), 0.02),
        "w_up": nrm(ks[17], (DEPTH, D, 2 * D_FF), D ** -0.5),
        "conv_w": nrm(ks[18], (DEPTH, CONV_W, D_FF), CONV_W ** -0.5),
        "conv_b": nrm(ks[19], (DEPTH, D_FF), 0.02),
        "w_down": nrm(ks[20], (DEPTH, D_FF, D), BETA * D_FF ** -0.5),
        "ln2_g": 1.0 + nrm(ks[21], (DEPTH, D), 0.02),
        "ln2_b": nrm(ks[22], (DEPTH, D), 0.02),
    }


def reference(x_prompt, x_sample, c_prompt, c_sample, w_ada, b_ada, w_in, lambda_qk, subln_g,
              b_mgate, mnorm_g, w_pa, w_pm, w_out, ln1_g, ln1_b, w_up, conv_w, conv_b, w_down,
              ln2_g, ln2_b):
    y_prompt = run_trunk(x_prompt, c_prompt, w_ada, b_ada, w_in, lambda_qk, subln_g, b_mgate, mnorm_g,
                         w_pa, w_pm, w_out, ln1_g, ln1_b, w_up, conv_w, conv_b, w_down, ln2_g, ln2_b)
    y_sample = run_trunk(x_sample, c_sample, w_ada, b_ada, w_in, lambda_qk, subln_g, b_mgate, mnorm_g,
                         w_pa, w_pm, w_out, ln1_g, ln1_b, w_up, conv_w, conv_b, w_down, ln2_g, ln2_b)
    return (y_prompt, y_sample)
```

```python
import functools
import math

import jax
import jax.numpy as jnp
import numpy as np
from jax import lax
from jax.experimental import pallas as pl
from jax.experimental.pallas import tpu as pltpu

F32 = jnp.float32
BF16 = jnp.bfloat16

D_MODEL = 1024
N_HEADS_A = 8
DQK_A = 64
DV_A = 2 * DQK_A
ROT_DIM = DQK_A // 4
ROPE_THETA = 500000.0
N_HEADS_M = 4
DQK_M = 128
DV_M = 256
CHUNK = 128
D_FF = 2816
LN_EPS = 1e-5

W_AQ = N_HEADS_A * 2 * DQK_A
W_AK = N_HEADS_A * 2 * DQK_A
W_AV = N_HEADS_A * DV_A
W_MQ = N_HEADS_M * DQK_M
W_MK = N_HEADS_M * DQK_M
W_MV = N_HEADS_M * DV_M
W_MO = N_HEADS_M * DV_M
W_MG = 2 * 2 * N_HEADS_M
W_BG = 2 * D_MODEL
SEG_WIDTHS = (W_AQ, W_AK, W_AV, W_MQ, W_MK, W_MV, W_MO, W_MG, W_BG)
SEG_STARTS = [0] + [int(s) for s in np.cumsum(SEG_WIDTHS)[:-1]]

LANES = 128
SUBLANES = 8
LOG2E = 1.4426950408889634
VMEM_LIMIT = 56 * 1024 * 1024

NT_DIMS = (((1,), (1,)), ((), ()))


def _dot(a, b):
    return jnp.dot(a, b, preferred_element_type=F32)


def _dot_nt(a, b):
    return lax.dot_general(a, b, NT_DIMS, preferred_element_type=F32)


def _norm(x):
    mu = jnp.mean(x, axis=-1, keepdims=True)
    xc = x - mu
    var = jnp.mean(xc * xc, axis=-1, keepdims=True)
    return xc * lax.rsqrt(var + LN_EPS)


def _split_bf16(a):
    hi = a.astype(BF16)
    lo = (a - hi.astype(F32)).astype(BF16)
    return hi, lo


def _const_spec(shape):
    nd = len(shape)
    return pl.BlockSpec(shape, lambda *_: (0,) * nd, pipeline_mode=pl.Buffered(1))


def _params(semantics):
    return pltpu.CompilerParams(dimension_semantics=semantics, vmem_limit_bytes=VMEM_LIMIT)


def _ada_kernel(c_ref, w_ref, b_ref, o_ref):
    c = c_ref[...]
    a = c * jax.nn.sigmoid(c)
    a_hi, a_lo = _split_bf16(a)
    w_hi, w_lo = _split_bf16(w_ref[...])
    o_ref[...] = _dot(a_hi, w_hi) + _dot(a_lo, w_hi) + _dot(a_hi, w_lo) + b_ref[...]


def _ada(c, w_ada, b_ada):
    rows, d = c.shape
    n = w_ada.shape[1]
    tn = 1024
    return pl.pallas_call(
        _ada_kernel,
        grid=(n // tn,),
        in_specs=[pl.BlockSpec((rows, d), lambda j: (0, 0)),
                  pl.BlockSpec((d, tn), lambda j: (0, j)),
                  pl.BlockSpec((1, tn), lambda j: (0, j))],
        out_specs=pl.BlockSpec((rows, tn), lambda j: (0, j)),
        out_shape=jax.ShapeDtypeStruct((rows, n), F32),
        compiler_params=_params(("parallel",)),
        name="ada",
    )(c, w_ada, b_ada.reshape(1, n))


def _log_sigmoid(g):
    return jnp.minimum(g, 0.0) - jnp.log1p(jnp.exp(-jnp.abs(g)))


def _proj_kernel(x_ref, ada_ref, ct_ref, s1_ref, s2_ref,
                 wq_ref, wk_ref, wvt_ref, wmq_ref, wmk_ref, wmv_ref, wmo_ref,
                 wgc_ref, wgr_ref, bgc_ref, bgr_ref, wbg_ref,
                 q_ref, k_ref, vt_ref, mq_ref, mk_ref, mv_ref, og_ref, gc_ref, gr_ref, gab_ref):
    ts = x_ref.shape[1]
    ada = ada_ref[0]
    sh1 = ada[:, 0:D_MODEL]
    sc1 = ada[:, D_MODEL:2 * D_MODEL]
    h = (_norm(x_ref[0]) * (1.0 + sc1) + sh1).astype(BF16)

    ct = ct_ref[...]
    s1 = s1_ref[...]
    s2 = s2_ref[...]

    def rope_store(w_ref, o_ref, scale):
        a = _dot(h, w_ref[...])
        for j in range(W_AQ // LANES):
            blk = a[:, j * LANES:(j + 1) * LANES]
            r = (blk * ct + pltpu.roll(blk, LANES - ROT_DIM // 2, 1) * s1
                 + pltpu.roll(blk, ROT_DIM // 2, 1) * s2)
            o_ref[0, :, j * LANES:(j + 1) * LANES] = (r * scale).astype(o_ref.dtype)

    rope_store(wq_ref, q_ref, (DQK_A ** -0.5) * LOG2E)
    rope_store(wk_ref, k_ref, 1.0)
    vt_ref[0] = _dot_nt(wvt_ref[...], h).astype(vt_ref.dtype)
    mq_ref[0] = _dot(h, wmq_ref[...]).astype(mq_ref.dtype)
    mk_ref[0] = (_dot(h, wmk_ref[...]) * (DQK_M ** -0.5)).astype(mk_ref.dtype)
    mv_ref[0] = _dot(h, wmv_ref[...]).astype(mv_ref.dtype)
    og_ref[0] = jax.nn.sigmoid(_dot(h, wmo_ref[...])).astype(og_ref.dtype)
    gab_ref[0] = jax.nn.sigmoid(_dot(h, wbg_ref[...])).astype(gab_ref.dtype)

    r = lax.broadcasted_iota(jnp.int32, (ts, ts), 0)
    c = lax.broadcasted_iota(jnp.int32, (ts, ts), 1)
    shift = CHUNK.bit_length() - 1
    same = jnp.right_shift(r, shift) == jnp.right_shift(c, shift)
    lower = jnp.where(same & (c <= r), 1.0, 0.0).astype(BF16)
    upper = jnp.where(same & (c >= r), 1.0, 0.0).astype(BF16)

    def gate_parts(g, ch):
        is_f = (ch & (2 * N_HEADS_M - 1)) >= N_HEADS_M
        val = jnp.where(is_f, _log_sigmoid(g), g)
        vf = jnp.where(is_f & (ch < 8), val, 0.0)
        vb = jnp.where(is_f & (ch >= 8) & (ch < W_MG), val, 0.0)
        vi = jnp.where(is_f, 0.0, val)
        return _split_bf16(vf), _split_bf16(vb), vi

    g = _dot(h, wgc_ref[...]) + bgc_ref[...]
    (f_hi, f_lo), (b_hi, b_lo), vi = gate_parts(g, lax.broadcasted_iota(jnp.int32, g.shape, 1))
    gcol = _dot(lower, f_hi) + _dot(lower, f_lo) + _dot(upper, b_hi) + _dot(upper, b_lo) + vi
    gc_ref[0] = gcol[:, 0:W_MG]

    g = _dot_nt(wgr_ref[...], h) + bgr_ref[...]
    (f_hi, f_lo), (b_hi, b_lo), vi = gate_parts(g, lax.broadcasted_iota(jnp.int32, g.shape, 0))
    gr_ref[0] = _dot(f_hi, upper) + _dot(f_lo, upper) + _dot(b_hi, lower) + _dot(b_lo, lower) + vi


def _proj(x, ada3, rope, wts, b_mgate, ts):
    B, S, D = x.shape
    ct, s1, s2 = rope
    wq, wk, wvt, wmq, wmk, wmv, wmo, wgc, wgr, wbg = wts
    bgc = jnp.pad(b_mgate.reshape(1, W_MG).astype(F32), ((0, 0), (0, LANES - W_MG)))
    bgr = b_mgate.reshape(W_MG, 1).astype(F32)
    tok = lambda n: pl.BlockSpec((1, ts, n), lambda b, i: (b, i, 0))
    tab = pl.BlockSpec((ts, LANES), lambda b, i: (i, 0))
    consts = (wq, wk, wvt, wmq, wmk, wmv, wmo, wgc, wgr, bgc, bgr, wbg)
    out_shape = (
        jax.ShapeDtypeStruct((B, S, W_AQ), BF16),
        jax.ShapeDtypeStruct((B, S, W_AK), BF16),
        jax.ShapeDtypeStruct((B, W_AV, S), BF16),
        jax.ShapeDtypeStruct((B, S, W_MQ), BF16),
        jax.ShapeDtypeStruct((B, S, W_MK), BF16),
        jax.ShapeDtypeStruct((B, S, W_MV), BF16),
        jax.ShapeDtypeStruct((B, S, W_MO), BF16),
        jax.ShapeDtypeStruct((B, S, W_MG), F32),
        jax.ShapeDtypeStruct((B, W_MG, S), F32),
        jax.ShapeDtypeStruct((B, S, W_BG), BF16),
    )
    out_specs = (
        tok(W_AQ), tok(W_AK),
        pl.BlockSpec((1, W_AV, ts), lambda b, i: (b, 0, i)),
        tok(W_MQ), tok(W_MK), tok(W_MV), tok(W_MO), tok(W_MG),
        pl.BlockSpec((1, W_MG, ts), lambda b, i: (b, 0, i)),
        tok(W_BG),
    )
    return pl.pallas_call(
        _proj_kernel,
        grid=(B, S // ts),
        in_specs=[tok(D), pl.BlockSpec((1, 1, 6 * D), lambda b, i: (b, 0, 0)), tab, tab, tab]
                 + [_const_spec(w.shape) for w in consts],
        out_specs=out_specs,
        out_shape=out_shape,
        compiler_params=_params(("parallel", "parallel")),
        name="proj",
    )(x, ada3, ct, s1, s2, *consts)


def _attn_kernel(lam_ref, q_ref, k_ref, vt_ref, g_ref, o_ref,
                 acc1, acc2, m1, m2, l1, l2, *, tk, lam_init):
    tq = q_ref.shape[1]
    S = k_ref.shape[1]
    q = q_ref[0]
    lane = lax.broadcasted_iota(jnp.int32, q.shape, 1)
    zero = jnp.zeros_like(q)
    q1 = jnp.where(lane < DQK_A, q, zero)
    q2 = jnp.where(lane >= DQK_A, q, zero)
    for acc, m, l in ((acc1, m1, l1), (acc2, m2, l2)):
        acc[...] = jnp.zeros_like(acc)
        m[...] = jnp.full_like(m, -jnp.inf)
        l[...] = jnp.zeros_like(l)

    def body(j, carry):
        off = pl.multiple_of(j * tk, tk)
        kc = k_ref[0, pl.ds(off, tk), :]
        vc = vt_ref[0, :, pl.ds(off, tk)]
        for qq, acc, m, l in ((q1, acc1, m1, l1), (q2, acc2, m2, l2)):
            s = _dot_nt(kc, qq)
            m_old = m[...]
            m_new = jnp.maximum(m_old, jnp.max(s, axis=0, keepdims=True))
            alpha = jnp.exp2(m_old - m_new)
            p = jnp.exp2(s - m_new)
            l[...] = alpha * l[...] + jnp.sum(p, axis=0, keepdims=True)
            acc[...] = alpha * acc[...] + _dot(vc, p.astype(BF16))
            m[...] = m_new
        return carry

    lax.fori_loop(0, S // tk, body, 0)

    lq = lam_ref[...]
    la = jnp.sum(lq[0:1] * lq[1:2], axis=1, keepdims=True)
    lb = jnp.sum(lq[2:3] * lq[3:4], axis=1, keepdims=True)
    lam = jnp.exp(la) - jnp.exp(lb) + lam_init
    o = acc1[...] / l1[...] - lam * (acc2[...] / l2[...])
    ms = jnp.mean(o * o, axis=0, keepdims=True)
    y = o * lax.rsqrt(ms + LN_EPS) * g_ref[...] * (1.0 - lam_init)
    o_ref[0] = y.T.astype(o_ref.dtype)


def _attn(q, k, vt, lambda_qk, subln_g, lam_init, tq, tk):
    B, S, _ = q.shape
    kern = functools.partial(_attn_kernel, tk=tk, lam_init=lam_init)
    return pl.pallas_call(
        kern,
        grid=(B, N_HEADS_A, S // tq),
        in_specs=[pl.BlockSpec((4, DQK_A), lambda b, h, i: (0, 0)),
                  pl.BlockSpec((1, tq, DV_A), lambda b, h, i: (b, i, h)),
                  pl.BlockSpec((1, S, DV_A), lambda b, h, i: (b, 0, h)),
                  pl.BlockSpec((1, DV_A, S), lambda b, h, i: (b, h, 0)),
                  pl.BlockSpec((DV_A, 1), lambda b, h, i: (0, 0))],
        out_specs=pl.BlockSpec((1, tq, DV_A), lambda b, h, i: (b, i, h)),
        out_shape=jax.ShapeDtypeStruct((B, S, N_HEADS_A * DV_A), BF16),
        scratch_shapes=[pltpu.VMEM((DV_A, tq), F32), pltpu.VMEM((DV_A, tq), F32),
                        pltpu.VMEM((1, tq), F32), pltpu.VMEM((1, tq), F32),
                        pltpu.VMEM((1, tq), F32), pltpu.VMEM((1, tq), F32)],
        compiler_params=_params(("parallel", "parallel", "parallel")),
        name="attn",
    )(lambda_qk.astype(F32), q, k, vt, subln_g.reshape(DV_A, 1).astype(F32))


N_STREAMS = 2 * N_HEADS_M
DV_AUG = DV_M + LANES


def _mlstm_kernel(qf_ref, kf_ref, vf_ref, gcf_ref, grf_ref,
                  qb_ref, kb_ref, vb_ref, gcb_ref, grb_ref,
                  hf_ref, hb_ref, ct_ref, m_ref):
    L = CHUNK

    @pl.when(pl.program_id(1) == 0)
    def _():
        ct_ref[...] = jnp.zeros_like(ct_ref)
        m_ref[...] = jnp.zeros_like(m_ref)

    row = lax.broadcasted_iota(jnp.int32, (L, L), 0)
    col = lax.broadcasted_iota(jnp.int32, (L, L), 1)
    ones_blk = jnp.where(lax.broadcasted_iota(jnp.int32, (L, LANES), 1) == 0, 1.0, 0.0).astype(BF16)
    dirs = ((qf_ref, kf_ref, vf_ref, gcf_ref, grf_ref, hf_ref),
            (qb_ref, kb_ref, vb_ref, gcb_ref, grb_ref, hb_ref))
    for d, (q_ref, k_ref, v_ref, gc_ref, gr_ref, h_ref) in enumerate(dirs):
        tri = (col <= row) if d == 0 else (col >= row)
        e = L - 1 if d == 0 else 0
        gc = gc_ref[0]
        gr = gr_ref[0]
        for hd in range(N_HEADS_M):
            st = d * N_HEADS_M + hd
            ci = d * 2 * N_HEADS_M + hd
            cf = ci + N_HEADS_M
            qh = q_ref[0, :, hd * DQK_M:(hd + 1) * DQK_M]
            kh = k_ref[0, :, hd * DQK_M:(hd + 1) * DQK_M]
            va = jnp.concatenate([v_ref[0, :, hd * DV_M:(hd + 1) * DV_M], ones_blk], axis=1)
            i_row = gr[ci:ci + 1, :]
            b_row = gr[cf:cf + 1, :]
            i_col = gc[:, ci:ci + 1]
            b_col = gc[:, cf:cf + 1]
            m_prev = m_ref[st:st + 1, 0:1]
            dm = jnp.where(tri, b_col - b_row + i_row, -jnp.inf)
            inter = b_col + m_prev
            m_t = jnp.maximum(inter, jnp.max(dm, axis=1, keepdims=True))
            w = jnp.exp(dm - m_t)
            w_inter = jnp.exp(inter - m_t)
            sqk = (_dot_nt(qh, kh) * w).astype(BF16)
            ct = ct_ref[st]
            numa = _dot(sqk, va) + w_inter * _dot(qh, ct.astype(BF16))
            den = numa[:, DV_M:DV_M + 1]
            h = numa[:, 0:DV_M] / jnp.maximum(jnp.abs(den), jnp.exp(-m_t))
            h_ref[0, :, hd * DV_M:(hd + 1) * DV_M] = h.astype(h_ref.dtype)
            b_end = b_row[:, e:e + 1]
            m_new = m_t[e:e + 1, :]
            ws = jnp.exp(b_end - b_col + i_col - m_new)
            decay = jnp.exp(b_end + m_prev - m_new)
            kst = (kh.astype(F32) * ws).T.astype(BF16)
            ct_ref[st] = decay * ct + _dot(kst, va)
            m_ref[st:st + 1, :] = jnp.broadcast_to(m_new, (1, LANES))


def _mlstm(mq, mk, mv, gcol, grow):
    B, S, _ = mq.shape
    nc = S // CHUNK
    fwd = lambda n: pl.BlockSpec((1, CHUNK, n), lambda b, c: (b, c, 0))
    bwd = lambda n: pl.BlockSpec((1, CHUNK, n), lambda b, c: (b, nc - 1 - c, 0))
    rowf = pl.BlockSpec((1, W_MG, CHUNK), lambda b, c: (b, 0, c))
    rowb = pl.BlockSpec((1, W_MG, CHUNK), lambda b, c: (b, 0, nc - 1 - c))
    return pl.pallas_call(
        _mlstm_kernel,
        grid=(B, nc),
        in_specs=[fwd(W_MQ), fwd(W_MK), fwd(W_MV), fwd(W_MG), rowf,
                  bwd(W_MQ), bwd(W_MK), bwd(W_MV), bwd(W_MG), rowb],
        out_specs=(fwd(W_MV), bwd(W_MV)),
        out_shape=(jax.ShapeDtypeStruct((B, S, W_MV), BF16),
                   jax.ShapeDtypeStruct((B, S, W_MV), BF16)),
        scratch_shapes=[pltpu.VMEM((N_STREAMS, DQK_M, DV_AUG), F32),
                        pltpu.VMEM((N_STREAMS, LANES), F32)],
        compiler_params=_params(("parallel", "arbitrary")),
        name="mlstm",
    )(mq, mk, mv, gcol, grow, mq, mk, mv, gcol, grow)


def _mix_kernel(x_ref, ada_ref, ya_ref, hf_ref, hb_ref, og_ref, gab_ref,
                wpa_ref, wpm_ref, wout_ref, mng_ref, lg_ref, lb_ref, o_ref, *, alpha):
    hs = hf_ref[0].astype(F32) + hb_ref[0].astype(F32)
    hn = jnp.concatenate(
        [_norm(hs[:, hd * DV_M:(hd + 1) * DV_M]) for hd in range(N_HEADS_M)], axis=1)
    ym = (og_ref[0].astype(F32) * (hn * mng_ref[...])).astype(BF16)
    gab = gab_ref[0].astype(F32)
    pre = gab[:, 0:D_MODEL] * _dot(ya_ref[0], wpa_ref[...]) + gab[:, D_MODEL:] * _dot(ym, wpm_ref[...])
    mix = _dot(pre.astype(BF16), wout_ref[...])
    g1 = ada_ref[0][:, 2 * D_MODEL:3 * D_MODEL]
    o_ref[0] = _norm(alpha * x_ref[0] + g1 * mix) * lg_ref[...] + lb_ref[...]


def _mix(x, ada3, ya, hf, hb, og, gab, wpa, wpm, wout, mnorm_g, ln_g, ln_b, alpha, ts):
    B, S, D = x.shape
    tok = lambda n: pl.BlockSpec((1, ts, n), lambda b, i: (b, i, 0))
    consts = (wpa, wpm, wout, mnorm_g.reshape(1, W_MV).astype(F32),
              ln_g.reshape(1, D).astype(F32), ln_b.reshape(1, D).astype(F32))
    return pl.pallas_call(
        functools.partial(_mix_kernel, alpha=alpha),
        grid=(B, S // ts),
        in_specs=[tok(D), pl.BlockSpec((1, 1, 6 * D), lambda b, i: (b, 0, 0)),
                  tok(W_AV), tok(W_MV), tok(W_MV), tok(W_MO), tok(W_BG)]
                 + [_const_spec(w.shape) for w in consts],
        out_specs=tok(D),
        out_shape=jax.ShapeDtypeStruct((B, S, D), F32),
        compiler_params=_params(("parallel", "parallel")),
        name="mix",
    )(x, ada3, ya, hf, hb, og, gab, *consts)


def _gelu_tanh(x):
    return 0.5 * x * (1.0 + jnp.tanh(math.sqrt(2.0 / math.pi) * (x + 0.044715 * (x * x * x))))


def _ffn_kernel(xp_ref, x_ref, xn_ref, ada_ref, wug_ref, wuv_ref, cw_ref, cb_ref, wd_ref,
                lg_ref, lb_ref, o_ref, *, alpha):
    ts = x_ref.shape[1]
    H = SUBLANES
    i = pl.program_id(1)
    ada = ada_ref[0]
    sh2 = ada[:, 3 * D_MODEL:4 * D_MODEL]
    sc2 = ada[:, 4 * D_MODEL:5 * D_MODEL]
    g2 = ada[:, 5 * D_MODEL:6 * D_MODEL]
    x = x_ref[0]
    xx = jnp.concatenate([xp_ref[0], x, xn_ref[0]], axis=0)
    hx = _norm(xx) * (1.0 + sc2) + sh2
    ug = _dot(hx.astype(BF16), wug_ref[...])
    uv = _dot(hx[H:H + ts].astype(BF16), wuv_ref[...])
    n = ts + 2 * H
    prev = pltpu.roll(ug, 1, 0)[H:H + ts]
    nxt = pltpu.roll(ug, n - 1, 0)[H:H + ts]
    t = lax.broadcasted_iota(jnp.int32, (ts, 1), 0)
    keep_prev = jnp.where((i == 0) & (t == 0), 0.0, 1.0)
    keep_next = jnp.where((i == pl.num_programs(1) - 1) & (t == ts - 1), 0.0, 1.0)
    cw = cw_ref[...]
    conv = (prev * keep_prev * cw[0:1] + ug[H:H + ts] * cw[1:2] + nxt * keep_next * cw[2:3]
            + cb_ref[...])
    y = _dot((_gelu_tanh(conv) * uv).astype(BF16), wd_ref[...])
    o_ref[0] = _norm(alpha * x + g2 * y) * lg_ref[...] + lb_ref[...]


def _ffn(x, ada3, wug, wuv, conv_w, conv_b, wd, ln_g, ln_b, alpha, ts):
    B, S, D = x.shape
    H = SUBLANES
    nb = ts // H
    last = S // H - 1
    tok = pl.BlockSpec((1, ts, D), lambda b, i: (b, i, 0))
    prev = pl.BlockSpec((1, H, D), lambda b, i: (b, jnp.maximum(i * nb - 1, 0), 0))
    nxt = pl.BlockSpec((1, H, D), lambda b, i: (b, jnp.minimum((i + 1) * nb, last), 0))
    consts = (wug, wuv, conv_w.astype(F32), conv_b.reshape(1, D_FF).astype(F32), wd,
              ln_g.reshape(1, D).astype(F32), ln_b.reshape(1, D).astype(F32))
    return pl.pallas_call(
        functools.partial(_ffn_kernel, alpha=alpha),
        grid=(B, S // ts),
        in_specs=[prev, tok, nxt, pl.BlockSpec((1, 1, 6 * D), lambda b, i: (b, 0, 0))]
                 + [_const_spec(w.shape) for w in consts],
        out_specs=tok,
        out_shape=jax.ShapeDtypeStruct((B, S, D), F32),
        compiler_params=_params(("parallel", "parallel")),
        name="ffn",
    )(x, x, x, ada3, *consts)


def _rope_tables(seq):
    half = ROT_DIM // 2
    inv = ROPE_THETA ** (-jnp.arange(0, ROT_DIM, 2, dtype=F32) / ROT_DIM)
    ang = jnp.arange(seq, dtype=F32)[:, None] * inv[None, :]
    cos, sin = jnp.cos(ang), jnp.sin(ang)
    rest = DQK_A - ROT_DIM
    one = jnp.ones((seq, rest), F32)
    zr = jnp.zeros((seq, rest), F32)
    zh = jnp.zeros((seq, half), F32)
    reps = LANES // DQK_A
    ct = jnp.tile(jnp.concatenate([cos, cos, one], axis=1), (1, reps))
    s1 = jnp.tile(jnp.concatenate([-sin, zh, zr], axis=1), (1, reps))
    s2 = jnp.tile(jnp.concatenate([zh, sin, zr], axis=1), (1, reps))
    return ct, s1, s2


def _layer_weights(w_in, w_pa, w_pm, w_out, w_up, w_down):
    seg = lambda j: w_in[:, SEG_STARTS[j]:SEG_STARTS[j] + SEG_WIDTHS[j]]
    wg = seg(7)
    proj = (seg(0).astype(BF16), seg(1).astype(BF16), seg(2).T.astype(BF16),
            seg(3).astype(BF16), seg(4).astype(BF16), seg(5).astype(BF16), seg(6).astype(BF16),
            jnp.pad(wg, ((0, 0), (0, LANES - W_MG))).astype(BF16), wg.T.astype(BF16),
            seg(8).astype(BF16))
    return (proj, w_pa.astype(BF16), w_pm.astype(BF16), w_out.astype(BF16),
            w_up[:, :D_FF].astype(BF16), w_up[:, D_FF:].astype(BF16), w_down.astype(BF16))


def _tile(seq, want):
    t = min(seq, want)
    assert seq % t == 0, (seq, want)
    return t


def _encoder_layer(x, ada, l, depth, weights, lambda_qk, subln_g, b_mgate, mnorm_g,
                   ln1_g, ln1_b, conv_w, conv_b, ln2_g, ln2_b, tiles):
    B, S, D = x.shape
    ts, tq, tk = (_tile(S, t) for t in tiles)
    alpha = (2.0 * depth) ** 0.25
    lam_init = 0.8 - 0.6 * math.exp(-0.3 * l)
    proj_w, wpa, wpm, wout, wug, wuv, wd = weights
    ada3 = ada.reshape(B, 1, 6 * D)
    q, k, vt, mq, mk, mv, og, gcol, grow, gab = _proj(x, ada3, _rope_tables(S), proj_w, b_mgate, ts)
    ya = _attn(q, k, vt, lambda_qk, subln_g, lam_init, tq, tk)
    hf, hb = _mlstm(mq, mk, mv, gcol, grow)
    x1 = _mix(x, ada3, ya, hf, hb, og, gab, wpa, wpm, wout, mnorm_g, ln1_g, ln1_b, alpha, ts)
    return _ffn(x1, ada3, wug, wuv, conv_w, conv_b, wd, ln2_g, ln2_b, alpha, ts)


def _trunk(xs, cs, w_ada, b_ada, w_in, lambda_qk, subln_g, b_mgate, mnorm_g, w_pa, w_pm, w_out,
           ln1_g, ln1_b, w_up, conv_w, conv_b, w_down, ln2_g, ln2_b, tiles=(512, 256, 512)):
    depth = w_in.shape[0]
    rows = [c.shape[0] for c in cs]
    c_all = jnp.concatenate(cs, axis=0)
    xs = list(xs)
    for l in range(depth):
        ada_all = _ada(c_all, w_ada[l], b_ada[l])
        weights = _layer_weights(w_in[l], w_pa[l], w_pm[l], w_out[l], w_up[l], w_down[l])
        start = 0
        for g, n in enumerate(rows):
            xs[g] = _encoder_layer(xs[g], ada_all[start:start + n], l, depth, weights,
                                   lambda_qk[l], subln_g[l], b_mgate[l], mnorm_g[l],
                                   ln1_g[l], ln1_b[l], conv_w[l], conv_b[l], ln2_g[l], ln2_b[l], tiles)
            start += n
    return tuple(xs)


def kernel(x_prompt, x_sample, c_prompt, c_sample, w_ada, b_ada, w_in, lambda_qk, subln_g, b_mgate, mnorm_g, w_pa, w_pm, w_out, ln1_g, ln1_b, w_up, conv_w, conv_b, w_down, ln2_g, ln2_b):
    return _trunk((x_prompt, x_sample), (c_prompt, c_sample), w_ada, b_ada, w_in, lambda_qk, subln_g,
                  b_mgate, mnorm_g, w_pa, w_pm, w_out, ln1_g, ln1_b, w_up, conv_w, conv_b, w_down,
                  ln2_g, ln2_b)
```

```python
import functools
import math

import jax
import jax.numpy as jnp
import numpy as np
from jax import lax
from jax.experimental import pallas as pl
from jax.experimental.pallas import tpu as pltpu

F32 = jnp.float32
BF16 = jnp.bfloat16

D_MODEL = 1024
N_HEADS_A = 8
DQK_A = 64
DV_A = 2 * DQK_A
ROT_DIM = DQK_A // 4
ROPE_THETA = 500000.0
N_HEADS_M = 4
DQK_M = 128
DV_M = 256
CHUNK = 128
D_FF = 2816
LN_EPS = 1e-5

W_AQ = N_HEADS_A * 2 * DQK_A
W_AK = N_HEADS_A * 2 * DQK_A
W_AV = N_HEADS_A * DV_A
W_MQ = N_HEADS_M * DQK_M
W_MK = N_HEADS_M * DQK_M
W_MV = N_HEADS_M * DV_M
W_MO = N_HEADS_M * DV_M
W_MG = 2 * 2 * N_HEADS_M
W_BG = 2 * D_MODEL
SEG_WIDTHS = (W_AQ, W_AK, W_AV, W_MQ, W_MK, W_MV, W_MO, W_MG, W_BG)
SEG_STARTS = [0] + [int(s) for s in np.cumsum(SEG_WIDTHS)[:-1]]

LANES = 128
SUBLANES = 8
LOG2E = 1.4426950408889634
VMEM_LIMIT = 56 * 1024 * 1024

NT_DIMS = (((1,), (1,)), ((), ()))


def _dot(a, b):
    return jnp.dot(a, b, preferred_element_type=F32)


def _dot_nt(a, b):
    return lax.dot_general(a, b, NT_DIMS, preferred_element_type=F32)


def _norm(x):
    mu = jnp.mean(x, axis=-1, keepdims=True)
    xc = x - mu
    var = jnp.mean(xc * xc, axis=-1, keepdims=True)
    return xc * lax.rsqrt(var + LN_EPS)


def _split_bf16(a):
    hi = a.astype(BF16)
    lo = (a - hi.astype(F32)).astype(BF16)
    return hi, lo


def _const_spec(shape):
    nd = len(shape)
    return pl.BlockSpec(shape, lambda *_: (0,) * nd, pipeline_mode=pl.Buffered(1))


def _params(semantics):
    return pltpu.CompilerParams(dimension_semantics=semantics, vmem_limit_bytes=VMEM_LIMIT)


def _ada_kernel(c_ref, w_ref, b_ref, o_ref):
    c = c_ref[...]
    a = c * jax.nn.sigmoid(c)
    a_hi, a_lo = _split_bf16(a)
    w_hi, w_lo = _split_bf16(w_ref[...])
    o_ref[...] = _dot(a_hi, w_hi) + _dot(a_lo, w_hi) + _dot(a_hi, w_lo) + b_ref[...]


def _ada(c, w_ada, b_ada):
    rows, d = c.shape
    n = w_ada.shape[1]
    tn = 1024
    return pl.pallas_call(
        _ada_kernel,
        grid=(n // tn,),
        in_specs=[pl.BlockSpec((rows, d), lambda j: (0, 0)),
                  pl.BlockSpec((d, tn), lambda j: (0, j)),
                  pl.BlockSpec((1, tn), lambda j: (0, j))],
        out_specs=pl.BlockSpec((rows, tn), lambda j: (0, j)),
        out_shape=jax.ShapeDtypeStruct((rows, n), F32),
        compiler_params=_params(("parallel",)),
        name="ada",
    )(c, w_ada, b_ada.reshape(1, n))


def _log_sigmoid(g):
    return jnp.minimum(g, 0.0) - jnp.log1p(jnp.exp(-jnp.abs(g)))


def _proj_kernel(x_ref, ada_ref, ct_ref, s1_ref, s2_ref,
                 wq_ref, wk_ref, wvt_ref, wmq_ref, wmk_ref, wmv_ref, wmo_ref,
                 wgc_ref, wgr_ref, bgc_ref, bgr_ref, wbg_ref,
                 q_ref, k_ref, vt_ref, mq_ref, mk_ref, mv_ref, og_ref, gc_ref, gr_ref, gab_ref):
    ts = x_ref.shape[1]
    ada = ada_ref[0]
    sh1 = ada[:, 0:D_MODEL]
    sc1 = ada[:, D_MODEL:2 * D_MODEL]
    h = (_norm(x_ref[0]) * (1.0 + sc1) + sh1).astype(BF16)

    ct = ct_ref[...]
    s1 = s1_ref[...]
    s2 = s2_ref[...]

    def rope_store(w_ref, o_ref, scale):
        a = _dot(h, w_ref[...])
        for j in range(W_AQ // LANES):
            blk = a[:, j * LANES:(j + 1) * LANES]
            r = (blk * ct + pltpu.roll(blk, LANES - ROT_DIM // 2, 1) * s1
                 + pltpu.roll(blk, ROT_DIM // 2, 1) * s2)
            o_ref[0, :, j * LANES:(j + 1) * LANES] = (r * scale).astype(o_ref.dtype)

    rope_store(wq_ref, q_ref, (DQK_A ** -0.5) * LOG2E)
    rope_store(wk_ref, k_ref, 1.0)
    vt_ref[0] = _dot_nt(wvt_ref[...], h).astype(vt_ref.dtype)
    mq_ref[0] = _dot(h, wmq_ref[...]).astype(mq_ref.dtype)
    mk_ref[0] = (_dot(h, wmk_ref[...]) * (DQK_M ** -0.5)).astype(mk_ref.dtype)
    mv_ref[0] = _dot(h, wmv_ref[...]).astype(mv_ref.dtype)
    og_ref[0] = jax.nn.sigmoid(_dot(h, wmo_ref[...])).astype(og_ref.dtype)
    gab_ref[0] = jax.nn.sigmoid(_dot(h, wbg_ref[...])).astype(gab_ref.dtype)

    r = lax.broadcasted_iota(jnp.int32, (ts, ts), 0)
    c = lax.broadcasted_iota(jnp.int32, (ts, ts), 1)
    shift = CHUNK.bit_length() - 1
    same = jnp.right_shift(r, shift) == jnp.right_shift(c, shift)
    lower = jnp.where(same & (c <= r), 1.0, 0.0).astype(BF16)
    upper = jnp.where(same & (c >= r), 1.0, 0.0).astype(BF16)

    def gate_parts(g, ch):
        is_f = (ch & (2 * N_HEADS_M - 1)) >= N_HEADS_M
        val = jnp.where(is_f, _log_sigmoid(g), g)
        vf = jnp.where(is_f & (ch < 8), val, 0.0)
        vb = jnp.where(is_f & (ch >= 8) & (ch < W_MG), val, 0.0)
        vi = jnp.where(is_f, 0.0, val)
        return _split_bf16(vf), _split_bf16(vb), vi

    g = _dot(h, wgc_ref[...]) + bgc_ref[...]
    (f_hi, f_lo), (b_hi, b_lo), vi = gate_parts(g, lax.broadcasted_iota(jnp.int32, g.shape, 1))
    gcol = _dot(lower, f_hi) + _dot(lower, f_lo) + _dot(upper, b_hi) + _dot(upper, b_lo) + vi
    gc_ref[0] = gcol[:, 0:W_MG]

    g = _dot_nt(wgr_ref[...], h) + bgr_ref[...]
    (f_hi, f_lo), (b_hi, b_lo), vi = gate_parts(g, lax.broadcasted_iota(jnp.int32, g.shape, 0))
    gr_ref[0] = _dot(f_hi, upper) + _dot(f_lo, upper) + _dot(b_hi, lower) + _dot(b_lo, lower) + vi


def _proj(x, ada3, rope, wts, b_mgate, ts):
    B, S, D = x.shape
    ct, s1, s2 = rope
    wq, wk, wvt, wmq, wmk, wmv, wmo, wgc, wgr, wbg = wts
    bgc = jnp.pad(b_mgate.reshape(1, W_MG).astype(F32), ((0, 0), (0, LANES - W_MG)))
    bgr = b_mgate.reshape(W_MG, 1).astype(F32)
    tok = lambda n: pl.BlockSpec((1, ts, n), lambda b, i: (b, i, 0))
    tab = pl.BlockSpec((ts, LANES), lambda b, i: (i, 0))
    consts = (wq, wk, wvt, wmq, wmk, wmv, wmo, wgc, wgr, bgc, bgr, wbg)
    out_shape = (
        jax.ShapeDtypeStruct((B, S, W_AQ), BF16),
        jax.ShapeDtypeStruct((B, S, W_AK), BF16),
        jax.ShapeDtypeStruct((B, W_AV, S), BF16),
        jax.ShapeDtypeStruct((B, S, W_MQ), BF16),
        jax.ShapeDtypeStruct((B, S, W_MK), BF16),
        jax.ShapeDtypeStruct((B, S, W_MV), BF16),
        jax.ShapeDtypeStruct((B, S, W_MO), BF16),
        jax.ShapeDtypeStruct((B, S, W_MG), F32),
        jax.ShapeDtypeStruct((B, W_MG, S), F32),
        jax.ShapeDtypeStruct((B, S, W_BG), BF16),
    )
    out_specs = (
        tok(W_AQ), tok(W_AK),
        pl.BlockSpec((1, W_AV, ts), lambda b, i: (b, 0, i)),
        tok(W_MQ), tok(W_MK), tok(W_MV), tok(W_MO), tok(W_MG),
        pl.BlockSpec((1, W_MG, ts), lambda b, i: (b, 0, i)),
        tok(W_BG),
    )
    return pl.pallas_call(
        _proj_kernel,
        grid=(B, S // ts),
        in_specs=[tok(D), pl.BlockSpec((1, 1, 6 * D), lambda b, i: (b, 0, 0)), tab, tab, tab]
                 + [_const_spec(w.shape) for w in consts],
        out_specs=out_specs,
        out_shape=out_shape,
        compiler_params=_params(("parallel", "parallel")),
        name="proj",
    )(x, ada3, ct, s1, s2, *consts)


ONES_ROWS = 16


def _attn_kernel(lam_ref, q_ref, k_ref, vt_ref, g_ref, o_ref,
                 s_a, s_b, p_a, p_b, acc, m_ref, *, tk, lam_init):
    n = k_ref.shape[1] // tk
    q = q_ref[0]
    lane = lax.broadcasted_iota(jnp.int32, q.shape, 1)
    zero = jnp.zeros_like(q)
    qs = (jnp.where(lane < DQK_A, q, zero), jnp.where(lane >= DQK_A, q, zero))
    ones_rows = jnp.ones((ONES_ROWS, tk), BF16)

    acc[...] = jnp.zeros_like(acc)
    m_ref[...] = jnp.full_like(m_ref, -jnp.inf)

    def scores(j, s_out):
        kc = k_ref[0, pl.ds(pl.multiple_of(j * tk, tk), tk), :]
        for mp in range(2):
            s_out[mp] = _dot_nt(kc, qs[mp])

    def softmax(s_in, p_out):
        alphas = []
        for mp in range(2):
            s = s_in[mp]
            m_old = m_ref[mp:mp + 1, :]
            m_new = jnp.maximum(m_old, jnp.max(s, axis=0, keepdims=True))
            alphas.append(jnp.exp2(m_old - m_new))
            p_out[mp] = jnp.exp2((s - m_new).astype(BF16))
            m_ref[mp:mp + 1, :] = m_new
        return tuple(alphas)

    def values(j, p_in, alphas):
        vc = vt_ref[0, :, pl.ds(pl.multiple_of(j * tk, tk), tk)]
        va = jnp.concatenate([vc, ones_rows], axis=0)
        for mp in range(2):
            acc[mp] = alphas[mp] * acc[mp] + _dot(va, p_in[mp])

    scores(0, s_a)
    scores(1, s_b)
    alpha_a = softmax(s_a, p_a)

    def body(t, alpha_a):
        j = 2 * t + 1
        scores(j + 1, s_a)
        alpha_b = softmax(s_b, p_b)
        values(j - 1, p_a, alpha_a)
        scores(j + 2, s_b)
        alpha_a = softmax(s_a, p_a)
        values(j, p_b, alpha_b)
        return alpha_a

    alpha_a = lax.fori_loop(0, n // 2 - 1, body, alpha_a)
    alpha_b = softmax(s_b, p_b)
    values(n - 2, p_a, alpha_a)
    values(n - 1, p_b, alpha_b)

    lq = lam_ref[...]
    la = jnp.sum(lq[0:1] * lq[1:2], axis=1, keepdims=True)
    lb = jnp.sum(lq[2:3] * lq[3:4], axis=1, keepdims=True)
    lam = jnp.exp(la) - jnp.exp(lb) + lam_init
    a1 = acc[0]
    a2 = acc[1]
    o = a1[0:DV_A] / a1[DV_A:DV_A + 1] - lam * (a2[0:DV_A] / a2[DV_A:DV_A + 1])
    ms = jnp.mean(o * o, axis=0, keepdims=True)
    y = o * lax.rsqrt(ms + LN_EPS) * g_ref[...] * (1.0 - lam_init)
    o_ref[0] = y.T.astype(o_ref.dtype)


def _attn(q, k, vt, lambda_qk, subln_g, lam_init, tq, tk):
    B, S, _ = q.shape
    assert (S // tk) % 2 == 0, (S, tk)
    kern = functools.partial(_attn_kernel, tk=tk, lam_init=lam_init)
    return pl.pallas_call(
        kern,
        grid=(B, N_HEADS_A, S // tq),
        in_specs=[pl.BlockSpec((4, DQK_A), lambda b, h, i: (0, 0)),
                  pl.BlockSpec((1, tq, DV_A), lambda b, h, i: (b, i, h)),
                  pl.BlockSpec((1, S, DV_A), lambda b, h, i: (b, 0, h)),
                  pl.BlockSpec((1, DV_A, S), lambda b, h, i: (b, h, 0)),
                  pl.BlockSpec((DV_A, 1), lambda b, h, i: (0, 0))],
        out_specs=pl.BlockSpec((1, tq, DV_A), lambda b, h, i: (b, i, h)),
        out_shape=jax.ShapeDtypeStruct((B, S, N_HEADS_A * DV_A), BF16),
        scratch_shapes=[pltpu.VMEM((2, tk, tq), F32), pltpu.VMEM((2, tk, tq), F32),
                        pltpu.VMEM((2, tk, tq), BF16), pltpu.VMEM((2, tk, tq), BF16),
                        pltpu.VMEM((2, DV_A + ONES_ROWS, tq), F32),
                        pltpu.VMEM((2, tq), F32)],
        compiler_params=_params(("parallel", "parallel", "parallel")),
        name="attn",
    )(lambda_qk.astype(F32), q, k, vt, subln_g.reshape(DV_A, 1).astype(F32))


N_STREAMS = 2 * N_HEADS_M
DV_AUG = DV_M + LANES


def _mlstm_kernel(qf_ref, kf_ref, vf_ref, gcf_ref, grf_ref,
                  qb_ref, kb_ref, vb_ref, gcb_ref, grb_ref,
                  hf_ref, hb_ref, ct_ref, m_ref):
    L = CHUNK

    @pl.when(pl.program_id(1) == 0)
    def _():
        ct_ref[...] = jnp.zeros_like(ct_ref)
        m_ref[...] = jnp.zeros_like(m_ref)

    row = lax.broadcasted_iota(jnp.int32, (L, L), 0)
    col = lax.broadcasted_iota(jnp.int32, (L, L), 1)
    ones_blk = jnp.where(lax.broadcasted_iota(jnp.int32, (L, LANES), 1) == 0, 1.0, 0.0).astype(BF16)
    dirs = ((qf_ref, kf_ref, vf_ref, gcf_ref, grf_ref, hf_ref),
            (qb_ref, kb_ref, vb_ref, gcb_ref, grb_ref, hb_ref))
    for d, (q_ref, k_ref, v_ref, gc_ref, gr_ref, h_ref) in enumerate(dirs):
        tri = (col <= row) if d == 0 else (col >= row)
        e = L - 1 if d == 0 else 0
        gc = gc_ref[0]
        gr = gr_ref[0]
        for hd in range(N_HEADS_M):
            st = d * N_HEADS_M + hd
            ci = d * 2 * N_HEADS_M + hd
            cf = ci + N_HEADS_M
            qh = q_ref[0, :, hd * DQK_M:(hd + 1) * DQK_M]
            kh = k_ref[0, :, hd * DQK_M:(hd + 1) * DQK_M]
            va = jnp.concatenate([v_ref[0, :, hd * DV_M:(hd + 1) * DV_M], ones_blk], axis=1)
            i_row = gr[ci:ci + 1, :]
            b_row = gr[cf:cf + 1, :]
            i_col = gc[:, ci:ci + 1]
            b_col = gc[:, cf:cf + 1]
            m_prev = m_ref[st:st + 1, 0:1]
            dm = jnp.where(tri, b_col - b_row + i_row, -jnp.inf)
            inter = b_col + m_prev
            m_t = jnp.maximum(inter, jnp.max(dm, axis=1, keepdims=True))
            w = jnp.exp(dm - m_t)
            w_inter = jnp.exp(inter - m_t)
            sqk = (_dot_nt(qh, kh) * w).astype(BF16)
            ct = ct_ref[st]
            numa = _dot(sqk, va) + w_inter * _dot(qh, ct.astype(BF16))
            den = numa[:, DV_M:DV_M + 1]
            h = numa[:, 0:DV_M] / jnp.maximum(jnp.abs(den), jnp.exp(-m_t))
            h_ref[0, :, hd * DV_M:(hd + 1) * DV_M] = h.astype(h_ref.dtype)
            b_end = b_row[:, e:e + 1]
            m_new = m_t[e:e + 1, :]
            ws = jnp.exp(b_end - b_col + i_col - m_new)
            decay = jnp.exp(b_end + m_prev - m_new)
            kst = (kh.astype(F32) * ws).T.astype(BF16)
            ct_ref[st] = decay * ct + _dot(kst, va)
            m_ref[st:st + 1, :] = jnp.broadcast_to(m_new, (1, LANES))


def _mlstm(mq, mk, mv, gcol, grow):
    B, S, _ = mq.shape
    nc = S // CHUNK
    fwd = lambda n: pl.BlockSpec((1, CHUNK, n), lambda b, c: (b, c, 0))
    bwd = lambda n: pl.BlockSpec((1, CHUNK, n), lambda b, c: (b, nc - 1 - c, 0))
    rowf = pl.BlockSpec((1, W_MG, CHUNK), lambda b, c: (b, 0, c))
    rowb = pl.BlockSpec((1, W_MG, CHUNK), lambda b, c: (b, 0, nc - 1 - c))
    return pl.pallas_call(
        _mlstm_kernel,
        grid=(B, nc),
        in_specs=[fwd(W_MQ), fwd(W_MK), fwd(W_MV), fwd(W_MG), rowf,
                  bwd(W_MQ), bwd(W_MK), bwd(W_MV), bwd(W_MG), rowb],
        out_specs=(fwd(W_MV), bwd(W_MV)),
        out_shape=(jax.ShapeDtypeStruct((B, S, W_MV), BF16),
                   jax.ShapeDtypeStruct((B, S, W_MV), BF16)),
        scratch_shapes=[pltpu.VMEM((N_STREAMS, DQK_M, DV_AUG), F32),
                        pltpu.VMEM((N_STREAMS, LANES), F32)],
        compiler_params=_params(("parallel", "arbitrary")),
        name="mlstm",
    )(mq, mk, mv, gcol, grow, mq, mk, mv, gcol, grow)


def _mix_kernel(x_ref, ada_ref, ya_ref, hf_ref, hb_ref, og_ref, gab_ref,
                wpa_ref, wpm_ref, wout_ref, mng_ref, lg_ref, lb_ref, o_ref, *, alpha):
    hs = hf_ref[0].astype(F32) + hb_ref[0].astype(F32)
    hn = jnp.concatenate(
        [_norm(hs[:, hd * DV_M:(hd + 1) * DV_M]) for hd in range(N_HEADS_M)], axis=1)
    ym = (og_ref[0].astype(F32) * (hn * mng_ref[...])).astype(BF16)
    gab = gab_ref[0].astype(F32)
    pre = gab[:, 0:D_MODEL] * _dot(ya_ref[0], wpa_ref[...]) + gab[:, D_MODEL:] * _dot(ym, wpm_ref[...])
    mix = _dot(pre.astype(BF16), wout_ref[...])
    g1 = ada_ref[0][:, 2 * D_MODEL:3 * D_MODEL]
    o_ref[0] = _norm(alpha * x_ref[0] + g1 * mix) * lg_ref[...] + lb_ref[...]


def _mix(x, ada3, ya, hf, hb, og, gab, wpa, wpm, wout, mnorm_g, ln_g, ln_b, alpha, ts):
    B, S, D = x.shape
    tok = lambda n: pl.BlockSpec((1, ts, n), lambda b, i: (b, i, 0))
    consts = (wpa, wpm, wout, mnorm_g.reshape(1, W_MV).astype(F32),
              ln_g.reshape(1, D).astype(F32), ln_b.reshape(1, D).astype(F32))
    return pl.pallas_call(
        functools.partial(_mix_kernel, alpha=alpha),
        grid=(B, S // ts),
        in_specs=[tok(D), pl.BlockSpec((1, 1, 6 * D), lambda b, i: (b, 0, 0)),
                  tok(W_AV), tok(W_MV), tok(W_MV), tok(W_MO), tok(W_BG)]
                 + [_const_spec(w.shape) for w in consts],
        out_specs=tok(D),
        out_shape=jax.ShapeDtypeStruct((B, S, D), F32),
        compiler_params=_params(("parallel", "parallel")),
        name="mix",
    )(x, ada3, ya, hf, hb, og, gab, *consts)


def _gelu_tanh(x):
    return 0.5 * x * (1.0 + jnp.tanh(math.sqrt(2.0 / math.pi) * (x + 0.044715 * (x * x * x))))


def _ffn_kernel(xp_ref, x_ref, xn_ref, ada_ref, wug_ref, wuv_ref, cw_ref, cb_ref, wd_ref,
                lg_ref, lb_ref, o_ref, *, alpha):
    ts = x_ref.shape[1]
    H = SUBLANES
    i = pl.program_id(1)
    ada = ada_ref[0]
    sh2 = ada[:, 3 * D_MODEL:4 * D_MODEL]
    sc2 = ada[:, 4 * D_MODEL:5 * D_MODEL]
    g2 = ada[:, 5 * D_MODEL:6 * D_MODEL]
    x = x_ref[0]
    xx = jnp.concatenate([xp_ref[0], x, xn_ref[0]], axis=0)
    hx = _norm(xx) * (1.0 + sc2) + sh2
    ug = _dot(hx.astype(BF16), wug_ref[...])
    uv = _dot(hx[H:H + ts].astype(BF16), wuv_ref[...])
    n = ts + 2 * H
    prev = pltpu.roll(ug, 1, 0)[H:H + ts]
    nxt = pltpu.roll(ug, n - 1, 0)[H:H + ts]
    t = lax.broadcasted_iota(jnp.int32, (ts, 1), 0)
    keep_prev = jnp.where((i == 0) & (t == 0), 0.0, 1.0)
    keep_next = jnp.where((i == pl.num_programs(1) - 1) & (t == ts - 1), 0.0, 1.0)
    cw = cw_ref[...]
    conv = (prev * keep_prev * cw[0:1] + ug[H:H + ts] * cw[1:2] + nxt * keep_next * cw[2:3]
            + cb_ref[...])
    y = _dot((_gelu_tanh(conv) * uv).astype(BF16), wd_ref[...])
    o_ref[0] = _norm(alpha * x + g2 * y) * lg_ref[...] + lb_ref[...]


def _ffn(x, ada3, wug, wuv, conv_w, conv_b, wd, ln_g, ln_b, alpha, ts):
    B, S, D = x.shape
    H = SUBLANES
    nb = ts // H
    last = S // H - 1
    tok = pl.BlockSpec((1, ts, D), lambda b, i: (b, i, 0))
    prev = pl.BlockSpec((1, H, D), lambda b, i: (b, jnp.maximum(i * nb - 1, 0), 0))
    nxt = pl.BlockSpec((1, H, D), lambda b, i: (b, jnp.minimum((i + 1) * nb, last), 0))
    consts = (wug, wuv, conv_w.astype(F32), conv_b.reshape(1, D_FF).astype(F32), wd,
              ln_g.reshape(1, D).astype(F32), ln_b.reshape(1, D).astype(F32))
    return pl.pallas_call(
        functools.partial(_ffn_kernel, alpha=alpha),
        grid=(B, S // ts),
        in_specs=[prev, tok, nxt, pl.BlockSpec((1, 1, 6 * D), lambda b, i: (b, 0, 0))]
                 + [_const_spec(w.shape) for w in consts],
        out_specs=tok,
        out_shape=jax.ShapeDtypeStruct((B, S, D), F32),
        compiler_params=_params(("parallel", "parallel")),
        name="ffn",
    )(x, x, x, ada3, *consts)


def _rope_tables(seq):
    half = ROT_DIM // 2
    inv = ROPE_THETA ** (-jnp.arange(0, ROT_DIM, 2, dtype=F32) / ROT_DIM)
    ang = jnp.arange(seq, dtype=F32)[:, None] * inv[None, :]
    cos, sin = jnp.cos(ang), jnp.sin(ang)
    rest = DQK_A - ROT_DIM
    one = jnp.ones((seq, rest), F32)
    zr = jnp.zeros((seq, rest), F32)
    zh = jnp.zeros((seq, half), F32)
    reps = LANES // DQK_A
    ct = jnp.tile(jnp.concatenate([cos, cos, one], axis=1), (1, reps))
    s1 = jnp.tile(jnp.concatenate([-sin, zh, zr], axis=1), (1, reps))
    s2 = jnp.tile(jnp.concatenate([zh, sin, zr], axis=1), (1, reps))
    return ct, s1, s2


def _layer_weights(w_in, w_pa, w_pm, w_out, w_up, w_down):
    seg = lambda j: w_in[:, SEG_STARTS[j]:SEG_STARTS[j] + SEG_WIDTHS[j]]
    wg = seg(7)
    proj = (seg(0).astype(BF16), seg(1).astype(BF16), seg(2).T.astype(BF16),
            seg(3).astype(BF16), seg(4).astype(BF16), seg(5).astype(BF16), seg(6).astype(BF16),
            jnp.pad(wg, ((0, 0), (0, LANES - W_MG))).astype(BF16), wg.T.astype(BF16),
            seg(8).astype(BF16))
    return (proj, w_pa.astype(BF16), w_pm.astype(BF16), w_out.astype(BF16),
            w_up[:, :D_FF].astype(BF16), w_up[:, D_FF:].astype(BF16), w_down.astype(BF16))


def _tile(seq, want):
    t = min(seq, want)
    assert seq % t == 0, (seq, want)
    return t


def _encoder_layer(x, ada, l, depth, weights, lambda_qk, subln_g, b_mgate, mnorm_g,
                   ln1_g, ln1_b, conv_w, conv_b, ln2_g, ln2_b, tiles):
    B, S, D = x.shape
    ts, tq, tk = (_tile(S, t) for t in tiles)
    alpha = (2.0 * depth) ** 0.25
    lam_init = 0.8 - 0.6 * math.exp(-0.3 * l)
    proj_w, wpa, wpm, wout, wug, wuv, wd = weights
    ada3 = ada.reshape(B, 1, 6 * D)
    q, k, vt, mq, mk, mv, og, gcol, grow, gab = _proj(x, ada3, _rope_tables(S), proj_w, b_mgate, ts)
    ya = _attn(q, k, vt, lambda_qk, subln_g, lam_init, tq, tk)
    hf, hb = _mlstm(mq, mk, mv, gcol, grow)
    x1 = _mix(x, ada3, ya, hf, hb, og, gab, wpa, wpm, wout, mnorm_g, ln1_g, ln1_b, alpha, ts)
    return _ffn(x1, ada3, wug, wuv, conv_w, conv_b, wd, ln2_g, ln2_b, alpha, ts)


def _trunk(xs, cs, w_ada, b_ada, w_in, lambda_qk, subln_g, b_mgate, mnorm_g, w_pa, w_pm, w_out,
           ln1_g, ln1_b, w_up, conv_w, conv_b, w_down, ln2_g, ln2_b, tiles=(512, 256, 512)):
    depth = w_in.shape[0]
    rows = [c.shape[0] for c in cs]
    c_all = jnp.concatenate(cs, axis=0)
    xs = list(xs)
    for l in range(depth):
        ada_all = _ada(c_all, w_ada[l], b_ada[l])
        weights = _layer_weights(w_in[l], w_pa[l], w_pm[l], w_out[l], w_up[l], w_down[l])
        start = 0
        for g, n in enumerate(rows):
            xs[g] = _encoder_layer(xs[g], ada_all[start:start + n], l, depth, weights,
                                   lambda_qk[l], subln_g[l], b_mgate[l], mnorm_g[l],
                                   ln1_g[l], ln1_b[l], conv_w[l], conv_b[l], ln2_g[l], ln2_b[l], tiles)
            start += n
    return tuple(xs)


def kernel(x_prompt, x_sample, c_prompt, c_sample, w_ada, b_ada, w_in, lambda_qk, subln_g, b_mgate, mnorm_g, w_pa, w_pm, w_out, ln1_g, ln1_b, w_up, conv_w, conv_b, w_down, ln2_g, ln2_b):
    return _trunk((x_prompt, x_sample), (c_prompt, c_sample), w_ada, b_ada, w_in, lambda_qk, subln_g,
                  b_mgate, mnorm_g, w_pa, w_pm, w_out, ln1_g, ln1_b, w_up, conv_w, conv_b, w_down,
                  ln2_g, ln2_b)
```

```python
import functools
import math

import jax
import jax.numpy as jnp
import numpy as np
from jax import lax
from jax.experimental import pallas as pl
from jax.experimental.pallas import tpu as pltpu

F32 = jnp.float32
BF16 = jnp.bfloat16

D_MODEL = 1024
N_HEADS_A = 8
DQK_A = 64
DV_A = 2 * DQK_A
ROT_DIM = DQK_A // 4
ROPE_THETA = 500000.0
N_HEADS_M = 4
DQK_M = 128
DV_M = 256
CHUNK = 128
D_FF = 2816
LN_EPS = 1e-5

W_AQ = N_HEADS_A * 2 * DQK_A
W_AK = N_HEADS_A * 2 * DQK_A
W_AV = N_HEADS_A * DV_A
W_MQ = N_HEADS_M * DQK_M
W_MK = N_HEADS_M * DQK_M
W_MV = N_HEADS_M * DV_M
W_MO = N_HEADS_M * DV_M
W_MG = 2 * 2 * N_HEADS_M
W_BG = 2 * D_MODEL
SEG_WIDTHS = (W_AQ, W_AK, W_AV, W_MQ, W_MK, W_MV, W_MO, W_MG, W_BG)
SEG_STARTS = [0] + [int(s) for s in np.cumsum(SEG_WIDTHS)[:-1]]

LANES = 128
SUBLANES = 8
LOG2E = 1.4426950408889634
VMEM_LIMIT = 56 * 1024 * 1024

NT_DIMS = (((1,), (1,)), ((), ()))


def _dot(a, b):
    return jnp.dot(a, b, preferred_element_type=F32)


def _dot_nt(a, b):
    return lax.dot_general(a, b, NT_DIMS, preferred_element_type=F32)


def _norm(x):
    mu = jnp.mean(x, axis=-1, keepdims=True)
    xc = x - mu
    var = jnp.mean(xc * xc, axis=-1, keepdims=True)
    return xc * lax.rsqrt(var + LN_EPS)


def _split_bf16(a):
    hi = a.astype(BF16)
    lo = (a - hi.astype(F32)).astype(BF16)
    return hi, lo


def _const_spec(shape):
    nd = len(shape)
    return pl.BlockSpec(shape, lambda *_: (0,) * nd, pipeline_mode=pl.Buffered(1))


def _params(semantics):
    return pltpu.CompilerParams(dimension_semantics=semantics, vmem_limit_bytes=VMEM_LIMIT)


def _ada_kernel(c_ref, w_ref, b_ref, o_ref):
    c = c_ref[...]
    a = c * jax.nn.sigmoid(c)
    a_hi, a_lo = _split_bf16(a)
    w_hi, w_lo = _split_bf16(w_ref[...])
    o_ref[...] = _dot(a_hi, w_hi) + _dot(a_lo, w_hi) + _dot(a_hi, w_lo) + b_ref[...]


def _ada(c, w_ada, b_ada):
    rows, d = c.shape
    n = w_ada.shape[1]
    tn = 1024
    return pl.pallas_call(
        _ada_kernel,
        grid=(n // tn,),
        in_specs=[pl.BlockSpec((rows, d), lambda j: (0, 0)),
                  pl.BlockSpec((d, tn), lambda j: (0, j)),
                  pl.BlockSpec((1, tn), lambda j: (0, j))],
        out_specs=pl.BlockSpec((rows, tn), lambda j: (0, j)),
        out_shape=jax.ShapeDtypeStruct((rows, n), F32),
        compiler_params=_params(("parallel",)),
        name="ada",
    )(c, w_ada, b_ada.reshape(1, n))


def _log_sigmoid(g):
    return jnp.minimum(g, 0.0) - jnp.log1p(jnp.exp(-jnp.abs(g)))


def _proj_kernel(x_ref, ada_ref, ct_ref, s1_ref, s2_ref,
                 wq_ref, wk_ref, wvt_ref, wmq_ref, wmk_ref, wmv_ref, wmo_ref,
                 wgc_ref, wgr_ref, bgc_ref, bgr_ref, wbg_ref,
                 q_ref, k_ref, vt_ref, mq_ref, mk_ref, mv_ref, og_ref, gc_ref, gr_ref, gab_ref):
    ts = x_ref.shape[1]
    ada = ada_ref[0]
    sh1 = ada[:, 0:D_MODEL]
    sc1 = ada[:, D_MODEL:2 * D_MODEL]
    h = (_norm(x_ref[0]) * (1.0 + sc1) + sh1).astype(BF16)

    ct = ct_ref[...]
    s1 = s1_ref[...]
    s2 = s2_ref[...]

    def rope_store(w_ref, o_ref, scale):
        a = _dot(h, w_ref[...])
        for j in range(W_AQ // LANES):
            blk = a[:, j * LANES:(j + 1) * LANES]
            r = (blk * ct + pltpu.roll(blk, LANES - ROT_DIM // 2, 1) * s1
                 + pltpu.roll(blk, ROT_DIM // 2, 1) * s2)
            o_ref[0, :, j * LANES:(j + 1) * LANES] = (r * scale).astype(o_ref.dtype)

    rope_store(wq_ref, q_ref, (DQK_A ** -0.5) * LOG2E)
    rope_store(wk_ref, k_ref, 1.0)
    vt_ref[0] = _dot_nt(wvt_ref[...], h).astype(vt_ref.dtype)
    mq_ref[0] = _dot(h, wmq_ref[...]).astype(mq_ref.dtype)
    mk_ref[0] = (_dot(h, wmk_ref[...]) * (DQK_M ** -0.5)).astype(mk_ref.dtype)
    mv_ref[0] = _dot(h, wmv_ref[...]).astype(mv_ref.dtype)
    og_ref[0] = jax.nn.sigmoid(_dot(h, wmo_ref[...])).astype(og_ref.dtype)
    gab_ref[0] = jax.nn.sigmoid(_dot(h, wbg_ref[...])).astype(gab_ref.dtype)

    r = lax.broadcasted_iota(jnp.int32, (ts, ts), 0)
    c = lax.broadcasted_iota(jnp.int32, (ts, ts), 1)
    shift = CHUNK.bit_length() - 1
    same = jnp.right_shift(r, shift) == jnp.right_shift(c, shift)
    lower = jnp.where(same & (c <= r), 1.0, 0.0).astype(BF16)
    upper = jnp.where(same & (c >= r), 1.0, 0.0).astype(BF16)

    def gate_parts(g, ch):
        is_f = (ch & (2 * N_HEADS_M - 1)) >= N_HEADS_M
        val = jnp.where(is_f, _log_sigmoid(g), g)
        vf = jnp.where(is_f & (ch < 8), val, 0.0)
        vb = jnp.where(is_f & (ch >= 8) & (ch < W_MG), val, 0.0)
        vi = jnp.where(is_f, 0.0, val)
        return _split_bf16(vf), _split_bf16(vb), vi

    g = _dot(h, wgc_ref[...]) + bgc_ref[...]
    (f_hi, f_lo), (b_hi, b_lo), vi = gate_parts(g, lax.broadcasted_iota(jnp.int32, g.shape, 1))
    gcol = _dot(lower, f_hi) + _dot(lower, f_lo) + _dot(upper, b_hi) + _dot(upper, b_lo) + vi
    gc_ref[0] = gcol[:, 0:W_MG]

    g = _dot_nt(wgr_ref[...], h) + bgr_ref[...]
    (f_hi, f_lo), (b_hi, b_lo), vi = gate_parts(g, lax.broadcasted_iota(jnp.int32, g.shape, 0))
    gr_ref[0] = _dot(f_hi, upper) + _dot(f_lo, upper) + _dot(b_hi, lower) + _dot(b_lo, lower) + vi


def _proj(x, ada3, rope, wts, b_mgate, ts):
    B, S, D = x.shape
    ct, s1, s2 = rope
    wq, wk, wvt, wmq, wmk, wmv, wmo, wgc, wgr, wbg = wts
    bgc = jnp.pad(b_mgate.reshape(1, W_MG).astype(F32), ((0, 0), (0, LANES - W_MG)))
    bgr = b_mgate.reshape(W_MG, 1).astype(F32)
    tok = lambda n: pl.BlockSpec((1, ts, n), lambda b, i: (b, i, 0))
    tab = pl.BlockSpec((ts, LANES), lambda b, i: (i, 0))
    consts = (wq, wk, wvt, wmq, wmk, wmv, wmo, wgc, wgr, bgc, bgr, wbg)
    out_shape = (
        jax.ShapeDtypeStruct((B, S, W_AQ), BF16),
        jax.ShapeDtypeStruct((B, S, W_AK), BF16),
        jax.ShapeDtypeStruct((B, W_AV, S), BF16),
        jax.ShapeDtypeStruct((B, S, W_MQ), BF16),
        jax.ShapeDtypeStruct((B, S, W_MK), BF16),
        jax.ShapeDtypeStruct((B, S, W_MV), BF16),
        jax.ShapeDtypeStruct((B, S, W_MO), BF16),
        jax.ShapeDtypeStruct((B, S, W_MG), F32),
        jax.ShapeDtypeStruct((B, W_MG, S), F32),
        jax.ShapeDtypeStruct((B, S, W_BG), BF16),
    )
    out_specs = (
        tok(W_AQ), tok(W_AK),
        pl.BlockSpec((1, W_AV, ts), lambda b, i: (b, 0, i)),
        tok(W_MQ), tok(W_MK), tok(W_MV), tok(W_MO), tok(W_MG),
        pl.BlockSpec((1, W_MG, ts), lambda b, i: (b, 0, i)),
        tok(W_BG),
    )
    return pl.pallas_call(
        _proj_kernel,
        grid=(B, S // ts),
        in_specs=[tok(D), pl.BlockSpec((1, 1, 6 * D), lambda b, i: (b, 0, 0)), tab, tab, tab]
                 + [_const_spec(w.shape) for w in consts],
        out_specs=out_specs,
        out_shape=out_shape,
        compiler_params=_params(("parallel", "parallel")),
        name="proj",
    )(x, ada3, ct, s1, s2, *consts)


ONES_ROWS = 16


def _attn_kernel(lam_ref, q_ref, k_ref, vt_ref, g_ref, o_ref,
                 s_ref, p_ref, acc, m_ref, mx_ref, alpha_ref, *, tk, sub, lam_init):
    n = k_ref.shape[1] // tk
    nsub = tk // sub
    q = q_ref[0]
    lane = lax.broadcasted_iota(jnp.int32, q.shape, 1)
    zero = jnp.zeros_like(q)
    qs = (jnp.where(lane < DQK_A, q, zero), jnp.where(lane >= DQK_A, q, zero))
    ones_rows = jnp.ones((ONES_ROWS, sub), BF16)
    maps = range(2)

    acc[...] = jnp.zeros_like(acc)
    m_ref[...] = jnp.full_like(m_ref, -jnp.inf)

    def scores_sub(j, r, mx):
        kc = k_ref[0, pl.ds(pl.multiple_of(j * tk + r * sub, sub), sub), :]
        out = []
        for mp in maps:
            s = _dot_nt(kc, qs[mp])
            s_ref[mp, r * sub:(r + 1) * sub, :] = s
            cm = jnp.max(s, axis=0, keepdims=True)
            out.append(cm if mx is None else jnp.maximum(mx[mp], cm))
        return out

    def softmax_sub(r, m_new):
        for mp in maps:
            s = s_ref[mp, r * sub:(r + 1) * sub, :]
            p_ref[mp, r * sub:(r + 1) * sub, :] = jnp.exp2((s - m_new[mp]).astype(BF16))

    def values_sub(j, r, pv):
        vc = vt_ref[0, :, pl.ds(pl.multiple_of(j * tk + r * sub, sub), sub)]
        va = jnp.concatenate([vc, ones_rows], axis=0)
        out = []
        for mp in maps:
            d = _dot(va, p_ref[mp, r * sub:(r + 1) * sub, :])
            out.append(d if pv is None else pv[mp] + d)
        return out

    def accumulate(pv):
        for mp in maps:
            acc[mp] = alpha_ref[mp:mp + 1, :] * acc[mp] + pv[mp]

    def trip(j, first, last):
        m_old = [m_ref[mp:mp + 1, :] for mp in maps]
        m_new = [jnp.maximum(m_old[mp], mx_ref[mp:mp + 1, :]) for mp in maps]
        pv = mx = None
        for r in range(nsub):
            if not first:
                pv = values_sub(j - 1, r, pv)
            softmax_sub(r, m_new)
            if not last:
                mx = scores_sub(j + 1, r, mx)
        if not first:
            accumulate(pv)
        for mp in maps:
            alpha_ref[mp:mp + 1, :] = jnp.exp2(m_old[mp] - m_new[mp])
            m_ref[mp:mp + 1, :] = m_new[mp]
            if not last:
                mx_ref[mp:mp + 1, :] = mx[mp]

    mx = None
    for r in range(nsub):
        mx = scores_sub(0, r, mx)
    for mp in maps:
        mx_ref[mp:mp + 1, :] = mx[mp]
    trip(0, True, False)

    def body(j, carry):
        trip(j, False, False)
        return carry

    lax.fori_loop(1, n - 1, body, 0)
    trip(n - 1, False, True)
    pv = None
    for r in range(nsub):
        pv = values_sub(n - 1, r, pv)
    accumulate(pv)

    lq = lam_ref[...]
    la = jnp.sum(lq[0:1] * lq[1:2], axis=1, keepdims=True)
    lb = jnp.sum(lq[2:3] * lq[3:4], axis=1, keepdims=True)
    lam = jnp.exp(la) - jnp.exp(lb) + lam_init
    a1 = acc[0]
    a2 = acc[1]
    o = a1[0:DV_A] / a1[DV_A:DV_A + 1] - lam * (a2[0:DV_A] / a2[DV_A:DV_A + 1])
    ms = jnp.mean(o * o, axis=0, keepdims=True)
    y = o * lax.rsqrt(ms + LN_EPS) * g_ref[...] * (1.0 - lam_init)
    o_ref[0] = y.T.astype(o_ref.dtype)


def _attn(q, k, vt, lambda_qk, subln_g, lam_init, tq, tk, sub):
    B, S, _ = q.shape
    assert S // tk >= 2 and tk % sub == 0, (S, tk, sub)
    kern = functools.partial(_attn_kernel, tk=tk, sub=sub, lam_init=lam_init)
    return pl.pallas_call(
        kern,
        grid=(B, N_HEADS_A, S // tq),
        in_specs=[pl.BlockSpec((4, DQK_A), lambda b, h, i: (0, 0)),
                  pl.BlockSpec((1, tq, DV_A), lambda b, h, i: (b, i, h)),
                  pl.BlockSpec((1, S, DV_A), lambda b, h, i: (b, 0, h)),
                  pl.BlockSpec((1, DV_A, S), lambda b, h, i: (b, h, 0)),
                  pl.BlockSpec((DV_A, 1), lambda b, h, i: (0, 0))],
        out_specs=pl.BlockSpec((1, tq, DV_A), lambda b, h, i: (b, i, h)),
        out_shape=jax.ShapeDtypeStruct((B, S, N_HEADS_A * DV_A), BF16),
        scratch_shapes=[pltpu.VMEM((2, tk, tq), F32), pltpu.VMEM((2, tk, tq), BF16),
                        pltpu.VMEM((2, DV_A + ONES_ROWS, tq), F32),
                        pltpu.VMEM((2, tq), F32), pltpu.VMEM((2, tq), F32), pltpu.VMEM((2, tq), F32)],
        compiler_params=_params(("parallel", "parallel", "parallel")),
        name="attn",
    )(lambda_qk.astype(F32), q, k, vt, subln_g.reshape(DV_A, 1).astype(F32))


N_STREAMS = 2 * N_HEADS_M
DV_AUG = DV_M + LANES


def _mlstm_kernel(qf_ref, kf_ref, vf_ref, gcf_ref, grf_ref,
                  qb_ref, kb_ref, vb_ref, gcb_ref, grb_ref,
                  hf_ref, hb_ref, ct_ref, m_ref):
    L = CHUNK

    @pl.when(pl.program_id(1) == 0)
    def _():
        ct_ref[...] = jnp.zeros_like(ct_ref)
        m_ref[...] = jnp.zeros_like(m_ref)

    row = lax.broadcasted_iota(jnp.int32, (L, L), 0)
    col = lax.broadcasted_iota(jnp.int32, (L, L), 1)
    ones_blk = jnp.where(lax.broadcasted_iota(jnp.int32, (L, LANES), 1) == 0, 1.0, 0.0).astype(BF16)
    dirs = ((qf_ref, kf_ref, vf_ref, gcf_ref, grf_ref, hf_ref),
            (qb_ref, kb_ref, vb_ref, gcb_ref, grb_ref, hb_ref))
    for d, (q_ref, k_ref, v_ref, gc_ref, gr_ref, h_ref) in enumerate(dirs):
        tri = (col <= row) if d == 0 else (col >= row)
        e = L - 1 if d == 0 else 0
        gc = gc_ref[0]
        gr = gr_ref[0]
        for hd in range(N_HEADS_M):
            st = d * N_HEADS_M + hd
            ci = d * 2 * N_HEADS_M + hd
            cf = ci + N_HEADS_M
            qh = q_ref[0, :, hd * DQK_M:(hd + 1) * DQK_M]
            kh = k_ref[0, :, hd * DQK_M:(hd + 1) * DQK_M]
            va = jnp.concatenate([v_ref[0, :, hd * DV_M:(hd + 1) * DV_M], ones_blk], axis=1)
            i_row = gr[ci:ci + 1, :]
            b_row = gr[cf:cf + 1, :]
            i_col = gc[:, ci:ci + 1]
            b_col = gc[:, cf:cf + 1]
            m_prev = m_ref[st:st + 1, 0:1]
            dm = jnp.where(tri, b_col - b_row + i_row, -jnp.inf)
            inter = b_col + m_prev
            m_t = jnp.maximum(inter, jnp.max(dm, axis=1, keepdims=True))
            w = jnp.exp(dm - m_t)
            w_inter = jnp.exp(inter - m_t)
            sqk = (_dot_nt(qh, kh) * w).astype(BF16)
            ct = ct_ref[st]
            numa = _dot(sqk, va) + w_inter * _dot(qh, ct.astype(BF16))
            den = numa[:, DV_M:DV_M + 1]
            h = numa[:, 0:DV_M] / jnp.maximum(jnp.abs(den), jnp.exp(-m_t))
            h_ref[0, :, hd * DV_M:(hd + 1) * DV_M] = h.astype(h_ref.dtype)
            b_end = b_row[:, e:e + 1]
            m_new = m_t[e:e + 1, :]
            ws = jnp.exp(b_end - b_col + i_col - m_new)
            decay = jnp.exp(b_end + m_prev - m_new)
            kst = (kh.astype(F32) * ws).T.astype(BF16)
            ct_ref[st] = decay * ct + _dot(kst, va)
            m_ref[st:st + 1, :] = jnp.broadcast_to(m_new, (1, LANES))


def _mlstm(mq, mk, mv, gcol, grow):
    B, S, _ = mq.shape
    nc = S // CHUNK
    fwd = lambda n: pl.BlockSpec((1, CHUNK, n), lambda b, c: (b, c, 0))
    bwd = lambda n: pl.BlockSpec((1, CHUNK, n), lambda b, c: (b, nc - 1 - c, 0))
    rowf = pl.BlockSpec((1, W_MG, CHUNK), lambda b, c: (b, 0, c))
    rowb = pl.BlockSpec((1, W_MG, CHUNK), lambda b, c: (b, 0, nc - 1 - c))
    return pl.pallas_call(
        _mlstm_kernel,
        grid=(B, nc),
        in_specs=[fwd(W_MQ), fwd(W_MK), fwd(W_MV), fwd(W_MG), rowf,
                  bwd(W_MQ), bwd(W_MK), bwd(W_MV), bwd(W_MG), rowb],
        out_specs=(fwd(W_MV), bwd(W_MV)),
        out_shape=(jax.ShapeDtypeStruct((B, S, W_MV), BF16),
                   jax.ShapeDtypeStruct((B, S, W_MV), BF16)),
        scratch_shapes=[pltpu.VMEM((N_STREAMS, DQK_M, DV_AUG), F32),
                        pltpu.VMEM((N_STREAMS, LANES), F32)],
        compiler_params=_params(("parallel", "arbitrary")),
        name="mlstm",
    )(mq, mk, mv, gcol, grow, mq, mk, mv, gcol, grow)


def _mix_kernel(x_ref, ada_ref, ya_ref, hf_ref, hb_ref, og_ref, gab_ref,
                wpa_ref, wpm_ref, wout_ref, mng_ref, lg_ref, lb_ref, o_ref, *, alpha):
    hs = hf_ref[0].astype(F32) + hb_ref[0].astype(F32)
    hn = jnp.concatenate(
        [_norm(hs[:, hd * DV_M:(hd + 1) * DV_M]) for hd in range(N_HEADS_M)], axis=1)
    ym = (og_ref[0].astype(F32) * (hn * mng_ref[...])).astype(BF16)
    gab = gab_ref[0].astype(F32)
    pre = gab[:, 0:D_MODEL] * _dot(ya_ref[0], wpa_ref[...]) + gab[:, D_MODEL:] * _dot(ym, wpm_ref[...])
    mix = _dot(pre.astype(BF16), wout_ref[...])
    g1 = ada_ref[0][:, 2 * D_MODEL:3 * D_MODEL]
    o_ref[0] = _norm(alpha * x_ref[0] + g1 * mix) * lg_ref[...] + lb_ref[...]


def _mix(x, ada3, ya, hf, hb, og, gab, wpa, wpm, wout, mnorm_g, ln_g, ln_b, alpha, ts):
    B, S, D = x.shape
    tok = lambda n: pl.BlockSpec((1, ts, n), lambda b, i: (b, i, 0))
    consts = (wpa, wpm, wout, mnorm_g.reshape(1, W_MV).astype(F32),
              ln_g.reshape(1, D).astype(F32), ln_b.reshape(1, D).astype(F32))
    return pl.pallas_call(
        functools.partial(_mix_kernel, alpha=alpha),
        grid=(B, S // ts),
        in_specs=[tok(D), pl.BlockSpec((1, 1, 6 * D), lambda b, i: (b, 0, 0)),
                  tok(W_AV), tok(W_MV), tok(W_MV), tok(W_MO), tok(W_BG)]
                 + [_const_spec(w.shape) for w in consts],
        out_specs=tok(D),
        out_shape=jax.ShapeDtypeStruct((B, S, D), F32),
        compiler_params=_params(("parallel", "parallel")),
        name="mix",
    )(x, ada3, ya, hf, hb, og, gab, *consts)


def _gelu_tanh(x):
    return 0.5 * x * (1.0 + jnp.tanh(math.sqrt(2.0 / math.pi) * (x + 0.044715 * (x * x * x))))


def _ffn_kernel(xp_ref, x_ref, xn_ref, ada_ref, wug_ref, wuv_ref, cw_ref, cb_ref, wd_ref,
                lg_ref, lb_ref, o_ref, *, alpha):
    ts = x_ref.shape[1]
    H = SUBLANES
    i = pl.program_id(1)
    ada = ada_ref[0]
    sh2 = ada[:, 3 * D_MODEL:4 * D_MODEL]
    sc2 = ada[:, 4 * D_MODEL:5 * D_MODEL]
    g2 = ada[:, 5 * D_MODEL:6 * D_MODEL]
    x = x_ref[0]
    xx = jnp.concatenate([xp_ref[0], x, xn_ref[0]], axis=0)
    hx = _norm(xx) * (1.0 + sc2) + sh2
    ug = _dot(hx.astype(BF16), wug_ref[...])
    uv = _dot(hx[H:H + ts].astype(BF16), wuv_ref[...])
    n = ts + 2 * H
    prev = pltpu.roll(ug, 1, 0)[H:H + ts]
    nxt = pltpu.roll(ug, n - 1, 0)[H:H + ts]
    t = lax.broadcasted_iota(jnp.int32, (ts, 1), 0)
    keep_prev = jnp.where((i == 0) & (t == 0), 0.0, 1.0)
    keep_next = jnp.where((i == pl.num_programs(1) - 1) & (t == ts - 1), 0.0, 1.0)
    cw = cw_ref[...]
    conv = (prev * keep_prev * cw[0:1] + ug[H:H + ts] * cw[1:2] + nxt * keep_next * cw[2:3]
            + cb_ref[...])
    y = _dot((_gelu_tanh(conv) * uv).astype(BF16), wd_ref[...])
    o_ref[0] = _norm(alpha * x + g2 * y) * lg_ref[...] + lb_ref[...]


def _ffn(x, ada3, wug, wuv, conv_w, conv_b, wd, ln_g, ln_b, alpha, ts):
    B, S, D = x.shape
    H = SUBLANES
    nb = ts // H
    last = S // H - 1
    tok = pl.BlockSpec((1, ts, D), lambda b, i: (b, i, 0))
    prev = pl.BlockSpec((1, H, D), lambda b, i: (b, jnp.maximum(i * nb - 1, 0), 0))
    nxt = pl.BlockSpec((1, H, D), lambda b, i: (b, jnp.minimum((i + 1) * nb, last), 0))
    consts = (wug, wuv, conv_w.astype(F32), conv_b.reshape(1, D_FF).astype(F32), wd,
              ln_g.reshape(1, D).astype(F32), ln_b.reshape(1, D).astype(F32))
    return pl.pallas_call(
        functools.partial(_ffn_kernel, alpha=alpha),
        grid=(B, S // ts),
        in_specs=[prev, tok, nxt, pl.BlockSpec((1, 1, 6 * D), lambda b, i: (b, 0, 0))]
                 + [_const_spec(w.shape) for w in consts],
        out_specs=tok,
        out_shape=jax.ShapeDtypeStruct((B, S, D), F32),
        compiler_params=_params(("parallel", "parallel")),
        name="ffn",
    )(x, x, x, ada3, *consts)


def _rope_tables(seq):
    half = ROT_DIM // 2
    inv = ROPE_THETA ** (-jnp.arange(0, ROT_DIM, 2, dtype=F32) / ROT_DIM)
    ang = jnp.arange(seq, dtype=F32)[:, None] * inv[None, :]
    cos, sin = jnp.cos(ang), jnp.sin(ang)
    rest = DQK_A - ROT_DIM
    one = jnp.ones((seq, rest), F32)
    zr = jnp.zeros((seq, rest), F32)
    zh = jnp.zeros((seq, half), F32)
    reps = LANES // DQK_A
    ct = jnp.tile(jnp.concatenate([cos, cos, one], axis=1), (1, reps))
    s1 = jnp.tile(jnp.concatenate([-sin, zh, zr], axis=1), (1, reps))
    s2 = jnp.tile(jnp.concatenate([zh, sin, zr], axis=1), (1, reps))
    return ct, s1, s2


def _layer_weights(w_in, w_pa, w_pm, w_out, w_up, w_down):
    seg = lambda j: w_in[:, SEG_STARTS[j]:SEG_STARTS[j] + SEG_WIDTHS[j]]
    wg = seg(7)
    proj = (seg(0).astype(BF16), seg(1).astype(BF16), seg(2).T.astype(BF16),
            seg(3).astype(BF16), seg(4).astype(BF16), seg(5).astype(BF16), seg(6).astype(BF16),
            jnp.pad(wg, ((0, 0), (0, LANES - W_MG))).astype(BF16), wg.T.astype(BF16),
            seg(8).astype(BF16))
    return (proj, w_pa.astype(BF16), w_pm.astype(BF16), w_out.astype(BF16),
            w_up[:, :D_FF].astype(BF16), w_up[:, D_FF:].astype(BF16), w_down.astype(BF16))


def _tile(seq, want):
    t = min(seq, want)
    assert seq % t == 0, (seq, want)
    return t


def _encoder_layer(x, ada, l, depth, weights, lambda_qk, subln_g, b_mgate, mnorm_g,
                   ln1_g, ln1_b, conv_w, conv_b, ln2_g, ln2_b, tiles):
    B, S, D = x.shape
    ts, tq, tk, sub = (_tile(S, t) for t in tiles)
    alpha = (2.0 * depth) ** 0.25
    lam_init = 0.8 - 0.6 * math.exp(-0.3 * l)
    proj_w, wpa, wpm, wout, wug, wuv, wd = weights
    ada3 = ada.reshape(B, 1, 6 * D)
    q, k, vt, mq, mk, mv, og, gcol, grow, gab = _proj(x, ada3, _rope_tables(S), proj_w, b_mgate, ts)
    ya = _attn(q, k, vt, lambda_qk, subln_g, lam_init, tq, tk, sub)
    hf, hb = _mlstm(mq, mk, mv, gcol, grow)
    x1 = _mix(x, ada3, ya, hf, hb, og, gab, wpa, wpm, wout, mnorm_g, ln1_g, ln1_b, alpha, ts)
    return _ffn(x1, ada3, wug, wuv, conv_w, conv_b, wd, ln2_g, ln2_b, alpha, ts)


def _trunk(xs, cs, w_ada, b_ada, w_in, lambda_qk, subln_g, b_mgate, mnorm_g, w_pa, w_pm, w_out,
           ln1_g, ln1_b, w_up, conv_w, conv_b, w_down, ln2_g, ln2_b, tiles=(512, 512, 1024, 256)):
    depth = w_in.shape[0]
    rows = [c.shape[0] for c in cs]
    c_all = jnp.concatenate(cs, axis=0)
    xs = list(xs)
    for l in range(depth):
        ada_all = _ada(c_all, w_ada[l], b_ada[l])
        weights = _layer_weights(w_in[l], w_pa[l], w_pm[l], w_out[l], w_up[l], w_down[l])
        start = 0
        for g, n in enumerate(rows):
            xs[g] = _encoder_layer(xs[g], ada_all[start:start + n], l, depth, weights,
                                   lambda_qk[l], subln_g[l], b_mgate[l], mnorm_g[l],
                                   ln1_g[l], ln1_b[l], conv_w[l], conv_b[l], ln2_g[l], ln2_b[l], tiles)
            start += n
    return tuple(xs)


def kernel(x_prompt, x_sample, c_prompt, c_sample, w_ada, b_ada, w_in, lambda_qk, subln_g, b_mgate, mnorm_g, w_pa, w_pm, w_out, ln1_g, ln1_b, w_up, conv_w, conv_b, w_down, ln2_g, ln2_b):
    return _trunk((x_prompt, x_sample), (c_prompt, c_sample), w_ada, b_ada, w_in, lambda_qk, subln_g,
                  b_mgate, mnorm_g, w_pa, w_pm, w_out, ln1_g, ln1_b, w_up, conv_w, conv_b, w_down,
                  ln2_g, ln2_b)
```

```python
import functools
import math

import jax
import jax.numpy as jnp
import numpy as np
from jax import lax
from jax.experimental import pallas as pl
from jax.experimental.pallas import tpu as pltpu

F32 = jnp.float32
BF16 = jnp.bfloat16

D_MODEL = 1024
N_HEADS_A = 8
DQK_A = 64
DV_A = 2 * DQK_A
ROT_DIM = DQK_A // 4
ROPE_THETA = 500000.0
N_HEADS_M = 4
DQK_M = 128
DV_M = 256
CHUNK = 128
D_FF = 2816
LN_EPS = 1e-5

W_AQ = N_HEADS_A * 2 * DQK_A
W_AK = N_HEADS_A * 2 * DQK_A
W_AV = N_HEADS_A * DV_A
W_MQ = N_HEADS_M * DQK_M
W_MK = N_HEADS_M * DQK_M
W_MV = N_HEADS_M * DV_M
W_MO = N_HEADS_M * DV_M
W_MG = 2 * 2 * N_HEADS_M
W_BG = 2 * D_MODEL
SEG_WIDTHS = (W_AQ, W_AK, W_AV, W_MQ, W_MK, W_MV, W_MO, W_MG, W_BG)
SEG_STARTS = [0] + [int(s) for s in np.cumsum(SEG_WIDTHS)[:-1]]

LANES = 128
SUBLANES = 8
LOG2E = 1.4426950408889634
VMEM_LIMIT = 56 * 1024 * 1024

NT_DIMS = (((1,), (1,)), ((), ()))


def _dot(a, b):
    return jnp.dot(a, b, preferred_element_type=F32)


def _dot_nt(a, b):
    return lax.dot_general(a, b, NT_DIMS, preferred_element_type=F32)


def _norm(x):
    mu = jnp.mean(x, axis=-1, keepdims=True)
    xc = x - mu
    var = jnp.mean(xc * xc, axis=-1, keepdims=True)
    return xc * lax.rsqrt(var + LN_EPS)


def _split_bf16(a):
    hi = a.astype(BF16)
    lo = (a - hi.astype(F32)).astype(BF16)
    return hi, lo


def _const_spec(shape):
    nd = len(shape)
    return pl.BlockSpec(shape, lambda *_: (0,) * nd, pipeline_mode=pl.Buffered(1))


def _params(semantics):
    return pltpu.CompilerParams(dimension_semantics=semantics, vmem_limit_bytes=VMEM_LIMIT)


def _ada_kernel(c_ref, w_ref, b_ref, o_ref):
    c = c_ref[...]
    a = c * jax.nn.sigmoid(c)
    a_hi, a_lo = _split_bf16(a)
    w_hi, w_lo = _split_bf16(w_ref[...])
    o_ref[...] = _dot(a_hi, w_hi) + _dot(a_lo, w_hi) + _dot(a_hi, w_lo) + b_ref[...]


def _ada(c, w_ada, b_ada):
    rows, d = c.shape
    n = w_ada.shape[1]
    tn = 1024
    return pl.pallas_call(
        _ada_kernel,
        grid=(n // tn,),
        in_specs=[pl.BlockSpec((rows, d), lambda j: (0, 0)),
                  pl.BlockSpec((d, tn), lambda j: (0, j)),
                  pl.BlockSpec((1, tn), lambda j: (0, j))],
        out_specs=pl.BlockSpec((rows, tn), lambda j: (0, j)),
        out_shape=jax.ShapeDtypeStruct((rows, n), F32),
        compiler_params=_params(("parallel",)),
        name="ada",
    )(c, w_ada, b_ada.reshape(1, n))


def _log_sigmoid(g):
    return jnp.minimum(g, 0.0) - jnp.log1p(jnp.exp(-jnp.abs(g)))


def _proj_kernel(x_ref, ada_ref, ct_ref, s1_ref, s2_ref,
                 wq_ref, wk_ref, wvt_ref, wmq_ref, wmk_ref, wmv_ref, wmo_ref,
                 wgc_ref, wgr_ref, bgc_ref, bgr_ref, wbg_ref,
                 q_ref, k_ref, vt_ref, mq_ref, mk_ref, mv_ref, og_ref, gc_ref, gr_ref, gab_ref):
    ts = x_ref.shape[1]
    ada = ada_ref[0]
    sh1 = ada[:, 0:D_MODEL]
    sc1 = ada[:, D_MODEL:2 * D_MODEL]
    h = (_norm(x_ref[0]) * (1.0 + sc1) + sh1).astype(BF16)

    ct = ct_ref[...]
    s1 = s1_ref[...]
    s2 = s2_ref[...]

    def rope_store(w_ref, o_ref, scale):
        a = _dot(h, w_ref[...])
        for j in range(W_AQ // LANES):
            blk = a[:, j * LANES:(j + 1) * LANES]
            r = (blk * ct + pltpu.roll(blk, LANES - ROT_DIM // 2, 1) * s1
                 + pltpu.roll(blk, ROT_DIM // 2, 1) * s2)
            o_ref[0, :, j * LANES:(j + 1) * LANES] = (r * scale).astype(o_ref.dtype)

    rope_store(wq_ref, q_ref, (DQK_A ** -0.5) * LOG2E)
    rope_store(wk_ref, k_ref, 1.0)
    vt_ref[0] = _dot_nt(wvt_ref[...], h).astype(vt_ref.dtype)
    mq_ref[0] = _dot(h, wmq_ref[...]).astype(mq_ref.dtype)
    mk_ref[0] = (_dot(h, wmk_ref[...]) * (DQK_M ** -0.5)).astype(mk_ref.dtype)
    mv_ref[0] = _dot(h, wmv_ref[...]).astype(mv_ref.dtype)
    og_ref[0] = jax.nn.sigmoid(_dot(h, wmo_ref[...])).astype(og_ref.dtype)
    gab_ref[0] = jax.nn.sigmoid(_dot(h, wbg_ref[...])).astype(gab_ref.dtype)

    r = lax.broadcasted_iota(jnp.int32, (ts, ts), 0)
    c = lax.broadcasted_iota(jnp.int32, (ts, ts), 1)
    shift = CHUNK.bit_length() - 1
    same = jnp.right_shift(r, shift) == jnp.right_shift(c, shift)
    lower = jnp.where(same & (c <= r), 1.0, 0.0).astype(BF16)
    upper = jnp.where(same & (c >= r), 1.0, 0.0).astype(BF16)

    def gate_parts(g, ch):
        is_f = (ch & (2 * N_HEADS_M - 1)) >= N_HEADS_M
        val = jnp.where(is_f, _log_sigmoid(g), g)
        vf = jnp.where(is_f & (ch < 8), val, 0.0)
        vb = jnp.where(is_f & (ch >= 8) & (ch < W_MG), val, 0.0)
        vi = jnp.where(is_f, 0.0, val)
        return _split_bf16(vf), _split_bf16(vb), vi

    g = _dot(h, wgc_ref[...]) + bgc_ref[...]
    (f_hi, f_lo), (b_hi, b_lo), vi = gate_parts(g, lax.broadcasted_iota(jnp.int32, g.shape, 1))
    gcol = _dot(lower, f_hi) + _dot(lower, f_lo) + _dot(upper, b_hi) + _dot(upper, b_lo) + vi
    gc_ref[0] = gcol[:, 0:W_MG]

    g = _dot_nt(wgr_ref[...], h) + bgr_ref[...]
    (f_hi, f_lo), (b_hi, b_lo), vi = gate_parts(g, lax.broadcasted_iota(jnp.int32, g.shape, 0))
    gr_ref[0] = _dot(f_hi, upper) + _dot(f_lo, upper) + _dot(b_hi, lower) + _dot(b_lo, lower) + vi


def _proj(x, ada3, rope, wts, b_mgate, ts):
    B, S, D = x.shape
    ct, s1, s2 = rope
    wq, wk, wvt, wmq, wmk, wmv, wmo, wgc, wgr, wbg = wts
    bgc = jnp.pad(b_mgate.reshape(1, W_MG).astype(F32), ((0, 0), (0, LANES - W_MG)))
    bgr = b_mgate.reshape(W_MG, 1).astype(F32)
    tok = lambda n: pl.BlockSpec((1, ts, n), lambda b, i: (b, i, 0))
    tab = pl.BlockSpec((ts, LANES), lambda b, i: (i, 0))
    consts = (wq, wk, wvt, wmq, wmk, wmv, wmo, wgc, wgr, bgc, bgr, wbg)
    out_shape = (
        jax.ShapeDtypeStruct((B, S, W_AQ), BF16),
        jax.ShapeDtypeStruct((B, S, W_AK), BF16),
        jax.ShapeDtypeStruct((B, W_AV, S), BF16),
        jax.ShapeDtypeStruct((B, S, W_MQ), BF16),
        jax.ShapeDtypeStruct((B, S, W_MK), BF16),
        jax.ShapeDtypeStruct((B, S, W_MV), BF16),
        jax.ShapeDtypeStruct((B, S, W_MO), BF16),
        jax.ShapeDtypeStruct((B, S, W_MG), F32),
        jax.ShapeDtypeStruct((B, W_MG, S), F32),
        jax.ShapeDtypeStruct((B, S, W_BG), BF16),
    )
    out_specs = (
        tok(W_AQ), tok(W_AK),
        pl.BlockSpec((1, W_AV, ts), lambda b, i: (b, 0, i)),
        tok(W_MQ), tok(W_MK), tok(W_MV), tok(W_MO), tok(W_MG),
        pl.BlockSpec((1, W_MG, ts), lambda b, i: (b, 0, i)),
        tok(W_BG),
    )
    return pl.pallas_call(
        _proj_kernel,
        grid=(B, S // ts),
        in_specs=[tok(D), pl.BlockSpec((1, 1, 6 * D), lambda b, i: (b, 0, 0)), tab, tab, tab]
                 + [_const_spec(w.shape) for w in consts],
        out_specs=out_specs,
        out_shape=out_shape,
        compiler_params=_params(("parallel", "parallel")),
        name="proj",
    )(x, ada3, ct, s1, s2, *consts)


ONES_ROWS = 16


def _attn_kernel(lam_ref, q_ref, k_ref, vt_ref, g_ref, o_ref,
                 s_ref, p_ref, acc, m_ref, mx_ref, alpha_ref, *, tk, sub, lam_init):
    n = k_ref.shape[1] // tk
    nsub = tk // sub
    q = q_ref[0]
    lane = lax.broadcasted_iota(jnp.int32, q.shape, 1)
    zero = jnp.zeros_like(q)
    qs = (jnp.where(lane < DQK_A, q, zero), jnp.where(lane >= DQK_A, q, zero))
    ones_rows = jnp.ones((ONES_ROWS, sub), BF16)
    maps = range(2)

    acc[...] = jnp.zeros_like(acc)
    m_ref[...] = jnp.full_like(m_ref, -jnp.inf)

    def scores_sub(j, r, mx):
        kc = k_ref[0, pl.ds(pl.multiple_of(j * tk + r * sub, sub), sub), :]
        out = []
        for mp in maps:
            s = _dot_nt(kc, qs[mp])
            s_ref[mp, r * sub:(r + 1) * sub, :] = s
            cm = jnp.max(s, axis=0, keepdims=True)
            out.append(cm if mx is None else jnp.maximum(mx[mp], cm))
        return out

    def softmax_sub(r, m_new):
        for mp in maps:
            s = s_ref[mp, r * sub:(r + 1) * sub, :]
            p_ref[mp, r * sub:(r + 1) * sub, :] = jnp.exp2(s - m_new[mp]).astype(BF16)

    def values_sub(j, r, pv):
        vc = vt_ref[0, :, pl.ds(pl.multiple_of(j * tk + r * sub, sub), sub)]
        va = jnp.concatenate([vc, ones_rows], axis=0)
        out = []
        for mp in maps:
            d = _dot(va, p_ref[mp, r * sub:(r + 1) * sub, :])
            out.append(d if pv is None else pv[mp] + d)
        return out

    def accumulate(pv):
        for mp in maps:
            acc[mp] = alpha_ref[mp:mp + 1, :] * acc[mp] + pv[mp]

    def trip(j, first, last):
        m_old = [m_ref[mp:mp + 1, :] for mp in maps]
        m_new = [jnp.maximum(m_old[mp], mx_ref[mp:mp + 1, :]) for mp in maps]
        pv = mx = None
        for r in range(nsub):
            if not first:
                pv = values_sub(j - 1, r, pv)
            softmax_sub(r, m_new)
            if not last:
                mx = scores_sub(j + 1, r, mx)
        if not first:
            accumulate(pv)
        for mp in maps:
            alpha_ref[mp:mp + 1, :] = jnp.exp2(m_old[mp] - m_new[mp])
            m_ref[mp:mp + 1, :] = m_new[mp]
            if not last:
                mx_ref[mp:mp + 1, :] = mx[mp]

    mx = None
    for r in range(nsub):
        mx = scores_sub(0, r, mx)
    for mp in maps:
        mx_ref[mp:mp + 1, :] = mx[mp]
    trip(0, True, False)

    def body(j, carry):
        trip(j, False, False)
        return carry

    lax.fori_loop(1, n - 1, body, 0)
    trip(n - 1, False, True)
    pv = None
    for r in range(nsub):
        pv = values_sub(n - 1, r, pv)
    accumulate(pv)

    lq = lam_ref[...]
    la = jnp.sum(lq[0:1] * lq[1:2], axis=1, keepdims=True)
    lb = jnp.sum(lq[2:3] * lq[3:4], axis=1, keepdims=True)
    lam = jnp.exp(la) - jnp.exp(lb) + lam_init
    a1 = acc[0]
    a2 = acc[1]
    o = a1[0:DV_A] / a1[DV_A:DV_A + 1] - lam * (a2[0:DV_A] / a2[DV_A:DV_A + 1])
    ms = jnp.mean(o * o, axis=0, keepdims=True)
    y = o * lax.rsqrt(ms + LN_EPS) * g_ref[...] * (1.0 - lam_init)
    o_ref[0] = y.T.astype(o_ref.dtype)


def _attn(q, k, vt, lambda_qk, subln_g, lam_init, tq, tk, sub):
    B, S, _ = q.shape
    assert S // tk >= 2 and tk % sub == 0, (S, tk, sub)
    kern = functools.partial(_attn_kernel, tk=tk, sub=sub, lam_init=lam_init)
    return pl.pallas_call(
        kern,
        grid=(B, N_HEADS_A, S // tq),
        in_specs=[pl.BlockSpec((4, DQK_A), lambda b, h, i: (0, 0)),
                  pl.BlockSpec((1, tq, DV_A), lambda b, h, i: (b, i, h)),
                  pl.BlockSpec((1, S, DV_A), lambda b, h, i: (b, 0, h)),
                  pl.BlockSpec((1, DV_A, S), lambda b, h, i: (b, h, 0)),
                  pl.BlockSpec((DV_A, 1), lambda b, h, i: (0, 0))],
        out_specs=pl.BlockSpec((1, tq, DV_A), lambda b, h, i: (b, i, h)),
        out_shape=jax.ShapeDtypeStruct((B, S, N_HEADS_A * DV_A), BF16),
        scratch_shapes=[pltpu.VMEM((2, tk, tq), F32), pltpu.VMEM((2, tk, tq), BF16),
                        pltpu.VMEM((2, DV_A + ONES_ROWS, tq), F32),
                        pltpu.VMEM((2, tq), F32), pltpu.VMEM((2, tq), F32), pltpu.VMEM((2, tq), F32)],
        compiler_params=_params(("parallel", "parallel", "parallel")),
        name="attn",
    )(lambda_qk.astype(F32), q, k, vt, subln_g.reshape(DV_A, 1).astype(F32))


N_STREAMS = 2 * N_HEADS_M
DV_AUG = DV_M + LANES


def _mlstm_kernel(qf_ref, kf_ref, vf_ref, gcf_ref, grf_ref,
                  qb_ref, kb_ref, vb_ref, gcb_ref, grb_ref,
                  hf_ref, hb_ref, ct_ref, m_ref):
    L = CHUNK

    @pl.when(pl.program_id(1) == 0)
    def _():
        ct_ref[...] = jnp.zeros_like(ct_ref)
        m_ref[...] = jnp.zeros_like(m_ref)

    row = lax.broadcasted_iota(jnp.int32, (L, L), 0)
    col = lax.broadcasted_iota(jnp.int32, (L, L), 1)
    ones_blk = jnp.where(lax.broadcasted_iota(jnp.int32, (L, LANES), 1) == 0, 1.0, 0.0).astype(BF16)
    dirs = ((qf_ref, kf_ref, vf_ref, gcf_ref, grf_ref, hf_ref),
            (qb_ref, kb_ref, vb_ref, gcb_ref, grb_ref, hb_ref))
    for d, (q_ref, k_ref, v_ref, gc_ref, gr_ref, h_ref) in enumerate(dirs):
        tri = (col <= row) if d == 0 else (col >= row)
        e = L - 1 if d == 0 else 0
        gc = gc_ref[0]
        gr = gr_ref[0]
        for hd in range(N_HEADS_M):
            st = d * N_HEADS_M + hd
            ci = d * 2 * N_HEADS_M + hd
            cf = ci + N_HEADS_M
            qh = q_ref[0, :, hd * DQK_M:(hd + 1) * DQK_M]
            kh = k_ref[0, :, hd * DQK_M:(hd + 1) * DQK_M]
            va = jnp.concatenate([v_ref[0, :, hd * DV_M:(hd + 1) * DV_M], ones_blk], axis=1)
            i_row = gr[ci:ci + 1, :]
            b_row = gr[cf:cf + 1, :]
            i_col = gc[:, ci:ci + 1]
            b_col = gc[:, cf:cf + 1]
            m_prev = m_ref[st:st + 1, 0:1]
            dm = jnp.where(tri, b_col - b_row + i_row, -jnp.inf)
            inter = b_col + m_prev
            m_t = jnp.maximum(inter, jnp.max(dm, axis=1, keepdims=True))
            w = jnp.exp(dm - m_t)
            w_inter = jnp.exp(inter - m_t)
            sqk = (_dot_nt(qh, kh) * w).astype(BF16)
            ct = ct_ref[st]
            numa = _dot(sqk, va) + w_inter * _dot(qh, ct.astype(BF16))
            den = numa[:, DV_M:DV_M + 1]
            h = numa[:, 0:DV_M] / jnp.maximum(jnp.abs(den), jnp.exp(-m_t))
            h_ref[0, :, hd * DV_M:(hd + 1) * DV_M] = h.astype(h_ref.dtype)
            b_end = b_row[:, e:e + 1]
            m_new = m_t[e:e + 1, :]
            ws = jnp.exp(b_end - b_col + i_col - m_new)
            decay = jnp.exp(b_end + m_prev - m_new)
            kst = (kh.astype(F32) * ws).T.astype(BF16)
            ct_ref[st] = decay * ct + _dot(kst, va)
            m_ref[st:st + 1, :] = jnp.broadcast_to(m_new, (1, LANES))


def _mlstm(mq, mk, mv, gcol, grow):
    B, S, _ = mq.shape
    nc = S // CHUNK
    fwd = lambda n: pl.BlockSpec((1, CHUNK, n), lambda b, c: (b, c, 0))
    bwd = lambda n: pl.BlockSpec((1, CHUNK, n), lambda b, c: (b, nc - 1 - c, 0))
    rowf = pl.BlockSpec((1, W_MG, CHUNK), lambda b, c: (b, 0, c))
    rowb = pl.BlockSpec((1, W_MG, CHUNK), lambda b, c: (b, 0, nc - 1 - c))
    return pl.pallas_call(
        _mlstm_kernel,
        grid=(B, nc),
        in_specs=[fwd(W_MQ), fwd(W_MK), fwd(W_MV), fwd(W_MG), rowf,
                  bwd(W_MQ), bwd(W_MK), bwd(W_MV), bwd(W_MG), rowb],
        out_specs=(fwd(W_MV), bwd(W_MV)),
        out_shape=(jax.ShapeDtypeStruct((B, S, W_MV), BF16),
                   jax.ShapeDtypeStruct((B, S, W_MV), BF16)),
        scratch_shapes=[pltpu.VMEM((N_STREAMS, DQK_M, DV_AUG), F32),
                        pltpu.VMEM((N_STREAMS, LANES), F32)],
        compiler_params=_params(("parallel", "arbitrary")),
        name="mlstm",
    )(mq, mk, mv, gcol, grow, mq, mk, mv, gcol, grow)


def _mix_kernel(x_ref, ada_ref, ya_ref, hf_ref, hb_ref, og_ref, gab_ref,
                wpa_ref, wpm_ref, wout_ref, mng_ref, lg_ref, lb_ref, o_ref, *, alpha):
    hs = hf_ref[0].astype(F32) + hb_ref[0].astype(F32)
    hn = jnp.concatenate(
        [_norm(hs[:, hd * DV_M:(hd + 1) * DV_M]) for hd in range(N_HEADS_M)], axis=1)
    ym = (og_ref[0].astype(F32) * (hn * mng_ref[...])).astype(BF16)
    gab = gab_ref[0].astype(F32)
    pre = gab[:, 0:D_MODEL] * _dot(ya_ref[0], wpa_ref[...]) + gab[:, D_MODEL:] * _dot(ym, wpm_ref[...])
    mix = _dot(pre.astype(BF16), wout_ref[...])
    g1 = ada_ref[0][:, 2 * D_MODEL:3 * D_MODEL]
    o_ref[0] = _norm(alpha * x_ref[0] + g1 * mix) * lg_ref[...] + lb_ref[...]


def _mix(x, ada3, ya, hf, hb, og, gab, wpa, wpm, wout, mnorm_g, ln_g, ln_b, alpha, ts):
    B, S, D = x.shape
    tok = lambda n: pl.BlockSpec((1, ts, n), lambda b, i: (b, i, 0))
    consts = (wpa, wpm, wout, mnorm_g.reshape(1, W_MV).astype(F32),
              ln_g.reshape(1, D).astype(F32), ln_b.reshape(1, D).astype(F32))
    return pl.pallas_call(
        functools.partial(_mix_kernel, alpha=alpha),
        grid=(B, S // ts),
        in_specs=[tok(D), pl.BlockSpec((1, 1, 6 * D), lambda b, i: (b, 0, 0)),
                  tok(W_AV), tok(W_MV), tok(W_MV), tok(W_MO), tok(W_BG)]
                 + [_const_spec(w.shape) for w in consts],
        out_specs=tok(D),
        out_shape=jax.ShapeDtypeStruct((B, S, D), F32),
        compiler_params=_params(("parallel", "parallel")),
        name="mix",
    )(x, ada3, ya, hf, hb, og, gab, *consts)


def _gelu_tanh(x):
    return 0.5 * x * (1.0 + jnp.tanh(math.sqrt(2.0 / math.pi) * (x + 0.044715 * (x * x * x))))


def _ffn_kernel(xp_ref, x_ref, xn_ref, ada_ref, wug_ref, wuv_ref, cw_ref, cb_ref, wd_ref,
                lg_ref, lb_ref, o_ref, *, alpha):
    ts = x_ref.shape[1]
    H = SUBLANES
    i = pl.program_id(1)
    ada = ada_ref[0]
    sh2 = ada[:, 3 * D_MODEL:4 * D_MODEL]
    sc2 = ada[:, 4 * D_MODEL:5 * D_MODEL]
    g2 = ada[:, 5 * D_MODEL:6 * D_MODEL]
    x = x_ref[0]
    xx = jnp.concatenate([xp_ref[0], x, xn_ref[0]], axis=0)
    hx = _norm(xx) * (1.0 + sc2) + sh2
    ug = _dot(hx.astype(BF16), wug_ref[...])
    uv = _dot(hx[H:H + ts].astype(BF16), wuv_ref[...])
    n = ts + 2 * H
    prev = pltpu.roll(ug, 1, 0)[H:H + ts]
    nxt = pltpu.roll(ug, n - 1, 0)[H:H + ts]
    t = lax.broadcasted_iota(jnp.int32, (ts, 1), 0)
    keep_prev = jnp.where((i == 0) & (t == 0), 0.0, 1.0)
    keep_next = jnp.where((i == pl.num_programs(1) - 1) & (t == ts - 1), 0.0, 1.0)
    cw = cw_ref[...]
    conv = (prev * keep_prev * cw[0:1] + ug[H:H + ts] * cw[1:2] + nxt * keep_next * cw[2:3]
            + cb_ref[...])
    y = _dot((_gelu_tanh(conv) * uv).astype(BF16), wd_ref[...])
    o_ref[0] = _norm(alpha * x + g2 * y) * lg_ref[...] + lb_ref[...]


def _ffn(x, ada3, wug, wuv, conv_w, conv_b, wd, ln_g, ln_b, alpha, ts):
    B, S, D = x.shape
    H = SUBLANES
    nb = ts // H
    last = S // H - 1
    tok = pl.BlockSpec((1, ts, D), lambda b, i: (b, i, 0))
    prev = pl.BlockSpec((1, H, D), lambda b, i: (b, jnp.maximum(i * nb - 1, 0), 0))
    nxt = pl.BlockSpec((1, H, D), lambda b, i: (b, jnp.minimum((i + 1) * nb, last), 0))
    consts = (wug, wuv, conv_w.astype(F32), conv_b.reshape(1, D_FF).astype(F32), wd,
              ln_g.reshape(1, D).astype(F32), ln_b.reshape(1, D).astype(F32))
    return pl.pallas_call(
        functools.partial(_ffn_kernel, alpha=alpha),
        grid=(B, S // ts),
        in_specs=[prev, tok, nxt, pl.BlockSpec((1, 1, 6 * D), lambda b, i: (b, 0, 0))]
                 + [_const_spec(w.shape) for w in consts],
        out_specs=tok,
        out_shape=jax.ShapeDtypeStruct((B, S, D), F32),
        compiler_params=_params(("parallel", "parallel")),
        name="ffn",
    )(x, x, x, ada3, *consts)


def _rope_tables(seq):
    half = ROT_DIM // 2
    inv = ROPE_THETA ** (-jnp.arange(0, ROT_DIM, 2, dtype=F32) / ROT_DIM)
    ang = jnp.arange(seq, dtype=F32)[:, None] * inv[None, :]
    cos, sin = jnp.cos(ang), jnp.sin(ang)
    rest = DQK_A - ROT_DIM
    one = jnp.ones((seq, rest), F32)
    zr = jnp.zeros((seq, rest), F32)
    zh = jnp.zeros((seq, half), F32)
    reps = LANES // DQK_A
    ct = jnp.tile(jnp.concatenate([cos, cos, one], axis=1), (1, reps))
    s1 = jnp.tile(jnp.concatenate([-sin, zh, zr], axis=1), (1, reps))
    s2 = jnp.tile(jnp.concatenate([zh, sin, zr], axis=1), (1, reps))
    return ct, s1, s2


def _layer_weights(w_in, w_pa, w_pm, w_out, w_up, w_down):
    seg = lambda j: w_in[:, SEG_STARTS[j]:SEG_STARTS[j] + SEG_WIDTHS[j]]
    wg = seg(7)
    proj = (seg(0).astype(BF16), seg(1).astype(BF16), seg(2).T.astype(BF16),
            seg(3).astype(BF16), seg(4).astype(BF16), seg(5).astype(BF16), seg(6).astype(BF16),
            jnp.pad(wg, ((0, 0), (0, LANES - W_MG))).astype(BF16), wg.T.astype(BF16),
            seg(8).astype(BF16))
    return (proj, w_pa.astype(BF16), w_pm.astype(BF16), w_out.astype(BF16),
            w_up[:, :D_FF].astype(BF16), w_up[:, D_FF:].astype(BF16), w_down.astype(BF16))


def _tile(seq, want):
    t = min(seq, want)
    assert seq % t == 0, (seq, want)
    return t


def _default_tiles(seq):
    return (512, 512, min(2048, max(seq // 4, 256)), 256)


def _encoder_layer(x, ada, l, depth, weights, lambda_qk, subln_g, b_mgate, mnorm_g,
                   ln1_g, ln1_b, conv_w, conv_b, ln2_g, ln2_b, tiles):
    B, S, D = x.shape
    ts, tq, tk, sub = (_tile(S, t) for t in (tiles or _default_tiles(S)))
    alpha = (2.0 * depth) ** 0.25
    lam_init = 0.8 - 0.6 * math.exp(-0.3 * l)
    proj_w, wpa, wpm, wout, wug, wuv, wd = weights
    ada3 = ada.reshape(B, 1, 6 * D)
    q, k, vt, mq, mk, mv, og, gcol, grow, gab = _proj(x, ada3, _rope_tables(S), proj_w, b_mgate, ts)
    ya = _attn(q, k, vt, lambda_qk, subln_g, lam_init, tq, tk, sub)
    hf, hb = _mlstm(mq, mk, mv, gcol, grow)
    x1 = _mix(x, ada3, ya, hf, hb, og, gab, wpa, wpm, wout, mnorm_g, ln1_g, ln1_b, alpha, ts)
    return _ffn(x1, ada3, wug, wuv, conv_w, conv_b, wd, ln2_g, ln2_b, alpha, ts)


def _trunk(xs, cs, w_ada, b_ada, w_in, lambda_qk, subln_g, b_mgate, mnorm_g, w_pa, w_pm, w_out,
           ln1_g, ln1_b, w_up, conv_w, conv_b, w_down, ln2_g, ln2_b, tiles=None):
    depth = w_in.shape[0]
    rows = [c.shape[0] for c in cs]
    c_all = jnp.concatenate(cs, axis=0)
    xs = list(xs)
    for l in range(depth):
        ada_all = _ada(c_all, w_ada[l], b_ada[l])
        weights = _layer_weights(w_in[l], w_pa[l], w_pm[l], w_out[l], w_up[l], w_down[l])
        start = 0
        for g, n in enumerate(rows):
            xs[g] = _encoder_layer(xs[g], ada_all[start:start + n], l, depth, weights,
                                   lambda_qk[l], subln_g[l], b_mgate[l], mnorm_g[l],
                                   ln1_g[l], ln1_b[l], conv_w[l], conv_b[l], ln2_g[l], ln2_b[l], tiles)
            start += n
    return tuple(xs)


def kernel(x_prompt, x_sample, c_prompt, c_sample, w_ada, b_ada, w_in, lambda_qk, subln_g, b_mgate, mnorm_g, w_pa, w_pm, w_out, ln1_g, ln1_b, w_up, conv_w, conv_b, w_down, ln2_g, ln2_b):
    return _trunk((x_prompt, x_sample), (c_prompt, c_sample), w_ada, b_ada, w_in, lambda_qk, subln_g,
                  b_mgate, mnorm_g, w_pa, w_pm, w_out, ln1_g, ln1_b, w_up, conv_w, conv_b, w_down,
                  ln2_g, ln2_b)
```

```python
import functools
import math

import jax
import jax.numpy as jnp
import numpy as np
from jax import lax
from jax.experimental import pallas as pl
from jax.experimental.pallas import tpu as pltpu

F32 = jnp.float32
BF16 = jnp.bfloat16

D_MODEL = 1024
N_HEADS_A = 8
DQK_A = 64
DV_A = 2 * DQK_A
ROT_DIM = DQK_A // 4
ROPE_THETA = 500000.0
N_HEADS_M = 4
DQK_M = 128
DV_M = 256
CHUNK = 128
D_FF = 2816
LN_EPS = 1e-5

W_AQ = N_HEADS_A * 2 * DQK_A
W_AK = N_HEADS_A * 2 * DQK_A
W_AV = N_HEADS_A * DV_A
W_MQ = N_HEADS_M * DQK_M
W_MK = N_HEADS_M * DQK_M
W_MV = N_HEADS_M * DV_M
W_MO = N_HEADS_M * DV_M
W_MG = 2 * 2 * N_HEADS_M
W_BG = 2 * D_MODEL
SEG_WIDTHS = (W_AQ, W_AK, W_AV, W_MQ, W_MK, W_MV, W_MO, W_MG, W_BG)
SEG_STARTS = [0] + [int(s) for s in np.cumsum(SEG_WIDTHS)[:-1]]

LANES = 128
SUBLANES = 8
LOG2E = 1.4426950408889634
VMEM_LIMIT = 56 * 1024 * 1024

NT_DIMS = (((1,), (1,)), ((), ()))


def _dot(a, b):
    return jnp.dot(a, b, preferred_element_type=F32)


def _dot_nt(a, b):
    return lax.dot_general(a, b, NT_DIMS, preferred_element_type=F32)


def _norm(x):
    mu = jnp.mean(x, axis=-1, keepdims=True)
    xc = x - mu
    var = jnp.mean(xc * xc, axis=-1, keepdims=True)
    return xc * lax.rsqrt(var + LN_EPS)


def _split_bf16(a):
    hi = a.astype(BF16)
    lo = (a - hi.astype(F32)).astype(BF16)
    return hi, lo


def _const_spec(shape):
    nd = len(shape)
    return pl.BlockSpec(shape, lambda *_: (0,) * nd, pipeline_mode=pl.Buffered(1))


def _params(semantics):
    return pltpu.CompilerParams(dimension_semantics=semantics, vmem_limit_bytes=VMEM_LIMIT)


def _ada_kernel(c_ref, w_ref, b_ref, o_ref):
    c = c_ref[...]
    a = c * jax.nn.sigmoid(c)
    a_hi, a_lo = _split_bf16(a)
    w_hi, w_lo = _split_bf16(w_ref[...])
    o_ref[...] = _dot(a_hi, w_hi) + _dot(a_lo, w_hi) + _dot(a_hi, w_lo) + b_ref[...]


def _ada(c, w_ada, b_ada):
    rows, d = c.shape
    n = w_ada.shape[1]
    tn = 1024
    return pl.pallas_call(
        _ada_kernel,
        grid=(n // tn,),
        in_specs=[pl.BlockSpec((rows, d), lambda j: (0, 0)),
                  pl.BlockSpec((d, tn), lambda j: (0, j)),
                  pl.BlockSpec((1, tn), lambda j: (0, j))],
        out_specs=pl.BlockSpec((rows, tn), lambda j: (0, j)),
        out_shape=jax.ShapeDtypeStruct((rows, n), F32),
        compiler_params=_params(("parallel",)),
        name="ada",
    )(c, w_ada, b_ada.reshape(1, n))


def _log_sigmoid(g):
    return jnp.minimum(g, 0.0) - jnp.log1p(jnp.exp(-jnp.abs(g)))


def _proj_kernel(x_ref, ada_ref, ct_ref, s1_ref, s2_ref,
                 wq_ref, wk_ref, wvt_ref, wmq_ref, wmk_ref, wmv_ref, wmo_ref,
                 wgc_ref, wgr_ref, bgc_ref, bgr_ref, wbg_ref,
                 q_ref, k_ref, vt_ref, mq_ref, mk_ref, mv_ref, og_ref, gc_ref, gr_ref, gab_ref):
    ts = x_ref.shape[1]
    ada = ada_ref[0]
    sh1 = ada[:, 0:D_MODEL]
    sc1 = ada[:, D_MODEL:2 * D_MODEL]
    h = (_norm(x_ref[0]) * (1.0 + sc1) + sh1).astype(BF16)

    ct = ct_ref[...]
    s1 = s1_ref[...]
    s2 = s2_ref[...]

    def rope_store(w_ref, o_ref, scale):
        a = _dot(h, w_ref[...])
        for j in range(W_AQ // LANES):
            blk = a[:, j * LANES:(j + 1) * LANES]
            r = (blk * ct + pltpu.roll(blk, LANES - ROT_DIM // 2, 1) * s1
                 + pltpu.roll(blk, ROT_DIM // 2, 1) * s2)
            o_ref[0, :, j * LANES:(j + 1) * LANES] = (r * scale).astype(o_ref.dtype)

    rope_store(wq_ref, q_ref, (DQK_A ** -0.5) * LOG2E)
    rope_store(wk_ref, k_ref, 1.0)
    vt_ref[0] = _dot_nt(wvt_ref[...], h).astype(vt_ref.dtype)
    mq_ref[0] = _dot(h, wmq_ref[...]).astype(mq_ref.dtype)
    mk_ref[0] = (_dot(h, wmk_ref[...]) * (DQK_M ** -0.5)).astype(mk_ref.dtype)
    mv_ref[0] = _dot(h, wmv_ref[...]).astype(mv_ref.dtype)
    og_ref[0] = jax.nn.sigmoid(_dot(h, wmo_ref[...])).astype(og_ref.dtype)
    gab_ref[0] = jax.nn.sigmoid(_dot(h, wbg_ref[...])).astype(gab_ref.dtype)

    r = lax.broadcasted_iota(jnp.int32, (ts, ts), 0)
    c = lax.broadcasted_iota(jnp.int32, (ts, ts), 1)
    shift = CHUNK.bit_length() - 1
    same = jnp.right_shift(r, shift) == jnp.right_shift(c, shift)
    lower = jnp.where(same & (c <= r), 1.0, 0.0).astype(BF16)
    upper = jnp.where(same & (c >= r), 1.0, 0.0).astype(BF16)

    def gate_parts(g, ch):
        is_f = (ch & (2 * N_HEADS_M - 1)) >= N_HEADS_M
        val = jnp.where(is_f, _log_sigmoid(g), g)
        vf = jnp.where(is_f & (ch < 8), val, 0.0)
        vb = jnp.where(is_f & (ch >= 8) & (ch < W_MG), val, 0.0)
        vi = jnp.where(is_f, 0.0, val)
        return _split_bf16(vf), _split_bf16(vb), vi

    g = _dot(h, wgc_ref[...]) + bgc_ref[...]
    (f_hi, f_lo), (b_hi, b_lo), vi = gate_parts(g, lax.broadcasted_iota(jnp.int32, g.shape, 1))
    gcol = _dot(lower, f_hi) + _dot(lower, f_lo) + _dot(upper, b_hi) + _dot(upper, b_lo) + vi
    gc_ref[0] = gcol[:, 0:W_MG]

    g = _dot_nt(wgr_ref[...], h) + bgr_ref[...]
    (f_hi, f_lo), (b_hi, b_lo), vi = gate_parts(g, lax.broadcasted_iota(jnp.int32, g.shape, 0))
    gr_ref[0] = _dot(f_hi, upper) + _dot(f_lo, upper) + _dot(b_hi, lower) + _dot(b_lo, lower) + vi


def _proj(x, ada3, rope, wts, b_mgate, ts):
    B, S, D = x.shape
    ct, s1, s2 = rope
    wq, wk, wvt, wmq, wmk, wmv, wmo, wgc, wgr, wbg = wts
    bgc = jnp.pad(b_mgate.reshape(1, W_MG).astype(F32), ((0, 0), (0, LANES - W_MG)))
    bgr = b_mgate.reshape(W_MG, 1).astype(F32)
    tok = lambda n: pl.BlockSpec((1, ts, n), lambda b, i: (b, i, 0))
    tab = pl.BlockSpec((ts, LANES), lambda b, i: (i, 0))
    consts = (wq, wk, wvt, wmq, wmk, wmv, wmo, wgc, wgr, bgc, bgr, wbg)
    out_shape = (
        jax.ShapeDtypeStruct((B, S, W_AQ), BF16),
        jax.ShapeDtypeStruct((B, S, W_AK), BF16),
        jax.ShapeDtypeStruct((B, W_AV, S), BF16),
        jax.ShapeDtypeStruct((B, S, W_MQ), BF16),
        jax.ShapeDtypeStruct((B, S, W_MK), BF16),
        jax.ShapeDtypeStruct((B, S, W_MV), BF16),
        jax.ShapeDtypeStruct((B, S, W_MO), BF16),
        jax.ShapeDtypeStruct((B, S, W_MG), F32),
        jax.ShapeDtypeStruct((B, W_MG, S), F32),
        jax.ShapeDtypeStruct((B, S, W_BG), BF16),
    )
    out_specs = (
        tok(W_AQ), tok(W_AK),
        pl.BlockSpec((1, W_AV, ts), lambda b, i: (b, 0, i)),
        tok(W_MQ), tok(W_MK), tok(W_MV), tok(W_MO), tok(W_MG),
        pl.BlockSpec((1, W_MG, ts), lambda b, i: (b, 0, i)),
        tok(W_BG),
    )
    return pl.pallas_call(
        _proj_kernel,
        grid=(B, S // ts),
        in_specs=[tok(D), pl.BlockSpec((1, 1, 6 * D), lambda b, i: (b, 0, 0)), tab, tab, tab]
                 + [_const_spec(w.shape) for w in consts],
        out_specs=out_specs,
        out_shape=out_shape,
        compiler_params=_params(("parallel", "parallel")),
        name="proj",
    )(x, ada3, ct, s1, s2, *consts)


ONES_ROWS = 16


def _attn_kernel(lam_ref, q_ref, k_ref, vt_ref, g_ref, o_ref,
                 s_ref, p_ref, acc, m_ref, mx_ref, alpha_ref, *, tk, sub, lam_init):
    n = k_ref.shape[1] // tk
    nsub = tk // sub
    q = q_ref[0]
    lane = lax.broadcasted_iota(jnp.int32, q.shape, 1)
    zero = jnp.zeros_like(q)
    qs = (jnp.where(lane < DQK_A, q, zero), jnp.where(lane >= DQK_A, q, zero))
    ones_rows = jnp.ones((ONES_ROWS, sub), BF16)
    maps = range(2)

    acc[...] = jnp.zeros_like(acc)
    m_ref[...] = jnp.full_like(m_ref, -jnp.inf)

    def scores_sub(j, r, mx):
        kc = k_ref[0, pl.ds(pl.multiple_of(j * tk + r * sub, sub), sub), :]
        out = []
        for mp in maps:
            s = _dot_nt(kc, qs[mp])
            s_ref[mp, r * sub:(r + 1) * sub, :] = s
            cm = jnp.max(s, axis=0, keepdims=True)
            out.append(cm if mx is None else jnp.maximum(mx[mp], cm))
        return out

    def softmax_sub(r, m_new):
        for mp in maps:
            s = s_ref[mp, r * sub:(r + 1) * sub, :]
            p_ref[mp, r * sub:(r + 1) * sub, :] = jnp.exp2(s - m_new[mp]).astype(BF16)

    def values_sub(j, r, pv):
        vc = vt_ref[0, :, pl.ds(pl.multiple_of(j * tk + r * sub, sub), sub)]
        va = jnp.concatenate([vc, ones_rows], axis=0)
        out = []
        for mp in maps:
            d = _dot(va, p_ref[mp, r * sub:(r + 1) * sub, :])
            out.append(d if pv is None else pv[mp] + d)
        return out

    def accumulate(pv):
        for mp in maps:
            acc[mp] = alpha_ref[mp] * acc[mp] + pv[mp]

    def trip(j, first, last):
        m_old = [m_ref[mp] for mp in maps]
        m_new = [jnp.maximum(m_old[mp], mx_ref[mp]) for mp in maps]
        pv = mx = None
        for r in range(nsub):
            if not first:
                pv = values_sub(j - 1, r, pv)
            softmax_sub(r, m_new)
            if not last:
                mx = scores_sub(j + 1, r, mx)
        if not first:
            accumulate(pv)
        for mp in maps:
            alpha_ref[mp] = jnp.exp2(m_old[mp] - m_new[mp])
            m_ref[mp] = m_new[mp]
            if not last:
                mx_ref[mp] = mx[mp]

    mx = None
    for r in range(nsub):
        mx = scores_sub(0, r, mx)
    for mp in maps:
        mx_ref[mp] = mx[mp]
    trip(0, True, False)

    def body(j, carry):
        trip(j, False, False)
        return carry

    lax.fori_loop(1, n - 1, body, 0)
    trip(n - 1, False, True)
    pv = None
    for r in range(nsub):
        pv = values_sub(n - 1, r, pv)
    accumulate(pv)

    lq = lam_ref[...]
    la = jnp.sum(lq[0:1] * lq[1:2], axis=1, keepdims=True)
    lb = jnp.sum(lq[2:3] * lq[3:4], axis=1, keepdims=True)
    lam = jnp.exp(la) - jnp.exp(lb) + lam_init
    a1 = acc[0]
    a2 = acc[1]
    o = a1[0:DV_A] / a1[DV_A:DV_A + 1] - lam * (a2[0:DV_A] / a2[DV_A:DV_A + 1])
    ms = jnp.mean(o * o, axis=0, keepdims=True)
    y = o * lax.rsqrt(ms + LN_EPS) * g_ref[...] * (1.0 - lam_init)
    o_ref[0] = y.T.astype(o_ref.dtype)


def _attn(q, k, vt, lambda_qk, subln_g, lam_init, tq, tk, sub):
    B, S, _ = q.shape
    assert S // tk >= 2 and tk % sub == 0, (S, tk, sub)
    kern = functools.partial(_attn_kernel, tk=tk, sub=sub, lam_init=lam_init)
    return pl.pallas_call(
        kern,
        grid=(B, N_HEADS_A, S // tq),
        in_specs=[pl.BlockSpec((4, DQK_A), lambda b, h, i: (0, 0)),
                  pl.BlockSpec((1, tq, DV_A), lambda b, h, i: (b, i, h)),
                  pl.BlockSpec((1, S, DV_A), lambda b, h, i: (b, 0, h)),
                  pl.BlockSpec((1, DV_A, S), lambda b, h, i: (b, h, 0)),
                  pl.BlockSpec((DV_A, 1), lambda b, h, i: (0, 0))],
        out_specs=pl.BlockSpec((1, tq, DV_A), lambda b, h, i: (b, i, h)),
        out_shape=jax.ShapeDtypeStruct((B, S, N_HEADS_A * DV_A), BF16),
        scratch_shapes=[pltpu.VMEM((2, tk, tq), F32), pltpu.VMEM((2, tk, tq), BF16),
                        pltpu.VMEM((2, DV_A + ONES_ROWS, tq), F32),
                        pltpu.VMEM((2, 1, tq), F32), pltpu.VMEM((2, 1, tq), F32),
                        pltpu.VMEM((2, 1, tq), F32)],
        compiler_params=_params(("parallel", "parallel", "parallel")),
        name="attn",
    )(lambda_qk.astype(F32), q, k, vt, subln_g.reshape(DV_A, 1).astype(F32))


N_STREAMS = 2 * N_HEADS_M
DV_AUG = DV_M + LANES


def _mlstm_kernel(qf_ref, kf_ref, vf_ref, gcf_ref, grf_ref,
                  qb_ref, kb_ref, vb_ref, gcb_ref, grb_ref,
                  hf_ref, hb_ref, ct_ref, m_ref):
    L = CHUNK

    @pl.when(pl.program_id(1) == 0)
    def _():
        ct_ref[...] = jnp.zeros_like(ct_ref)
        m_ref[...] = jnp.zeros_like(m_ref)

    row = lax.broadcasted_iota(jnp.int32, (L, L), 0)
    col = lax.broadcasted_iota(jnp.int32, (L, L), 1)
    assert L == LANES
    ones_blk = jnp.ones((L, LANES), BF16)

    def lane_tile(a, width):
        return jnp.concatenate([a] * (width // LANES), axis=1)

    dirs = ((qf_ref, kf_ref, vf_ref, gcf_ref, grf_ref, hf_ref),
            (qb_ref, kb_ref, vb_ref, gcb_ref, grb_ref, hb_ref))
    m_all = m_ref[...]
    m_out = []
    for d, (q_ref, k_ref, v_ref, gc_ref, gr_ref, h_ref) in enumerate(dirs):
        tri = (col <= row) if d == 0 else (col >= row)
        e = L - 1 if d == 0 else 0
        gc = gc_ref[0]
        gr = gr_ref[0]
        for hd in range(N_HEADS_M):
            st = d * N_HEADS_M + hd
            ci = d * 2 * N_HEADS_M + hd
            cf = ci + N_HEADS_M
            qh = q_ref[0, :, hd * DQK_M:(hd + 1) * DQK_M]
            kh = k_ref[0, :, hd * DQK_M:(hd + 1) * DQK_M]
            va = jnp.concatenate([v_ref[0, :, hd * DV_M:(hd + 1) * DV_M], ones_blk], axis=1)
            i_row = gr[ci:ci + 1, :]
            b_row = gr[cf:cf + 1, :]
            b_t = jnp.broadcast_to(gc[:, cf:cf + 1], (L, LANES))
            m_prev = m_all[st:st + 1, :]
            dm = jnp.where(tri, b_t - b_row + i_row, -jnp.inf)
            inter = b_t + m_prev
            m_t = jnp.maximum(inter, jnp.broadcast_to(jnp.max(dm, axis=1, keepdims=True), (L, LANES)))
            w = jnp.exp(dm - m_t)
            w_inter = jnp.exp(inter - m_t)
            sqk = (_dot_nt(qh, kh) * w).astype(BF16)
            ct = ct_ref[st]
            numa = _dot(sqk, va) + lane_tile(w_inter, DV_AUG) * _dot(qh, ct.astype(BF16))
            den = numa[:, DV_M:DV_AUG]
            inv = 1.0 / jnp.maximum(jnp.abs(den), jnp.exp(-m_t))
            h = numa[:, 0:DV_M] * lane_tile(inv, DV_M)
            h_ref[0, :, hd * DV_M:(hd + 1) * DV_M] = h.astype(h_ref.dtype)
            b_end = b_row[:, e:e + 1]
            m_new = m_t[e:e + 1, :]
            ws = jnp.exp(b_end - b_row + i_row - m_new)
            decay = jnp.exp(b_end + m_prev - m_new)
            kst = (kh.astype(F32).T * ws).astype(BF16)
            ct_ref[st] = lane_tile(decay, DV_AUG) * ct + _dot(kst, va)
            m_out.append(m_new)
    m_ref[...] = jnp.concatenate(m_out, axis=0)


def _mlstm(mq, mk, mv, gcol, grow):
    B, S, _ = mq.shape
    nc = S // CHUNK
    fwd = lambda n: pl.BlockSpec((1, CHUNK, n), lambda b, c: (b, c, 0))
    bwd = lambda n: pl.BlockSpec((1, CHUNK, n), lambda b, c: (b, nc - 1 - c, 0))
    rowf = pl.BlockSpec((1, W_MG, CHUNK), lambda b, c: (b, 0, c))
    rowb = pl.BlockSpec((1, W_MG, CHUNK), lambda b, c: (b, 0, nc - 1 - c))
    return pl.pallas_call(
        _mlstm_kernel,
        grid=(B, nc),
        in_specs=[fwd(W_MQ), fwd(W_MK), fwd(W_MV), fwd(W_MG), rowf,
                  bwd(W_MQ), bwd(W_MK), bwd(W_MV), bwd(W_MG), rowb],
        out_specs=(fwd(W_MV), bwd(W_MV)),
        out_shape=(jax.ShapeDtypeStruct((B, S, W_MV), BF16),
                   jax.ShapeDtypeStruct((B, S, W_MV), BF16)),
        scratch_shapes=[pltpu.VMEM((N_STREAMS, DQK_M, DV_AUG), F32),
                        pltpu.VMEM((N_STREAMS, LANES), F32)],
        compiler_params=_params(("parallel", "arbitrary")),
        name="mlstm",
    )(mq, mk, mv, gcol, grow, mq, mk, mv, gcol, grow)


def _mix_kernel(x_ref, ada_ref, ya_ref, hf_ref, hb_ref, og_ref, gab_ref,
                wpa_ref, wpm_ref, wout_ref, mng_ref, lg_ref, lb_ref, o_ref, *, alpha):
    hs = hf_ref[0].astype(F32) + hb_ref[0].astype(F32)
    hn = jnp.concatenate(
        [_norm(hs[:, hd * DV_M:(hd + 1) * DV_M]) for hd in range(N_HEADS_M)], axis=1)
    ym = (og_ref[0].astype(F32) * (hn * mng_ref[...])).astype(BF16)
    gab = gab_ref[0].astype(F32)
    pre = gab[:, 0:D_MODEL] * _dot(ya_ref[0], wpa_ref[...]) + gab[:, D_MODEL:] * _dot(ym, wpm_ref[...])
    mix = _dot(pre.astype(BF16), wout_ref[...])
    g1 = ada_ref[0][:, 2 * D_MODEL:3 * D_MODEL]
    o_ref[0] = _norm(alpha * x_ref[0] + g1 * mix) * lg_ref[...] + lb_ref[...]


def _mix(x, ada3, ya, hf, hb, og, gab, wpa, wpm, wout, mnorm_g, ln_g, ln_b, alpha, ts):
    B, S, D = x.shape
    tok = lambda n: pl.BlockSpec((1, ts, n), lambda b, i: (b, i, 0))
    consts = (wpa, wpm, wout, mnorm_g.reshape(1, W_MV).astype(F32),
              ln_g.reshape(1, D).astype(F32), ln_b.reshape(1, D).astype(F32))
    return pl.pallas_call(
        functools.partial(_mix_kernel, alpha=alpha),
        grid=(B, S // ts),
        in_specs=[tok(D), pl.BlockSpec((1, 1, 6 * D), lambda b, i: (b, 0, 0)),
                  tok(W_AV), tok(W_MV), tok(W_MV), tok(W_MO), tok(W_BG)]
                 + [_const_spec(w.shape) for w in consts],
        out_specs=tok(D),
        out_shape=jax.ShapeDtypeStruct((B, S, D), F32),
        compiler_params=_params(("parallel", "parallel")),
        name="mix",
    )(x, ada3, ya, hf, hb, og, gab, *consts)


def _gelu_tanh(x):
    return 0.5 * x * (1.0 + jnp.tanh(math.sqrt(2.0 / math.pi) * (x + 0.044715 * (x * x * x))))


def _ffn_kernel(xp_ref, x_ref, xn_ref, ada_ref, wug_ref, wuv_ref, cw_ref, cb_ref, wd_ref,
                lg_ref, lb_ref, o_ref, *, alpha):
    ts = x_ref.shape[1]
    H = SUBLANES
    i = pl.program_id(1)
    ada = ada_ref[0]
    sh2 = ada[:, 3 * D_MODEL:4 * D_MODEL]
    sc2 = ada[:, 4 * D_MODEL:5 * D_MODEL]
    g2 = ada[:, 5 * D_MODEL:6 * D_MODEL]
    x = x_ref[0]
    xx = jnp.concatenate([xp_ref[0], x, xn_ref[0]], axis=0)
    hx = _norm(xx) * (1.0 + sc2) + sh2
    ug = _dot(hx.astype(BF16), wug_ref[...])
    uv = _dot(hx[H:H + ts].astype(BF16), wuv_ref[...])
    n = ts + 2 * H
    prev = pltpu.roll(ug, 1, 0)[H:H + ts]
    nxt = pltpu.roll(ug, n - 1, 0)[H:H + ts]
    t = lax.broadcasted_iota(jnp.int32, (ts, 1), 0)
    keep_prev = jnp.where((i == 0) & (t == 0), 0.0, 1.0)
    keep_next = jnp.where((i == pl.num_programs(1) - 1) & (t == ts - 1), 0.0, 1.0)
    cw = cw_ref[...]
    conv = (prev * keep_prev * cw[0:1] + ug[H:H + ts] * cw[1:2] + nxt * keep_next * cw[2:3]
            + cb_ref[...])
    y = _dot((_gelu_tanh(conv) * uv).astype(BF16), wd_ref[...])
    o_ref[0] = _norm(alpha * x + g2 * y) * lg_ref[...] + lb_ref[...]


def _ffn(x, ada3, wug, wuv, conv_w, conv_b, wd, ln_g, ln_b, alpha, ts):
    B, S, D = x.shape
    H = SUBLANES
    nb = ts // H
    last = S // H - 1
    tok = pl.BlockSpec((1, ts, D), lambda b, i: (b, i, 0))
    prev = pl.BlockSpec((1, H, D), lambda b, i: (b, jnp.maximum(i * nb - 1, 0), 0))
    nxt = pl.BlockSpec((1, H, D), lambda b, i: (b, jnp.minimum((i + 1) * nb, last), 0))
    consts = (wug, wuv, conv_w.astype(F32), conv_b.reshape(1, D_FF).astype(F32), wd,
              ln_g.reshape(1, D).astype(F32), ln_b.reshape(1, D).astype(F32))
    return pl.pallas_call(
        functools.partial(_ffn_kernel, alpha=alpha),
        grid=(B, S // ts),
        in_specs=[prev, tok, nxt, pl.BlockSpec((1, 1, 6 * D), lambda b, i: (b, 0, 0))]
                 + [_const_spec(w.shape) for w in consts],
        out_specs=tok,
        out_shape=jax.ShapeDtypeStruct((B, S, D), F32),
        compiler_params=_params(("parallel", "parallel")),
        name="ffn",
    )(x, x, x, ada3, *consts)


def _rope_tables(seq):
    half = ROT_DIM // 2
    inv = ROPE_THETA ** (-jnp.arange(0, ROT_DIM, 2, dtype=F32) / ROT_DIM)
    ang = jnp.arange(seq, dtype=F32)[:, None] * inv[None, :]
    cos, sin = jnp.cos(ang), jnp.sin(ang)
    rest = DQK_A - ROT_DIM
    one = jnp.ones((seq, rest), F32)
    zr = jnp.zeros((seq, rest), F32)
    zh = jnp.zeros((seq, half), F32)
    reps = LANES // DQK_A
    ct = jnp.tile(jnp.concatenate([cos, cos, one], axis=1), (1, reps))
    s1 = jnp.tile(jnp.concatenate([-sin, zh, zr], axis=1), (1, reps))
    s2 = jnp.tile(jnp.concatenate([zh, sin, zr], axis=1), (1, reps))
    return ct, s1, s2


def _layer_weights(w_in, w_pa, w_pm, w_out, w_up, w_down):
    seg = lambda j: w_in[:, SEG_STARTS[j]:SEG_STARTS[j] + SEG_WIDTHS[j]]
    wg = seg(7)
    proj = (seg(0).astype(BF16), seg(1).astype(BF16), seg(2).T.astype(BF16),
            seg(3).astype(BF16), seg(4).astype(BF16), seg(5).astype(BF16), seg(6).astype(BF16),
            jnp.pad(wg, ((0, 0), (0, LANES - W_MG))).astype(BF16), wg.T.astype(BF16),
            seg(8).astype(BF16))
    return (proj, w_pa.astype(BF16), w_pm.astype(BF16), w_out.astype(BF16),
            w_up[:, :D_FF].astype(BF16), w_up[:, D_FF:].astype(BF16), w_down.astype(BF16))


def _tile(seq, want):
    t = min(seq, want)
    assert seq % t == 0, (seq, want)
    return t


def _default_tiles(seq):
    return (512, 512, min(2048, max(seq // 4, 256)), 256)


def _encoder_layer(x, ada, l, depth, weights, lambda_qk, subln_g, b_mgate, mnorm_g,
                   ln1_g, ln1_b, conv_w, conv_b, ln2_g, ln2_b, tiles):
    B, S, D = x.shape
    ts, tq, tk, sub = (_tile(S, t) for t in (tiles or _default_tiles(S)))
    alpha = (2.0 * depth) ** 0.25
    lam_init = 0.8 - 0.6 * math.exp(-0.3 * l)
    proj_w, wpa, wpm, wout, wug, wuv, wd = weights
    ada3 = ada.reshape(B, 1, 6 * D)
    q, k, vt, mq, mk, mv, og, gcol, grow, gab = _proj(x, ada3, _rope_tables(S), proj_w, b_mgate, ts)
    ya = _attn(q, k, vt, lambda_qk, subln_g, lam_init, tq, tk, sub)
    hf, hb = _mlstm(mq, mk, mv, gcol, grow)
    x1 = _mix(x, ada3, ya, hf, hb, og, gab, wpa, wpm, wout, mnorm_g, ln1_g, ln1_b, alpha, ts)
    return _ffn(x1, ada3, wug, wuv, conv_w, conv_b, wd, ln2_g, ln2_b, alpha, ts)


def _trunk(xs, cs, w_ada, b_ada, w_in, lambda_qk, subln_g, b_mgate, mnorm_g, w_pa, w_pm, w_out,
           ln1_g, ln1_b, w_up, conv_w, conv_b, w_down, ln2_g, ln2_b, tiles=None):
    depth = w_in.shape[0]
    rows = [c.shape[0] for c in cs]
    c_all = jnp.concatenate(cs, axis=0)
    xs = list(xs)
    for l in range(depth):
        ada_all = _ada(c_all, w_ada[l], b_ada[l])
        weights = _layer_weights(w_in[l], w_pa[l], w_pm[l], w_out[l], w_up[l], w_down[l])
        start = 0
        for g, n in enumerate(rows):
            xs[g] = _encoder_layer(xs[g], ada_all[start:start + n], l, depth, weights,
                                   lambda_qk[l], subln_g[l], b_mgate[l], mnorm_g[l],
                                   ln1_g[l], ln1_b[l], conv_w[l], conv_b[l], ln2_g[l], ln2_b[l], tiles)
            start += n
    return tuple(xs)


def kernel(x_prompt, x_sample, c_prompt, c_sample, w_ada, b_ada, w_in, lambda_qk, subln_g, b_mgate, mnorm_g, w_pa, w_pm, w_out, ln1_g, ln1_b, w_up, conv_w, conv_b, w_down, ln2_g, ln2_b):
    return _trunk((x_prompt, x_sample), (c_prompt, c_sample), w_ada, b_ada, w_in, lambda_qk, subln_g,
                  b_mgate, mnorm_g, w_pa, w_pm, w_out, ln1_g, ln1_b, w_up, conv_w, conv_b, w_down,
                  ln2_g, ln2_b)
```

```python
import functools
import math

import jax
import jax.numpy as jnp
import numpy as np
from jax import lax
from jax.experimental import pallas as pl
from jax.experimental.pallas import tpu as pltpu

F32 = jnp.float32
BF16 = jnp.bfloat16

D_MODEL = 1024
N_HEADS_A = 8
DQK_A = 64
DV_A = 2 * DQK_A
ROT_DIM = DQK_A // 4
ROPE_THETA = 500000.0
N_HEADS_M = 4
DQK_M = 128
DV_M = 256
CHUNK = 128
D_FF = 2816
LN_EPS = 1e-5

W_AQ = N_HEADS_A * 2 * DQK_A
W_AK = N_HEADS_A * 2 * DQK_A
W_AV = N_HEADS_A * DV_A
W_MQ = N_HEADS_M * DQK_M
W_MK = N_HEADS_M * DQK_M
W_MV = N_HEADS_M * DV_M
W_MO = N_HEADS_M * DV_M
W_MG = 2 * 2 * N_HEADS_M
W_BG = 2 * D_MODEL
SEG_WIDTHS = (W_AQ, W_AK, W_AV, W_MQ, W_MK, W_MV, W_MO, W_MG, W_BG)
SEG_STARTS = [0] + [int(s) for s in np.cumsum(SEG_WIDTHS)[:-1]]

LANES = 128
SUBLANES = 8
LOG2E = 1.4426950408889634
VMEM_LIMIT = 56 * 1024 * 1024

NT_DIMS = (((1,), (1,)), ((), ()))


def _dot(a, b):
    return jnp.dot(a, b, preferred_element_type=F32)


def _dot_nt(a, b):
    return lax.dot_general(a, b, NT_DIMS, preferred_element_type=F32)


def _norm(x):
    mu = jnp.mean(x, axis=-1, keepdims=True)
    xc = x - mu
    var = jnp.mean(xc * xc, axis=-1, keepdims=True)
    return xc * lax.rsqrt(var + LN_EPS)


def _split_bf16(a):
    hi = a.astype(BF16)
    lo = (a - hi.astype(F32)).astype(BF16)
    return hi, lo


def _const_spec(shape):
    nd = len(shape)
    return pl.BlockSpec(shape, lambda *_: (0,) * nd, pipeline_mode=pl.Buffered(1))


def _params(semantics):
    return pltpu.CompilerParams(dimension_semantics=semantics, vmem_limit_bytes=VMEM_LIMIT)


def _ada_kernel(c_ref, w_ref, b_ref, o_ref):
    c = c_ref[...]
    a = c * jax.nn.sigmoid(c)
    a_hi, a_lo = _split_bf16(a)
    w_hi, w_lo = _split_bf16(w_ref[...])
    o_ref[...] = _dot(a_hi, w_hi) + _dot(a_lo, w_hi) + _dot(a_hi, w_lo) + b_ref[...]


def _ada(c, w_ada, b_ada):
    rows, d = c.shape
    n = w_ada.shape[1]
    tn = 1024
    return pl.pallas_call(
        _ada_kernel,
        grid=(n // tn,),
        in_specs=[pl.BlockSpec((rows, d), lambda j: (0, 0)),
                  pl.BlockSpec((d, tn), lambda j: (0, j)),
                  pl.BlockSpec((1, tn), lambda j: (0, j))],
        out_specs=pl.BlockSpec((rows, tn), lambda j: (0, j)),
        out_shape=jax.ShapeDtypeStruct((rows, n), F32),
        compiler_params=_params(("parallel",)),
        name="ada",
    )(c, w_ada, b_ada.reshape(1, n))


def _log_sigmoid(g):
    return jnp.minimum(g, 0.0) - jnp.log1p(jnp.exp(-jnp.abs(g)))


def _proj_kernel(x_ref, ada_ref, ct_ref, s1_ref, s2_ref,
                 wq_ref, wk_ref, wvt_ref, wmq_ref, wmk_ref, wmv_ref, wmo_ref,
                 wgc_ref, wgr_ref, bgc_ref, bgr_ref, wbg_ref,
                 q_ref, k_ref, vt_ref, mq_ref, mk_ref, mv_ref, og_ref, gc_ref, gr_ref, gab_ref):
    ts = x_ref.shape[1]
    ada = ada_ref[0]
    sh1 = ada[:, 0:D_MODEL]
    sc1 = ada[:, D_MODEL:2 * D_MODEL]
    h = (_norm(x_ref[0]) * (1.0 + sc1) + sh1).astype(BF16)

    ct = ct_ref[...]
    s1 = s1_ref[...]
    s2 = s2_ref[...]

    def rope_store(w_ref, o_ref, scale):
        a = _dot(h, w_ref[...])
        for j in range(W_AQ // LANES):
            blk = a[:, j * LANES:(j + 1) * LANES]
            r = (blk * ct + pltpu.roll(blk, LANES - ROT_DIM // 2, 1) * s1
                 + pltpu.roll(blk, ROT_DIM // 2, 1) * s2)
            o_ref[0, :, j * LANES:(j + 1) * LANES] = (r * scale).astype(o_ref.dtype)

    rope_store(wq_ref, q_ref, (DQK_A ** -0.5) * LOG2E)
    rope_store(wk_ref, k_ref, 1.0)
    vt_ref[0] = _dot_nt(wvt_ref[...], h).astype(vt_ref.dtype)
    mq_ref[0] = _dot(h, wmq_ref[...]).astype(mq_ref.dtype)
    mk_ref[0] = (_dot(h, wmk_ref[...]) * (DQK_M ** -0.5)).astype(mk_ref.dtype)
    mv_ref[0] = _dot(h, wmv_ref[...]).astype(mv_ref.dtype)
    og_ref[0] = jax.nn.sigmoid(_dot(h, wmo_ref[...])).astype(og_ref.dtype)
    gab_ref[0] = jax.nn.sigmoid(_dot(h, wbg_ref[...])).astype(gab_ref.dtype)

    r = lax.broadcasted_iota(jnp.int32, (ts, ts), 0)
    c = lax.broadcasted_iota(jnp.int32, (ts, ts), 1)
    shift = CHUNK.bit_length() - 1
    same = jnp.right_shift(r, shift) == jnp.right_shift(c, shift)
    lower = jnp.where(same & (c <= r), 1.0, 0.0).astype(BF16)
    upper = jnp.where(same & (c >= r), 1.0, 0.0).astype(BF16)

    def gate_parts(g, ch):
        is_f = (ch & (2 * N_HEADS_M - 1)) >= N_HEADS_M
        val = jnp.where(is_f, _log_sigmoid(g), g)
        vf = jnp.where(is_f & (ch < 8), val, 0.0)
        vb = jnp.where(is_f & (ch >= 8) & (ch < W_MG), val, 0.0)
        vi = jnp.where(is_f, 0.0, val)
        return _split_bf16(vf), _split_bf16(vb), vi

    g = _dot(h, wgc_ref[...]) + bgc_ref[...]
    (f_hi, f_lo), (b_hi, b_lo), vi = gate_parts(g, lax.broadcasted_iota(jnp.int32, g.shape, 1))
    gcol = _dot(lower, f_hi) + _dot(lower, f_lo) + _dot(upper, b_hi) + _dot(upper, b_lo) + vi
    gc_ref[0] = gcol[:, 0:W_MG]

    g = _dot_nt(wgr_ref[...], h) + bgr_ref[...]
    (f_hi, f_lo), (b_hi, b_lo), vi = gate_parts(g, lax.broadcasted_iota(jnp.int32, g.shape, 0))
    gr_ref[0] = _dot(f_hi, upper) + _dot(f_lo, upper) + _dot(b_hi, lower) + _dot(b_lo, lower) + vi


def _proj(x, ada3, rope, wts, b_mgate, ts):
    B, S, D = x.shape
    ct, s1, s2 = rope
    wq, wk, wvt, wmq, wmk, wmv, wmo, wgc, wgr, wbg = wts
    bgc = jnp.pad(b_mgate.reshape(1, W_MG).astype(F32), ((0, 0), (0, LANES - W_MG)))
    bgr = b_mgate.reshape(W_MG, 1).astype(F32)
    tok = lambda n: pl.BlockSpec((1, ts, n), lambda b, i: (b, i, 0))
    tab = pl.BlockSpec((ts, LANES), lambda b, i: (i, 0))
    consts = (wq, wk, wvt, wmq, wmk, wmv, wmo, wgc, wgr, bgc, bgr, wbg)
    out_shape = (
        jax.ShapeDtypeStruct((B, S, W_AQ), BF16),
        jax.ShapeDtypeStruct((B, S, W_AK), BF16),
        jax.ShapeDtypeStruct((B, W_AV, S), BF16),
        jax.ShapeDtypeStruct((B, S, W_MQ), BF16),
        jax.ShapeDtypeStruct((B, S, W_MK), BF16),
        jax.ShapeDtypeStruct((B, S, W_MV), BF16),
        jax.ShapeDtypeStruct((B, S, W_MO), BF16),
        jax.ShapeDtypeStruct((B, S, W_MG), F32),
        jax.ShapeDtypeStruct((B, W_MG, S), F32),
        jax.ShapeDtypeStruct((B, S, W_BG), BF16),
    )
    out_specs = (
        tok(W_AQ), tok(W_AK),
        pl.BlockSpec((1, W_AV, ts), lambda b, i: (b, 0, i)),
        tok(W_MQ), tok(W_MK), tok(W_MV), tok(W_MO), tok(W_MG),
        pl.BlockSpec((1, W_MG, ts), lambda b, i: (b, 0, i)),
        tok(W_BG),
    )
    return pl.pallas_call(
        _proj_kernel,
        grid=(B, S // ts),
        in_specs=[tok(D), pl.BlockSpec((1, 1, 6 * D), lambda b, i: (b, 0, 0)), tab, tab, tab]
                 + [_const_spec(w.shape) for w in consts],
        out_specs=out_specs,
        out_shape=out_shape,
        compiler_params=_params(("parallel", "parallel")),
        name="proj",
    )(x, ada3, ct, s1, s2, *consts)


ONES_ROWS = 16


def _attn_kernel(lam_ref, q_ref, k_ref, vt_ref, g_ref, o_ref,
                 s_ref, p_ref, acc, m_ref, mx_ref, alpha_ref, *, tk, sub, lam_init):
    n = k_ref.shape[1] // tk
    nsub = tk // sub
    q = q_ref[0]
    lane = lax.broadcasted_iota(jnp.int32, q.shape, 1)
    zero = jnp.zeros_like(q)
    qs = (jnp.where(lane < DQK_A, q, zero), jnp.where(lane >= DQK_A, q, zero))
    ones_rows = jnp.ones((ONES_ROWS, sub), BF16)
    maps = range(2)

    acc[...] = jnp.zeros_like(acc)
    m_ref[...] = jnp.full_like(m_ref, -jnp.inf)

    def scores_sub(j, r, mx):
        kc = k_ref[0, pl.ds(pl.multiple_of(j * tk + r * sub, sub), sub), :]
        out = []
        for mp in maps:
            s = _dot_nt(kc, qs[mp])
            s_ref[mp, r * sub:(r + 1) * sub, :] = s
            cm = jnp.max(s, axis=0, keepdims=True)
            out.append(cm if mx is None else jnp.maximum(mx[mp], cm))
        return out

    def softmax_sub(r, m_new):
        for mp in maps:
            s = s_ref[mp, r * sub:(r + 1) * sub, :]
            p_ref[mp, r * sub:(r + 1) * sub, :] = jnp.exp2(s - m_new[mp]).astype(BF16)

    def values_sub(j, r, pv):
        vc = vt_ref[0, :, pl.ds(pl.multiple_of(j * tk + r * sub, sub), sub)]
        va = jnp.concatenate([vc, ones_rows], axis=0)
        out = []
        for mp in maps:
            d = _dot(va, p_ref[mp, r * sub:(r + 1) * sub, :])
            out.append(d if pv is None else pv[mp] + d)
        return out

    def accumulate(pv):
        for mp in maps:
            acc[mp] = alpha_ref[mp] * acc[mp] + pv[mp]

    def trip(j, first, last):
        m_old = [m_ref[mp] for mp in maps]
        m_new = [jnp.maximum(m_old[mp], mx_ref[mp]) for mp in maps]
        pv = mx = None
        for r in range(nsub):
            if not first:
                pv = values_sub(j - 1, r, pv)
            softmax_sub(r, m_new)
            if not last:
                mx = scores_sub(j + 1, r, mx)
        if not first:
            accumulate(pv)
        for mp in maps:
            alpha_ref[mp] = jnp.exp2(m_old[mp] - m_new[mp])
            m_ref[mp] = m_new[mp]
            if not last:
                mx_ref[mp] = mx[mp]

    mx = None
    for r in range(nsub):
        mx = scores_sub(0, r, mx)
    for mp in maps:
        mx_ref[mp] = mx[mp]
    trip(0, True, False)

    def body(j, carry):
        trip(j, False, False)
        return carry

    lax.fori_loop(1, n - 1, body, 0)
    trip(n - 1, False, True)
    pv = None
    for r in range(nsub):
        pv = values_sub(n - 1, r, pv)
    accumulate(pv)

    lq = lam_ref[...]
    la = jnp.sum(lq[0:1] * lq[1:2], axis=1, keepdims=True)
    lb = jnp.sum(lq[2:3] * lq[3:4], axis=1, keepdims=True)
    lam = jnp.exp(la) - jnp.exp(lb) + lam_init
    a1 = acc[0]
    a2 = acc[1]
    o = a1[0:DV_A] / a1[DV_A:DV_A + 1] - lam * (a2[0:DV_A] / a2[DV_A:DV_A + 1])
    ms = jnp.mean(o * o, axis=0, keepdims=True)
    y = o * lax.rsqrt(ms + LN_EPS) * g_ref[...] * (1.0 - lam_init)
    o_ref[0] = y.T.astype(o_ref.dtype)


def _attn(q, k, vt, lambda_qk, subln_g, lam_init, tq, tk, sub):
    B, S, _ = q.shape
    assert S // tk >= 2 and tk % sub == 0, (S, tk, sub)
    kern = functools.partial(_attn_kernel, tk=tk, sub=sub, lam_init=lam_init)
    return pl.pallas_call(
        kern,
        grid=(B, N_HEADS_A, S // tq),
        in_specs=[pl.BlockSpec((4, DQK_A), lambda b, h, i: (0, 0)),
                  pl.BlockSpec((1, tq, DV_A), lambda b, h, i: (b, i, h)),
                  pl.BlockSpec((1, S, DV_A), lambda b, h, i: (b, 0, h)),
                  pl.BlockSpec((1, DV_A, S), lambda b, h, i: (b, h, 0)),
                  pl.BlockSpec((DV_A, 1), lambda b, h, i: (0, 0))],
        out_specs=pl.BlockSpec((1, tq, DV_A), lambda b, h, i: (b, i, h)),
        out_shape=jax.ShapeDtypeStruct((B, S, N_HEADS_A * DV_A), BF16),
        scratch_shapes=[pltpu.VMEM((2, tk, tq), F32), pltpu.VMEM((2, tk, tq), BF16),
                        pltpu.VMEM((2, DV_A + ONES_ROWS, tq), F32),
                        pltpu.VMEM((2, 1, tq), F32), pltpu.VMEM((2, 1, tq), F32),
                        pltpu.VMEM((2, 1, tq), F32)],
        compiler_params=_params(("parallel", "parallel", "parallel")),
        name="attn",
    )(lambda_qk.astype(F32), q, k, vt, subln_g.reshape(DV_A, 1).astype(F32))


N_STREAMS = 2 * N_HEADS_M
DV_AUG = DV_M + LANES


def _mlstm_kernel(qf_ref, kf_ref, vf_ref, gcf_ref, grf_ref,
                  qb_ref, kb_ref, vb_ref, gcb_ref, grb_ref,
                  hf_ref, hb_ref, ct_ref, m_ref):
    L = CHUNK

    @pl.when(pl.program_id(1) == 0)
    def _():
        ct_ref[...] = jnp.zeros_like(ct_ref)
        m_ref[...] = jnp.zeros_like(m_ref)

    row = lax.broadcasted_iota(jnp.int32, (L, L), 0)
    col = lax.broadcasted_iota(jnp.int32, (L, L), 1)
    assert L == LANES
    ones_blk = jnp.ones((L, LANES), BF16)

    def lane_tile(a, width):
        return jnp.concatenate([a] * (width // LANES), axis=1)

    dirs = ((qf_ref, kf_ref, vf_ref, gcf_ref, grf_ref, hf_ref),
            (qb_ref, kb_ref, vb_ref, gcb_ref, grb_ref, hb_ref))
    m_all = m_ref[...]
    m_out = []
    nb = qf_ref.shape[0]
    for bb, (d, (q_ref, k_ref, v_ref, gc_ref, gr_ref, h_ref)) in (
            (bb, dd) for bb in range(nb) for dd in enumerate(dirs)):
        tri = (col <= row) if d == 0 else (col >= row)
        e = L - 1 if d == 0 else 0
        gc = gc_ref[bb]
        gr = gr_ref[bb]
        for hd in range(N_HEADS_M):
            st = (bb * 2 + d) * N_HEADS_M + hd
            ci = d * 2 * N_HEADS_M + hd
            cf = ci + N_HEADS_M
            qh = q_ref[bb, :, hd * DQK_M:(hd + 1) * DQK_M]
            kh = k_ref[bb, :, hd * DQK_M:(hd + 1) * DQK_M]
            va = jnp.concatenate([v_ref[bb, :, hd * DV_M:(hd + 1) * DV_M], ones_blk], axis=1)
            i_row = gr[ci:ci + 1, :]
            b_row = gr[cf:cf + 1, :]
            b_t = jnp.broadcast_to(gc[:, cf:cf + 1], (L, LANES))
            m_prev = m_all[st:st + 1, :]
            dm = jnp.where(tri, b_t - b_row + i_row, -jnp.inf)
            inter = b_t + m_prev
            m_t = jnp.maximum(inter, jnp.broadcast_to(jnp.max(dm, axis=1, keepdims=True), (L, LANES)))
            w = jnp.exp(dm - m_t)
            w_inter = jnp.exp(inter - m_t)
            sqk = (_dot_nt(qh, kh) * w).astype(BF16)
            ct = ct_ref[st]
            numa = _dot(sqk, va) + lane_tile(w_inter, DV_AUG) * _dot(qh, ct.astype(BF16))
            den = numa[:, DV_M:DV_AUG]
            inv = 1.0 / jnp.maximum(jnp.abs(den), jnp.exp(-m_t))
            h = numa[:, 0:DV_M] * lane_tile(inv, DV_M)
            h_ref[bb, :, hd * DV_M:(hd + 1) * DV_M] = h.astype(h_ref.dtype)
            b_end = b_row[:, e:e + 1]
            m_new = m_t[e:e + 1, :]
            ws = jnp.exp(b_end - b_row + i_row - m_new)
            decay = jnp.exp(b_end + m_prev - m_new)
            kst = (kh.astype(F32).T * ws).astype(BF16)
            ct_ref[st] = lane_tile(decay, DV_AUG) * ct + _dot(kst, va)
            m_out.append(m_new)
    m_ref[...] = jnp.concatenate(m_out, axis=0)


def _mlstm(mq, mk, mv, gcol, grow):
    B, S, _ = mq.shape
    nc = S // CHUNK
    nb = 2 if B % 2 == 0 else 1
    fwd = lambda n: pl.BlockSpec((nb, CHUNK, n), lambda b, c: (b, c, 0))
    bwd = lambda n: pl.BlockSpec((nb, CHUNK, n), lambda b, c: (b, nc - 1 - c, 0))
    rowf = pl.BlockSpec((nb, W_MG, CHUNK), lambda b, c: (b, 0, c))
    rowb = pl.BlockSpec((nb, W_MG, CHUNK), lambda b, c: (b, 0, nc - 1 - c))
    return pl.pallas_call(
        _mlstm_kernel,
        grid=(B // nb, nc),
        in_specs=[fwd(W_MQ), fwd(W_MK), fwd(W_MV), fwd(W_MG), rowf,
                  bwd(W_MQ), bwd(W_MK), bwd(W_MV), bwd(W_MG), rowb],
        out_specs=(fwd(W_MV), bwd(W_MV)),
        out_shape=(jax.ShapeDtypeStruct((B, S, W_MV), BF16),
                   jax.ShapeDtypeStruct((B, S, W_MV), BF16)),
        scratch_shapes=[pltpu.VMEM((nb * N_STREAMS, DQK_M, DV_AUG), F32),
                        pltpu.VMEM((nb * N_STREAMS, LANES), F32)],
        compiler_params=_params(("parallel", "arbitrary")),
        name="mlstm",
    )(mq, mk, mv, gcol, grow, mq, mk, mv, gcol, grow)


def _mix_kernel(x_ref, ada_ref, ya_ref, hf_ref, hb_ref, og_ref, gab_ref,
                wpa_ref, wpm_ref, wout_ref, mng_ref, lg_ref, lb_ref, o_ref, *, alpha):
    hs = hf_ref[0].astype(F32) + hb_ref[0].astype(F32)
    hn = jnp.concatenate(
        [_norm(hs[:, hd * DV_M:(hd + 1) * DV_M]) for hd in range(N_HEADS_M)], axis=1)
    ym = (og_ref[0].astype(F32) * (hn * mng_ref[...])).astype(BF16)
    gab = gab_ref[0].astype(F32)
    pre = gab[:, 0:D_MODEL] * _dot(ya_ref[0], wpa_ref[...]) + gab[:, D_MODEL:] * _dot(ym, wpm_ref[...])
    mix = _dot(pre.astype(BF16), wout_ref[...])
    g1 = ada_ref[0][:, 2 * D_MODEL:3 * D_MODEL]
    o_ref[0] = _norm(alpha * x_ref[0] + g1 * mix) * lg_ref[...] + lb_ref[...]


def _mix(x, ada3, ya, hf, hb, og, gab, wpa, wpm, wout, mnorm_g, ln_g, ln_b, alpha, ts):
    B, S, D = x.shape
    tok = lambda n: pl.BlockSpec((1, ts, n), lambda b, i: (b, i, 0))
    consts = (wpa, wpm, wout, mnorm_g.reshape(1, W_MV).astype(F32),
              ln_g.reshape(1, D).astype(F32), ln_b.reshape(1, D).astype(F32))
    return pl.pallas_call(
        functools.partial(_mix_kernel, alpha=alpha),
        grid=(B, S // ts),
        in_specs=[tok(D), pl.BlockSpec((1, 1, 6 * D), lambda b, i: (b, 0, 0)),
                  tok(W_AV), tok(W_MV), tok(W_MV), tok(W_MO), tok(W_BG)]
                 + [_const_spec(w.shape) for w in consts],
        out_specs=tok(D),
        out_shape=jax.ShapeDtypeStruct((B, S, D), F32),
        compiler_params=_params(("parallel", "parallel")),
        name="mix",
    )(x, ada3, ya, hf, hb, og, gab, *consts)


def _gelu_tanh(x):
    return 0.5 * x * (1.0 + jnp.tanh(math.sqrt(2.0 / math.pi) * (x + 0.044715 * (x * x * x))))


def _ffn_kernel(xp_ref, x_ref, xn_ref, ada_ref, wug_ref, wuv_ref, cw_ref, cb_ref, wd_ref,
                lg_ref, lb_ref, o_ref, *, alpha):
    ts = x_ref.shape[1]
    H = SUBLANES
    i = pl.program_id(1)
    ada = ada_ref[0]
    sh2 = ada[:, 3 * D_MODEL:4 * D_MODEL]
    sc2 = ada[:, 4 * D_MODEL:5 * D_MODEL]
    g2 = ada[:, 5 * D_MODEL:6 * D_MODEL]
    x = x_ref[0]
    xx = jnp.concatenate([xp_ref[0], x, xn_ref[0]], axis=0)
    hx = _norm(xx) * (1.0 + sc2) + sh2
    ug = _dot(hx.astype(BF16), wug_ref[...])
    uv = _dot(hx[H:H + ts].astype(BF16), wuv_ref[...])
    n = ts + 2 * H
    prev = pltpu.roll(ug, 1, 0)[H:H + ts]
    nxt = pltpu.roll(ug, n - 1, 0)[H:H + ts]
    t = lax.broadcasted_iota(jnp.int32, (ts, 1), 0)
    keep_prev = jnp.where((i == 0) & (t == 0), 0.0, 1.0)
    keep_next = jnp.where((i == pl.num_programs(1) - 1) & (t == ts - 1), 0.0, 1.0)
    cw = cw_ref[...]
    conv = (prev * keep_prev * cw[0:1] + ug[H:H + ts] * cw[1:2] + nxt * keep_next * cw[2:3]
            + cb_ref[...])
    y = _dot((_gelu_tanh(conv) * uv).astype(BF16), wd_ref[...])
    o_ref[0] = _norm(alpha * x + g2 * y) * lg_ref[...] + lb_ref[...]


def _ffn(x, ada3, wug, wuv, conv_w, conv_b, wd, ln_g, ln_b, alpha, ts):
    B, S, D = x.shape
    H = SUBLANES
    nb = ts // H
    last = S // H - 1
    tok = pl.BlockSpec((1, ts, D), lambda b, i: (b, i, 0))
    prev = pl.BlockSpec((1, H, D), lambda b, i: (b, jnp.maximum(i * nb - 1, 0), 0))
    nxt = pl.BlockSpec((1, H, D), lambda b, i: (b, jnp.minimum((i + 1) * nb, last), 0))
    consts = (wug, wuv, conv_w.astype(F32), conv_b.reshape(1, D_FF).astype(F32), wd,
              ln_g.reshape(1, D).astype(F32), ln_b.reshape(1, D).astype(F32))
    return pl.pallas_call(
        functools.partial(_ffn_kernel, alpha=alpha),
        grid=(B, S // ts),
        in_specs=[prev, tok, nxt, pl.BlockSpec((1, 1, 6 * D), lambda b, i: (b, 0, 0))]
                 + [_const_spec(w.shape) for w in consts],
        out_specs=tok,
        out_shape=jax.ShapeDtypeStruct((B, S, D), F32),
        compiler_params=_params(("parallel", "parallel")),
        name="ffn",
    )(x, x, x, ada3, *consts)


def _rope_tables(seq):
    half = ROT_DIM // 2
    inv = ROPE_THETA ** (-jnp.arange(0, ROT_DIM, 2, dtype=F32) / ROT_DIM)
    ang = jnp.arange(seq, dtype=F32)[:, None] * inv[None, :]
    cos, sin = jnp.cos(ang), jnp.sin(ang)
    rest = DQK_A - ROT_DIM
    one = jnp.ones((seq, rest), F32)
    zr = jnp.zeros((seq, rest), F32)
    zh = jnp.zeros((seq, half), F32)
    reps = LANES // DQK_A
    ct = jnp.tile(jnp.concatenate([cos, cos, one], axis=1), (1, reps))
    s1 = jnp.tile(jnp.concatenate([-sin, zh, zr], axis=1), (1, reps))
    s2 = jnp.tile(jnp.concatenate([zh, sin, zr], axis=1), (1, reps))
    return ct, s1, s2


def _layer_weights(w_in, w_pa, w_pm, w_out, w_up, w_down):
    seg = lambda j: w_in[:, SEG_STARTS[j]:SEG_STARTS[j] + SEG_WIDTHS[j]]
    wg = seg(7)
    proj = (seg(0).astype(BF16), seg(1).astype(BF16), seg(2).T.astype(BF16),
            seg(3).astype(BF16), seg(4).astype(BF16), seg(5).astype(BF16), seg(6).astype(BF16),
            jnp.pad(wg, ((0, 0), (0, LANES - W_MG))).astype(BF16), wg.T.astype(BF16),
            seg(8).astype(BF16))
    return (proj, w_pa.astype(BF16), w_pm.astype(BF16), w_out.astype(BF16),
            w_up[:, :D_FF].astype(BF16), w_up[:, D_FF:].astype(BF16), w_down.astype(BF16))


def _tile(seq, want):
    t = min(seq, want)
    assert seq % t == 0, (seq, want)
    return t


def _default_tiles(seq):
    return (512, 1024, min(2048, max(seq // 4, 256)), 256)


def _encoder_layer(x, ada, l, depth, weights, lambda_qk, subln_g, b_mgate, mnorm_g,
                   ln1_g, ln1_b, conv_w, conv_b, ln2_g, ln2_b, tiles):
    B, S, D = x.shape
    ts, tq, tk, sub = (_tile(S, t) for t in (tiles or _default_tiles(S)))
    alpha = (2.0 * depth) ** 0.25
    lam_init = 0.8 - 0.6 * math.exp(-0.3 * l)
    proj_w, wpa, wpm, wout, wug, wuv, wd = weights
    ada3 = ada.reshape(B, 1, 6 * D)
    q, k, vt, mq, mk, mv, og, gcol, grow, gab = _proj(x, ada3, _rope_tables(S), proj_w, b_mgate, ts)
    ya = _attn(q, k, vt, lambda_qk, subln_g, lam_init, tq, tk, sub)
    hf, hb = _mlstm(mq, mk, mv, gcol, grow)
    x1 = _mix(x, ada3, ya, hf, hb, og, gab, wpa, wpm, wout, mnorm_g, ln1_g, ln1_b, alpha, ts)
    return _ffn(x1, ada3, wug, wuv, conv_w, conv_b, wd, ln2_g, ln2_b, alpha, ts)


def _trunk(xs, cs, w_ada, b_ada, w_in, lambda_qk, subln_g, b_mgate, mnorm_g, w_pa, w_pm, w_out,
           ln1_g, ln1_b, w_up, conv_w, conv_b, w_down, ln2_g, ln2_b, tiles=None):
    depth = w_in.shape[0]
    rows = [c.shape[0] for c in cs]
    c_all = jnp.concatenate(cs, axis=0)
    xs = list(xs)
    for l in range(depth):
        ada_all = _ada(c_all, w_ada[l], b_ada[l])
        weights = _layer_weights(w_in[l], w_pa[l], w_pm[l], w_out[l], w_up[l], w_down[l])
        start = 0
        for g, n in enumerate(rows):
            xs[g] = _encoder_layer(xs[g], ada_all[start:start + n], l, depth, weights,
                                   lambda_qk[l], subln_g[l], b_mgate[l], mnorm_g[l],
                                   ln1_g[l], ln1_b[l], conv_w[l], conv_b[l], ln2_g[l], ln2_b[l], tiles)
            start += n
    return tuple(xs)


def kernel(x_prompt, x_sample, c_prompt, c_sample, w_ada, b_ada, w_in, lambda_qk, subln_g, b_mgate, mnorm_g, w_pa, w_pm, w_out, ln1_g, ln1_b, w_up, conv_w, conv_b, w_down, ln2_g, ln2_b):
    return _trunk((x_prompt, x_sample), (c_prompt, c_sample), w_ada, b_ada, w_in, lambda_qk, subln_g,
                  b_mgate, mnorm_g, w_pa, w_pm, w_out, ln1_g, ln1_b, w_up, conv_w, conv_b, w_down,
                  ln2_g, ln2_b)
```

```python
import functools
import math

import jax
import jax.numpy as jnp
import numpy as np
from jax import lax
from jax.experimental import pallas as pl
from jax.experimental.pallas import tpu as pltpu

F32 = jnp.float32
BF16 = jnp.bfloat16

D_MODEL = 1024
N_HEADS_A = 8
DQK_A = 64
DV_A = 2 * DQK_A
ROT_DIM = DQK_A // 4
ROPE_THETA = 500000.0
N_HEADS_M = 4
DQK_M = 128
DV_M = 256
CHUNK = 128
D_FF = 2816
LN_EPS = 1e-5

W_AQ = N_HEADS_A * 2 * DQK_A
W_AK = N_HEADS_A * 2 * DQK_A
W_AV = N_HEADS_A * DV_A
W_MQ = N_HEADS_M * DQK_M
W_MK = N_HEADS_M * DQK_M
W_MV = N_HEADS_M * DV_M
W_MO = N_HEADS_M * DV_M
W_MG = 2 * 2 * N_HEADS_M
W_BG = 2 * D_MODEL
SEG_WIDTHS = (W_AQ, W_AK, W_AV, W_MQ, W_MK, W_MV, W_MO, W_MG, W_BG)
SEG_STARTS = [0] + [int(s) for s in np.cumsum(SEG_WIDTHS)[:-1]]

LANES = 128
SUBLANES = 8
LOG2E = 1.4426950408889634
VMEM_LIMIT = 56 * 1024 * 1024

NT_DIMS = (((1,), (1,)), ((), ()))


def _dot(a, b):
    return jnp.dot(a, b, preferred_element_type=F32)


def _dot_nt(a, b):
    return lax.dot_general(a, b, NT_DIMS, preferred_element_type=F32)


def _norm(x):
    mu = jnp.mean(x, axis=-1, keepdims=True)
    xc = x - mu
    var = jnp.mean(xc * xc, axis=-1, keepdims=True)
    return xc * lax.rsqrt(var + LN_EPS)


def _split_bf16(a):
    hi = a.astype(BF16)
    lo = (a - hi.astype(F32)).astype(BF16)
    return hi, lo


def _const_spec(shape):
    nd = len(shape)
    return pl.BlockSpec(shape, lambda *_: (0,) * nd, pipeline_mode=pl.Buffered(1))


def _params(semantics):
    return pltpu.CompilerParams(dimension_semantics=semantics, vmem_limit_bytes=VMEM_LIMIT)


def _ada_kernel(c_ref, w_ref, b_ref, o_ref):
    c = c_ref[...]
    a = c * jax.nn.sigmoid(c)
    a_hi, a_lo = _split_bf16(a)
    w_hi, w_lo = _split_bf16(w_ref[...])
    o_ref[...] = _dot(a_hi, w_hi) + _dot(a_lo, w_hi) + _dot(a_hi, w_lo) + b_ref[...]


def _ada(c, w_ada, b_ada):
    rows, d = c.shape
    n = w_ada.shape[1]
    tn = 1024
    return pl.pallas_call(
        _ada_kernel,
        grid=(n // tn,),
        in_specs=[pl.BlockSpec((rows, d), lambda j: (0, 0)),
                  pl.BlockSpec((d, tn), lambda j: (0, j)),
                  pl.BlockSpec((1, tn), lambda j: (0, j))],
        out_specs=pl.BlockSpec((rows, tn), lambda j: (0, j)),
        out_shape=jax.ShapeDtypeStruct((rows, n), F32),
        compiler_params=_params(("parallel",)),
        name="ada",
    )(c, w_ada, b_ada.reshape(1, n))


def _log_sigmoid(g):
    return jnp.minimum(g, 0.0) - jnp.log1p(jnp.exp(-jnp.abs(g)))


def _proj_kernel(x_ref, ada_ref, ct_ref, s1_ref, s2_ref,
                 wq_ref, wk_ref, wvt_ref, wmq_ref, wmk_ref, wmv_ref, wmo_ref,
                 wgc_ref, wgr_ref, bgc_ref, bgr_ref, wbg_ref,
                 q_ref, k_ref, vt_ref, mq_ref, mk_ref, mv_ref, og_ref, gc_ref, gr_ref, gab_ref):
    ts = x_ref.shape[1]
    ada = ada_ref[0]
    sh1 = ada[:, 0:D_MODEL]
    sc1 = ada[:, D_MODEL:2 * D_MODEL]
    h = (_norm(x_ref[0]) * (1.0 + sc1) + sh1).astype(BF16)

    ct = ct_ref[...]
    s1 = s1_ref[...]
    s2 = s2_ref[...]

    def rope_store(w_ref, o_ref, scale):
        a = _dot(h, w_ref[...])
        for j in range(W_AQ // LANES):
            blk = a[:, j * LANES:(j + 1) * LANES]
            r = (blk * ct + pltpu.roll(blk, LANES - ROT_DIM // 2, 1) * s1
                 + pltpu.roll(blk, ROT_DIM // 2, 1) * s2)
            o_ref[0, :, j * LANES:(j + 1) * LANES] = (r * scale).astype(o_ref.dtype)

    rope_store(wq_ref, q_ref, (DQK_A ** -0.5) * LOG2E)
    rope_store(wk_ref, k_ref, 1.0)
    vt_ref[0] = _dot_nt(wvt_ref[...], h).astype(vt_ref.dtype)
    mq_ref[0] = _dot(h, wmq_ref[...]).astype(mq_ref.dtype)
    mk_ref[0] = (_dot(h, wmk_ref[...]) * (DQK_M ** -0.5)).astype(mk_ref.dtype)
    mv_ref[0] = _dot(h, wmv_ref[...]).astype(mv_ref.dtype)
    og_ref[0] = jax.nn.sigmoid(_dot(h, wmo_ref[...])).astype(og_ref.dtype)
    gab_ref[0] = jax.nn.sigmoid(_dot(h, wbg_ref[...])).astype(gab_ref.dtype)

    r = lax.broadcasted_iota(jnp.int32, (ts, ts), 0)
    c = lax.broadcasted_iota(jnp.int32, (ts, ts), 1)
    shift = CHUNK.bit_length() - 1
    same = jnp.right_shift(r, shift) == jnp.right_shift(c, shift)
    lower = jnp.where(same & (c <= r), 1.0, 0.0).astype(BF16)
    upper = jnp.where(same & (c >= r), 1.0, 0.0).astype(BF16)

    def gate_parts(g, ch):
        is_f = (ch & (2 * N_HEADS_M - 1)) >= N_HEADS_M
        val = jnp.where(is_f, _log_sigmoid(g), g)
        vf = jnp.where(is_f & (ch < 8), val, 0.0)
        vb = jnp.where(is_f & (ch >= 8) & (ch < W_MG), val, 0.0)
        vi = jnp.where(is_f, 0.0, val)
        return _split_bf16(vf), _split_bf16(vb), vi

    g = _dot(h, wgc_ref[...]) + bgc_ref[...]
    (f_hi, f_lo), (b_hi, b_lo), vi = gate_parts(g, lax.broadcasted_iota(jnp.int32, g.shape, 1))
    gcol = _dot(lower, f_hi) + _dot(lower, f_lo) + _dot(upper, b_hi) + _dot(upper, b_lo) + vi
    gc_ref[0] = gcol[:, 0:W_MG]

    g = _dot_nt(wgr_ref[...], h) + bgr_ref[...]
    (f_hi, f_lo), (b_hi, b_lo), vi = gate_parts(g, lax.broadcasted_iota(jnp.int32, g.shape, 0))
    gr_ref[0] = _dot(f_hi, upper) + _dot(f_lo, upper) + _dot(b_hi, lower) + _dot(b_lo, lower) + vi


def _proj(x, ada3, rope, wts, b_mgate, ts):
    B, S, D = x.shape
    ct, s1, s2 = rope
    wq, wk, wvt, wmq, wmk, wmv, wmo, wgc, wgr, wbg = wts
    bgc = jnp.pad(b_mgate.reshape(1, W_MG).astype(F32), ((0, 0), (0, LANES - W_MG)))
    bgr = b_mgate.reshape(W_MG, 1).astype(F32)
    tok = lambda n: pl.BlockSpec((1, ts, n), lambda b, i: (b, i, 0))
    tab = pl.BlockSpec((ts, LANES), lambda b, i: (i, 0))
    consts = (wq, wk, wvt, wmq, wmk, wmv, wmo, wgc, wgr, bgc, bgr, wbg)
    out_shape = (
        jax.ShapeDtypeStruct((B, S, W_AQ), BF16),
        jax.ShapeDtypeStruct((B, S, W_AK), BF16),
        jax.ShapeDtypeStruct((B, W_AV, S), BF16),
        jax.ShapeDtypeStruct((B, S, W_MQ), BF16),
        jax.ShapeDtypeStruct((B, S, W_MK), BF16),
        jax.ShapeDtypeStruct((B, S, W_MV), BF16),
        jax.ShapeDtypeStruct((B, S, W_MO), BF16),
        jax.ShapeDtypeStruct((B, S, W_MG), F32),
        jax.ShapeDtypeStruct((B, W_MG, S), F32),
        jax.ShapeDtypeStruct((B, S, W_BG), BF16),
    )
    out_specs = (
        tok(W_AQ), tok(W_AK),
        pl.BlockSpec((1, W_AV, ts), lambda b, i: (b, 0, i)),
        tok(W_MQ), tok(W_MK), tok(W_MV), tok(W_MO), tok(W_MG),
        pl.BlockSpec((1, W_MG, ts), lambda b, i: (b, 0, i)),
        tok(W_BG),
    )
    return pl.pallas_call(
        _proj_kernel,
        grid=(B, S // ts),
        in_specs=[tok(D), pl.BlockSpec((1, 1, 6 * D), lambda b, i: (b, 0, 0)), tab, tab, tab]
                 + [_const_spec(w.shape) for w in consts],
        out_specs=out_specs,
        out_shape=out_shape,
        compiler_params=_params(("parallel", "parallel")),
        name="proj",
    )(x, ada3, ct, s1, s2, *consts)


ONES_ROWS = 16


def _attn_kernel(lam_ref, q_ref, k_ref, vt_ref, g_ref, o_ref,
                 s_ref, p_ref, acc, m_ref, mx_ref, alpha_ref, *, tk, sub, lam_init):
    n = k_ref.shape[1] // tk
    nsub = tk // sub
    q = q_ref[0]
    lane = lax.broadcasted_iota(jnp.int32, q.shape, 1)
    zero = jnp.zeros_like(q)
    qs = (jnp.where(lane < DQK_A, q, zero), jnp.where(lane >= DQK_A, q, zero))
    ones_rows = jnp.ones((ONES_ROWS, sub), BF16)
    maps = range(2)

    acc[...] = jnp.zeros_like(acc)
    m_ref[...] = jnp.full_like(m_ref, -jnp.inf)

    def scores_sub(j, r, mx):
        kc = k_ref[0, pl.ds(pl.multiple_of(j * tk + r * sub, sub), sub), :]
        out = []
        for mp in maps:
            s = _dot_nt(kc, qs[mp])
            s_ref[mp, r * sub:(r + 1) * sub, :] = s
            cm = jnp.max(s, axis=0, keepdims=True)
            out.append(cm if mx is None else jnp.maximum(mx[mp], cm))
        return out

    def softmax_sub(r, m_new):
        for mp in maps:
            s = s_ref[mp, r * sub:(r + 1) * sub, :]
            p_ref[mp, r * sub:(r + 1) * sub, :] = jnp.exp2(s - m_new[mp]).astype(BF16)

    def values_sub(j, r, pv):
        vc = vt_ref[0, :, pl.ds(pl.multiple_of(j * tk + r * sub, sub), sub)]
        va = jnp.concatenate([vc, ones_rows], axis=0)
        out = []
        for mp in maps:
            d = _dot(va, p_ref[mp, r * sub:(r + 1) * sub, :])
            out.append(d if pv is None else pv[mp] + d)
        return out

    def accumulate(pv):
        for mp in maps:
            acc[mp] = alpha_ref[mp] * acc[mp] + pv[mp]

    def trip(j, first, last):
        m_old = [m_ref[mp] for mp in maps]
        m_new = [jnp.maximum(m_old[mp], mx_ref[mp]) for mp in maps]
        pv = mx = None
        for r in range(nsub):
            if not first:
                pv = values_sub(j - 1, r, pv)
            softmax_sub(r, m_new)
            if not last:
                mx = scores_sub(j + 1, r, mx)
        if not first:
            accumulate(pv)
        for mp in maps:
            alpha_ref[mp] = jnp.exp2(m_old[mp] - m_new[mp])
            m_ref[mp] = m_new[mp]
            if not last:
                mx_ref[mp] = mx[mp]

    mx = None
    for r in range(nsub):
        mx = scores_sub(0, r, mx)
    for mp in maps:
        mx_ref[mp] = mx[mp]
    trip(0, True, False)

    def body(j, carry):
        trip(j, False, False)
        return carry

    lax.fori_loop(1, n - 1, body, 0)
    trip(n - 1, False, True)
    pv = None
    for r in range(nsub):
        pv = values_sub(n - 1, r, pv)
    accumulate(pv)

    lq = lam_ref[...]
    la = jnp.sum(lq[0:1] * lq[1:2], axis=1, keepdims=True)
    lb = jnp.sum(lq[2:3] * lq[3:4], axis=1, keepdims=True)
    lam = jnp.exp(la) - jnp.exp(lb) + lam_init
    a1 = acc[0]
    a2 = acc[1]
    o = a1[0:DV_A] / a1[DV_A:DV_A + 1] - lam * (a2[0:DV_A] / a2[DV_A:DV_A + 1])
    ms = jnp.mean(o * o, axis=0, keepdims=True)
    y = o * lax.rsqrt(ms + LN_EPS) * g_ref[...] * (1.0 - lam_init)
    o_ref[0] = y.T.astype(o_ref.dtype)


def _attn(q, k, vt, lambda_qk, subln_g, lam_init, tq, tk, sub):
    B, S, _ = q.shape
    assert S // tk >= 2 and tk % sub == 0, (S, tk, sub)
    kern = functools.partial(_attn_kernel, tk=tk, sub=sub, lam_init=lam_init)
    return pl.pallas_call(
        kern,
        grid=(B, N_HEADS_A, S // tq),
        in_specs=[pl.BlockSpec((4, DQK_A), lambda b, h, i: (0, 0)),
                  pl.BlockSpec((1, tq, DV_A), lambda b, h, i: (b, i, h)),
                  pl.BlockSpec((1, S, DV_A), lambda b, h, i: (b, 0, h)),
                  pl.BlockSpec((1, DV_A, S), lambda b, h, i: (b, h, 0)),
                  pl.BlockSpec((DV_A, 1), lambda b, h, i: (0, 0))],
        out_specs=pl.BlockSpec((1, tq, DV_A), lambda b, h, i: (b, i, h)),
        out_shape=jax.ShapeDtypeStruct((B, S, N_HEADS_A * DV_A), BF16),
        scratch_shapes=[pltpu.VMEM((2, tk, tq), F32), pltpu.VMEM((2, tk, tq), BF16),
                        pltpu.VMEM((2, DV_A + ONES_ROWS, tq), F32),
                        pltpu.VMEM((2, 1, tq), F32), pltpu.VMEM((2, 1, tq), F32),
                        pltpu.VMEM((2, 1, tq), F32)],
        compiler_params=_params(("parallel", "parallel", "parallel")),
        name="attn",
    )(lambda_qk.astype(F32), q, k, vt, subln_g.reshape(DV_A, 1).astype(F32))


N_STREAMS = 2 * N_HEADS_M
MLSTM_GROUP = 4
DV_AUG = DV_M + LANES


def _mlstm_kernel(qf_ref, kf_ref, vf_ref, gcf_ref, grf_ref,
                  qb_ref, kb_ref, vb_ref, gcb_ref, grb_ref,
                  hf_ref, hb_ref, ct_ref, m_ref):
    L = CHUNK

    @pl.when(pl.program_id(1) == 0)
    def _():
        ct_ref[...] = jnp.zeros_like(ct_ref)
        m_ref[...] = jnp.zeros_like(m_ref)

    row = lax.broadcasted_iota(jnp.int32, (L, L), 0)
    col = lax.broadcasted_iota(jnp.int32, (L, L), 1)
    assert L == LANES
    ones_blk = jnp.ones((L, LANES), BF16)

    def lane_tile(a, width):
        return jnp.concatenate([a] * (width // LANES), axis=1)

    dirs = ((qf_ref, kf_ref, vf_ref, gcf_ref, grf_ref, hf_ref),
            (qb_ref, kb_ref, vb_ref, gcb_ref, grb_ref, hb_ref))
    m_all = m_ref[...]
    m_out = []
    nb = qf_ref.shape[0]
    streams = [(bb, d, hd) for bb in range(nb) for d in range(2) for hd in range(N_HEADS_M)]

    def operands(bb, d, hd):
        q_ref, k_ref, v_ref = dirs[d][0:3]
        qh = q_ref[bb, :, hd * DQK_M:(hd + 1) * DQK_M]
        kh = k_ref[bb, :, hd * DQK_M:(hd + 1) * DQK_M]
        va = jnp.concatenate([v_ref[bb, :, hd * DV_M:(hd + 1) * DV_M], ones_blk], axis=1)
        return qh, kh, va

    for g0 in range(0, len(streams), MLSTM_GROUP):
        group = list(enumerate(streams))[g0:g0 + MLSTM_GROUP]
        gate = {}
        for st, (bb, d, hd) in group:
            tri = (col <= row) if d == 0 else (col >= row)
            e = L - 1 if d == 0 else 0
            gc = dirs[d][3][bb]
            gr = dirs[d][4][bb]
            ci = d * 2 * N_HEADS_M + hd
            cf = ci + N_HEADS_M
            i_row = gr[ci:ci + 1, :]
            b_row = gr[cf:cf + 1, :]
            b_t = jnp.broadcast_to(gc[:, cf:cf + 1], (L, LANES))
            m_prev = m_all[st:st + 1, :]
            dm = jnp.where(tri, b_t - b_row + i_row, -jnp.inf)
            inter = b_t + m_prev
            m_t = jnp.maximum(inter, jnp.broadcast_to(jnp.max(dm, axis=1, keepdims=True), (L, LANES)))
            w = jnp.exp(dm - m_t)
            w_inter = jnp.exp(inter - m_t)
            b_end = b_row[:, e:e + 1]
            m_new = m_t[e:e + 1, :]
            ws = jnp.exp(b_end - b_row + i_row - m_new)
            decay = jnp.exp(b_end + m_prev - m_new)
            gate[st] = (w, w_inter, jnp.exp(-m_t), ws, decay)
            m_out.append(m_new)
        sqk = {}
        for st, sid in group:
            qh, kh, _ = operands(*sid)
            sqk[st] = (_dot_nt(qh, kh) * gate[st][0]).astype(BF16)
        for st, (bb, d, hd) in group:
            qh, _, va = operands(bb, d, hd)
            _, w_inter, floor, _, _ = gate[st]
            numa = _dot(sqk[st], va) + lane_tile(w_inter, DV_AUG) * _dot(qh, ct_ref[st].astype(BF16))
            den = numa[:, DV_M:DV_AUG]
            inv = 1.0 / jnp.maximum(jnp.abs(den), floor)
            h = numa[:, 0:DV_M] * lane_tile(inv, DV_M)
            dirs[d][5][bb, :, hd * DV_M:(hd + 1) * DV_M] = h.astype(BF16)
        for st, sid in group:
            _, kh, va = operands(*sid)
            _, _, _, ws, decay = gate[st]
            kst = (kh.astype(F32).T * ws).astype(BF16)
            ct_ref[st] = lane_tile(decay, DV_AUG) * ct_ref[st] + _dot(kst, va)
    m_ref[...] = jnp.concatenate(m_out, axis=0)


def _mlstm(mq, mk, mv, gcol, grow):
    B, S, _ = mq.shape
    nc = S // CHUNK
    nb = 2 if B % 2 == 0 else 1
    fwd = lambda n: pl.BlockSpec((nb, CHUNK, n), lambda b, c: (b, c, 0))
    bwd = lambda n: pl.BlockSpec((nb, CHUNK, n), lambda b, c: (b, nc - 1 - c, 0))
    rowf = pl.BlockSpec((nb, W_MG, CHUNK), lambda b, c: (b, 0, c))
    rowb = pl.BlockSpec((nb, W_MG, CHUNK), lambda b, c: (b, 0, nc - 1 - c))
    return pl.pallas_call(
        _mlstm_kernel,
        grid=(B // nb, nc),
        in_specs=[fwd(W_MQ), fwd(W_MK), fwd(W_MV), fwd(W_MG), rowf,
                  bwd(W_MQ), bwd(W_MK), bwd(W_MV), bwd(W_MG), rowb],
        out_specs=(fwd(W_MV), bwd(W_MV)),
        out_shape=(jax.ShapeDtypeStruct((B, S, W_MV), BF16),
                   jax.ShapeDtypeStruct((B, S, W_MV), BF16)),
        scratch_shapes=[pltpu.VMEM((nb * N_STREAMS, DQK_M, DV_AUG), F32),
                        pltpu.VMEM((nb * N_STREAMS, LANES), F32)],
        compiler_params=_params(("parallel", "arbitrary")),
        name="mlstm",
    )(mq, mk, mv, gcol, grow, mq, mk, mv, gcol, grow)


def _mix_kernel(x_ref, ada_ref, ya_ref, hf_ref, hb_ref, og_ref, gab_ref,
                wpa_ref, wpm_ref, wout_ref, mng_ref, lg_ref, lb_ref, o_ref, *, alpha):
    hs = hf_ref[0].astype(F32) + hb_ref[0].astype(F32)
    hn = jnp.concatenate(
        [_norm(hs[:, hd * DV_M:(hd + 1) * DV_M]) for hd in range(N_HEADS_M)], axis=1)
    ym = (og_ref[0].astype(F32) * (hn * mng_ref[...])).astype(BF16)
    gab = gab_ref[0].astype(F32)
    pre = gab[:, 0:D_MODEL] * _dot(ya_ref[0], wpa_ref[...]) + gab[:, D_MODEL:] * _dot(ym, wpm_ref[...])
    mix = _dot(pre.astype(BF16), wout_ref[...])
    g1 = ada_ref[0][:, 2 * D_MODEL:3 * D_MODEL]
    o_ref[0] = _norm(alpha * x_ref[0] + g1 * mix) * lg_ref[...] + lb_ref[...]


def _mix(x, ada3, ya, hf, hb, og, gab, wpa, wpm, wout, mnorm_g, ln_g, ln_b, alpha, ts):
    B, S, D = x.shape
    tok = lambda n: pl.BlockSpec((1, ts, n), lambda b, i: (b, i, 0))
    consts = (wpa, wpm, wout, mnorm_g.reshape(1, W_MV).astype(F32),
              ln_g.reshape(1, D).astype(F32), ln_b.reshape(1, D).astype(F32))
    return pl.pallas_call(
        functools.partial(_mix_kernel, alpha=alpha),
        grid=(B, S // ts),
        in_specs=[tok(D), pl.BlockSpec((1, 1, 6 * D), lambda b, i: (b, 0, 0)),
                  tok(W_AV), tok(W_MV), tok(W_MV), tok(W_MO), tok(W_BG)]
                 + [_const_spec(w.shape) for w in consts],
        out_specs=tok(D),
        out_shape=jax.ShapeDtypeStruct((B, S, D), F32),
        compiler_params=_params(("parallel", "parallel")),
        name="mix",
    )(x, ada3, ya, hf, hb, og, gab, *consts)


def _gelu_tanh(x):
    return 0.5 * x * (1.0 + jnp.tanh(math.sqrt(2.0 / math.pi) * (x + 0.044715 * (x * x * x))))


def _ffn_kernel(xp_ref, x_ref, xn_ref, ada_ref, wug_ref, wuv_ref, cw_ref, cb_ref, wd_ref,
                lg_ref, lb_ref, o_ref, *, alpha):
    ts = x_ref.shape[1]
    H = SUBLANES
    i = pl.program_id(1)
    ada = ada_ref[0]
    sh2 = ada[:, 3 * D_MODEL:4 * D_MODEL]
    sc2 = ada[:, 4 * D_MODEL:5 * D_MODEL]
    g2 = ada[:, 5 * D_MODEL:6 * D_MODEL]
    x = x_ref[0]
    xx = jnp.concatenate([xp_ref[0], x, xn_ref[0]], axis=0)
    hx = _norm(xx) * (1.0 + sc2) + sh2
    ug = _dot(hx.astype(BF16), wug_ref[...])
    uv = _dot(hx[H:H + ts].astype(BF16), wuv_ref[...])
    n = ts + 2 * H
    prev = pltpu.roll(ug, 1, 0)[H:H + ts]
    nxt = pltpu.roll(ug, n - 1, 0)[H:H + ts]
    t = lax.broadcasted_iota(jnp.int32, (ts, 1), 0)
    keep_prev = jnp.where((i == 0) & (t == 0), 0.0, 1.0)
    keep_next = jnp.where((i == pl.num_programs(1) - 1) & (t == ts - 1), 0.0, 1.0)
    cw = cw_ref[...]
    conv = (prev * keep_prev * cw[0:1] + ug[H:H + ts] * cw[1:2] + nxt * keep_next * cw[2:3]
            + cb_ref[...])
    y = _dot((_gelu_tanh(conv) * uv).astype(BF16), wd_ref[...])
    o_ref[0] = _norm(alpha * x + g2 * y) * lg_ref[...] + lb_ref[...]


def _ffn(x, ada3, wug, wuv, conv_w, conv_b, wd, ln_g, ln_b, alpha, ts):
    B, S, D = x.shape
    H = SUBLANES
    nb = ts // H
    last = S // H - 1
    tok = pl.BlockSpec((1, ts, D), lambda b, i: (b, i, 0))
    prev = pl.BlockSpec((1, H, D), lambda b, i: (b, jnp.maximum(i * nb - 1, 0), 0))
    nxt = pl.BlockSpec((1, H, D), lambda b, i: (b, jnp.minimum((i + 1) * nb, last), 0))
    consts = (wug, wuv, conv_w.astype(F32), conv_b.reshape(1, D_FF).astype(F32), wd,
              ln_g.reshape(1, D).astype(F32), ln_b.reshape(1, D).astype(F32))
    return pl.pallas_call(
        functools.partial(_ffn_kernel, alpha=alpha),
        grid=(B, S // ts),
        in_specs=[prev, tok, nxt, pl.BlockSpec((1, 1, 6 * D), lambda b, i: (b, 0, 0))]
                 + [_const_spec(w.shape) for w in consts],
        out_specs=tok,
        out_shape=jax.ShapeDtypeStruct((B, S, D), F32),
        compiler_params=_params(("parallel", "parallel")),
        name="ffn",
    )(x, x, x, ada3, *consts)


def _rope_tables(seq):
    half = ROT_DIM // 2
    inv = ROPE_THETA ** (-jnp.arange(0, ROT_DIM, 2, dtype=F32) / ROT_DIM)
    ang = jnp.arange(seq, dtype=F32)[:, None] * inv[None, :]
    cos, sin = jnp.cos(ang), jnp.sin(ang)
    rest = DQK_A - ROT_DIM
    one = jnp.ones((seq, rest), F32)
    zr = jnp.zeros((seq, rest), F32)
    zh = jnp.zeros((seq, half), F32)
    reps = LANES // DQK_A
    ct = jnp.tile(jnp.concatenate([cos, cos, one], axis=1), (1, reps))
    s1 = jnp.tile(jnp.concatenate([-sin, zh, zr], axis=1), (1, reps))
    s2 = jnp.tile(jnp.concatenate([zh, sin, zr], axis=1), (1, reps))
    return ct, s1, s2


def _layer_weights(w_in, w_pa, w_pm, w_out, w_up, w_down):
    seg = lambda j: w_in[:, SEG_STARTS[j]:SEG_STARTS[j] + SEG_WIDTHS[j]]
    wg = seg(7)
    proj = (seg(0).astype(BF16), seg(1).astype(BF16), seg(2).T.astype(BF16),
            seg(3).astype(BF16), seg(4).astype(BF16), seg(5).astype(BF16), seg(6).astype(BF16),
            jnp.pad(wg, ((0, 0), (0, LANES - W_MG))).astype(BF16), wg.T.astype(BF16),
            seg(8).astype(BF16))
    return (proj, w_pa.astype(BF16), w_pm.astype(BF16), w_out.astype(BF16),
            w_up[:, :D_FF].astype(BF16), w_up[:, D_FF:].astype(BF16), w_down.astype(BF16))


def _tile(seq, want):
    t = min(seq, want)
    assert seq % t == 0, (seq, want)
    return t


def _default_tiles(seq):
    return (512, 1024, min(2048, max(seq // 4, 256)), 256)


def _encoder_layer(x, ada, l, depth, weights, lambda_qk, subln_g, b_mgate, mnorm_g,
                   ln1_g, ln1_b, conv_w, conv_b, ln2_g, ln2_b, tiles):
    B, S, D = x.shape
    ts, tq, tk, sub = (_tile(S, t) for t in (tiles or _default_tiles(S)))
    alpha = (2.0 * depth) ** 0.25
    lam_init = 0.8 - 0.6 * math.exp(-0.3 * l)
    proj_w, wpa, wpm, wout, wug, wuv, wd = weights
    ada3 = ada.reshape(B, 1, 6 * D)
    q, k, vt, mq, mk, mv, og, gcol, grow, gab = _proj(x, ada3, _rope_tables(S), proj_w, b_mgate, ts)
    ya = _attn(q, k, vt, lambda_qk, subln_g, lam_init, tq, tk, sub)
    hf, hb = _mlstm(mq, mk, mv, gcol, grow)
    x1 = _mix(x, ada3, ya, hf, hb, og, gab, wpa, wpm, wout, mnorm_g, ln1_g, ln1_b, alpha, ts)
    return _ffn(x1, ada3, wug, wuv, conv_w, conv_b, wd, ln2_g, ln2_b, alpha, ts)


def _trunk(xs, cs, w_ada, b_ada, w_in, lambda_qk, subln_g, b_mgate, mnorm_g, w_pa, w_pm, w_out,
           ln1_g, ln1_b, w_up, conv_w, conv_b, w_down, ln2_g, ln2_b, tiles=None):
    depth = w_in.shape[0]
    rows = [c.shape[0] for c in cs]
    c_all = jnp.concatenate(cs, axis=0)
    xs = list(xs)
    for l in range(depth):
        ada_all = _ada(c_all, w_ada[l], b_ada[l])
        weights = _layer_weights(w_in[l], w_pa[l], w_pm[l], w_out[l], w_up[l], w_down[l])
        start = 0
        for g, n in enumerate(rows):
            xs[g] = _encoder_layer(xs[g], ada_all[start:start + n], l, depth, weights,
                                   lambda_qk[l], subln_g[l], b_mgate[l], mnorm_g[l],
                                   ln1_g[l], ln1_b[l], conv_w[l], conv_b[l], ln2_g[l], ln2_b[l], tiles)
            start += n
    return tuple(xs)


def kernel(x_prompt, x_sample, c_prompt, c_sample, w_ada, b_ada, w_in, lambda_qk, subln_g, b_mgate, mnorm_g, w_pa, w_pm, w_out, ln1_g, ln1_b, w_up, conv_w, conv_b, w_down, ln2_g, ln2_b):
    return _trunk((x_prompt, x_sample), (c_prompt, c_sample), w_ada, b_ada, w_in, lambda_qk, subln_g,
                  b_mgate, mnorm_g, w_pa, w_pm, w_out, ln1_g, ln1_b, w_up, conv_w, conv_b, w_down,
                  ln2_g, ln2_b)
```

```python
import functools
import math

import jax
import jax.numpy as jnp
import numpy as np
from jax import lax
from jax.experimental import pallas as pl
from jax.experimental.pallas import tpu as pltpu

F32 = jnp.float32
BF16 = jnp.bfloat16

D_MODEL = 1024
N_HEADS_A = 8
DQK_A = 64
DV_A = 2 * DQK_A
ROT_DIM = DQK_A // 4
ROPE_THETA = 500000.0
N_HEADS_M = 4
DQK_M = 128
DV_M = 256
CHUNK = 128
D_FF = 2816
LN_EPS = 1e-5

W_AQ = N_HEADS_A * 2 * DQK_A
W_AK = N_HEADS_A * 2 * DQK_A
W_AV = N_HEADS_A * DV_A
W_MQ = N_HEADS_M * DQK_M
W_MK = N_HEADS_M * DQK_M
W_MV = N_HEADS_M * DV_M
W_MO = N_HEADS_M * DV_M
W_MG = 2 * 2 * N_HEADS_M
W_BG = 2 * D_MODEL
SEG_WIDTHS = (W_AQ, W_AK, W_AV, W_MQ, W_MK, W_MV, W_MO, W_MG, W_BG)
SEG_STARTS = [0] + [int(s) for s in np.cumsum(SEG_WIDTHS)[:-1]]

LANES = 128
SUBLANES = 8
LOG2E = 1.4426950408889634
VMEM_LIMIT = 56 * 1024 * 1024

NT_DIMS = (((1,), (1,)), ((), ()))


def _dot(a, b):
    return jnp.dot(a, b, preferred_element_type=F32)


def _dot_nt(a, b):
    return lax.dot_general(a, b, NT_DIMS, preferred_element_type=F32)


def _norm(x):
    mu = jnp.mean(x, axis=-1, keepdims=True)
    xc = x - mu
    var = jnp.mean(xc * xc, axis=-1, keepdims=True)
    return xc * lax.rsqrt(var + LN_EPS)


def _split_bf16(a):
    hi = a.astype(BF16)
    lo = (a - hi.astype(F32)).astype(BF16)
    return hi, lo


def _const_spec(shape):
    nd = len(shape)
    return pl.BlockSpec(shape, lambda *_: (0,) * nd, pipeline_mode=pl.Buffered(1))


def _params(semantics):
    return pltpu.CompilerParams(dimension_semantics=semantics, vmem_limit_bytes=VMEM_LIMIT)


def _ada_kernel(c_ref, w_ref, b_ref, o_ref):
    c = c_ref[...]
    a = c * jax.nn.sigmoid(c)
    a_hi, a_lo = _split_bf16(a)
    w_hi, w_lo = _split_bf16(w_ref[...])
    o_ref[...] = _dot(a_hi, w_hi) + _dot(a_lo, w_hi) + _dot(a_hi, w_lo) + b_ref[...]


def _ada(c, w_ada, b_ada):
    rows, d = c.shape
    n = w_ada.shape[1]
    tn = 1024
    return pl.pallas_call(
        _ada_kernel,
        grid=(n // tn,),
        in_specs=[pl.BlockSpec((rows, d), lambda j: (0, 0)),
                  pl.BlockSpec((d, tn), lambda j: (0, j)),
                  pl.BlockSpec((1, tn), lambda j: (0, j))],
        out_specs=pl.BlockSpec((rows, tn), lambda j: (0, j)),
        out_shape=jax.ShapeDtypeStruct((rows, n), F32),
        compiler_params=_params(("parallel",)),
        name="ada",
    )(c, w_ada, b_ada.reshape(1, n))


def _log_sigmoid(g):
    return jnp.minimum(g, 0.0) - jnp.log1p(jnp.exp(-jnp.abs(g)))


def _proj_kernel(x_ref, ada_ref, ct_ref, s1_ref, s2_ref,
                 wq_ref, wk_ref, wvt_ref, wmq_ref, wmk_ref, wmv_ref, wmo_ref,
                 wgr_ref, bgr_ref, wbg_ref,
                 q_ref, k_ref, vt_ref, mq_ref, mk_ref, mv_ref, og_ref, gc_ref, gr_ref, gab_ref):
    ts = x_ref.shape[1]
    ada = ada_ref[0]
    sh1 = ada[:, 0:D_MODEL]
    sc1 = ada[:, D_MODEL:2 * D_MODEL]
    h = (_norm(x_ref[0]) * (1.0 + sc1) + sh1).astype(BF16)

    ct = ct_ref[...]
    s1 = s1_ref[...]
    s2 = s2_ref[...]

    def rope_store(w_ref, o_ref, scale):
        a = _dot(h, w_ref[...])
        for j in range(W_AQ // LANES):
            blk = a[:, j * LANES:(j + 1) * LANES]
            r = (blk * ct + pltpu.roll(blk, LANES - ROT_DIM // 2, 1) * s1
                 + pltpu.roll(blk, ROT_DIM // 2, 1) * s2)
            o_ref[0, :, j * LANES:(j + 1) * LANES] = (r * scale).astype(o_ref.dtype)

    rope_store(wq_ref, q_ref, (DQK_A ** -0.5) * LOG2E)
    rope_store(wk_ref, k_ref, 1.0)
    vt_ref[0] = _dot_nt(wvt_ref[...], h).astype(vt_ref.dtype)
    mq_ref[0] = _dot(h, wmq_ref[...]).astype(mq_ref.dtype)
    mk_ref[0] = (_dot(h, wmk_ref[...]) * (DQK_M ** -0.5)).astype(mk_ref.dtype)
    mv_ref[0] = _dot(h, wmv_ref[...]).astype(mv_ref.dtype)
    og_ref[0] = jax.nn.sigmoid(_dot(h, wmo_ref[...])).astype(og_ref.dtype)
    gab_ref[0] = jax.nn.sigmoid(_dot(h, wbg_ref[...])).astype(gab_ref.dtype)

    r = lax.broadcasted_iota(jnp.int32, (ts, ts), 0)
    c = lax.broadcasted_iota(jnp.int32, (ts, ts), 1)
    shift = CHUNK.bit_length() - 1
    same = jnp.right_shift(r, shift) == jnp.right_shift(c, shift)
    lower = jnp.where(same & (c <= r), 1.0, 0.0).astype(BF16)
    upper = jnp.where(same & (c >= r), 1.0, 0.0).astype(BF16)

    g = _dot_nt(wgr_ref[...], h) + bgr_ref[...]
    ch = lax.broadcasted_iota(jnp.int32, g.shape, 0)
    is_f = (ch & (2 * N_HEADS_M - 1)) >= N_HEADS_M
    val = jnp.where(is_f, _log_sigmoid(g), g)
    f_hi, f_lo = _split_bf16(jnp.where(is_f & (ch < 2 * N_HEADS_M), val, 0.0))
    b_hi, b_lo = _split_bf16(jnp.where(is_f & (ch >= 2 * N_HEADS_M), val, 0.0))
    grow = (_dot(f_hi, upper) + _dot(f_lo, upper) + _dot(b_hi, lower) + _dot(b_lo, lower)
            + jnp.where(is_f, 0.0, val))
    gr_ref[0] = grow
    padded = jnp.concatenate([grow, jnp.zeros((LANES - W_MG, ts), F32)], axis=0)
    for j in range(ts // LANES):
        gc_ref[0, j * LANES:(j + 1) * LANES, :] = padded[:, j * LANES:(j + 1) * LANES].T[:, 0:W_MG]


def _proj(x, ada3, rope, wts, b_mgate, ts):
    B, S, D = x.shape
    ct, s1, s2 = rope
    wq, wk, wvt, wmq, wmk, wmv, wmo, wgr, wbg = wts
    bgr = b_mgate.reshape(W_MG, 1).astype(F32)
    tok = lambda n: pl.BlockSpec((1, ts, n), lambda b, i: (b, i, 0))
    tab = pl.BlockSpec((ts, LANES), lambda b, i: (i, 0))
    consts = (wq, wk, wvt, wmq, wmk, wmv, wmo, wgr, bgr, wbg)
    out_shape = (
        jax.ShapeDtypeStruct((B, S, W_AQ), BF16),
        jax.ShapeDtypeStruct((B, S, W_AK), BF16),
        jax.ShapeDtypeStruct((B, W_AV, S), BF16),
        jax.ShapeDtypeStruct((B, S, W_MQ), BF16),
        jax.ShapeDtypeStruct((B, S, W_MK), BF16),
        jax.ShapeDtypeStruct((B, S, W_MV), BF16),
        jax.ShapeDtypeStruct((B, S, W_MO), BF16),
        jax.ShapeDtypeStruct((B, S, W_MG), F32),
        jax.ShapeDtypeStruct((B, W_MG, S), F32),
        jax.ShapeDtypeStruct((B, S, W_BG), BF16),
    )
    out_specs = (
        tok(W_AQ), tok(W_AK),
        pl.BlockSpec((1, W_AV, ts), lambda b, i: (b, 0, i)),
        tok(W_MQ), tok(W_MK), tok(W_MV), tok(W_MO), tok(W_MG),
        pl.BlockSpec((1, W_MG, ts), lambda b, i: (b, 0, i)),
        tok(W_BG),
    )
    return pl.pallas_call(
        _proj_kernel,
        grid=(B, S // ts),
        in_specs=[tok(D), pl.BlockSpec((1, 1, 6 * D), lambda b, i: (b, 0, 0)), tab, tab, tab]
                 + [_const_spec(w.shape) for w in consts],
        out_specs=out_specs,
        out_shape=out_shape,
        compiler_params=_params(("parallel", "parallel")),
        name="proj",
    )(x, ada3, ct, s1, s2, *consts)


ONES_ROWS = 16


def _attn_kernel(lam_ref, q_ref, k_ref, vt_ref, g_ref, o_ref,
                 s_ref, p_ref, acc, m_ref, mx_ref, alpha_ref, *, tk, sub, lam_init):
    n = k_ref.shape[1] // tk
    nsub = tk // sub
    q = q_ref[0]
    lane = lax.broadcasted_iota(jnp.int32, q.shape, 1)
    zero = jnp.zeros_like(q)
    qs = (jnp.where(lane < DQK_A, q, zero), jnp.where(lane >= DQK_A, q, zero))
    ones_rows = jnp.ones((ONES_ROWS, sub), BF16)
    maps = range(2)

    acc[...] = jnp.zeros_like(acc)
    m_ref[...] = jnp.full_like(m_ref, -jnp.inf)

    def scores_sub(j, r, mx):
        kc = k_ref[0, pl.ds(pl.multiple_of(j * tk + r * sub, sub), sub), :]
        out = []
        for mp in maps:
            s = _dot_nt(kc, qs[mp])
            s_ref[mp, r * sub:(r + 1) * sub, :] = s
            cm = jnp.max(s, axis=0, keepdims=True)
            out.append(cm if mx is None else jnp.maximum(mx[mp], cm))
        return out

    def softmax_sub(r, m_new):
        for mp in maps:
            s = s_ref[mp, r * sub:(r + 1) * sub, :]
            p_ref[mp, r * sub:(r + 1) * sub, :] = jnp.exp2(s - m_new[mp]).astype(BF16)

    def values_sub(j, r, pv):
        vc = vt_ref[0, :, pl.ds(pl.multiple_of(j * tk + r * sub, sub), sub)]
        va = jnp.concatenate([vc, ones_rows], axis=0)
        out = []
        for mp in maps:
            d = _dot(va, p_ref[mp, r * sub:(r + 1) * sub, :])
            out.append(d if pv is None else pv[mp] + d)
        return out

    def accumulate(pv):
        for mp in maps:
            acc[mp] = alpha_ref[mp] * acc[mp] + pv[mp]

    def trip(j, first, last):
        m_old = [m_ref[mp] for mp in maps]
        m_new = [jnp.maximum(m_old[mp], mx_ref[mp]) for mp in maps]
        pv = mx = None
        for r in range(nsub):
            if not first:
                pv = values_sub(j - 1, r, pv)
            softmax_sub(r, m_new)
            if not last:
                mx = scores_sub(j + 1, r, mx)
        if not first:
            accumulate(pv)
        for mp in maps:
            alpha_ref[mp] = jnp.exp2(m_old[mp] - m_new[mp])
            m_ref[mp] = m_new[mp]
            if not last:
                mx_ref[mp] = mx[mp]

    mx = None
    for r in range(nsub):
        mx = scores_sub(0, r, mx)
    for mp in maps:
        mx_ref[mp] = mx[mp]
    trip(0, True, False)

    def body(j, carry):
        trip(j, False, False)
        return carry

    lax.fori_loop(1, n - 1, body, 0)
    trip(n - 1, False, True)
    pv = None
    for r in range(nsub):
        pv = values_sub(n - 1, r, pv)
    accumulate(pv)

    lq = lam_ref[...]
    la = jnp.sum(lq[0:1] * lq[1:2], axis=1, keepdims=True)
    lb = jnp.sum(lq[2:3] * lq[3:4], axis=1, keepdims=True)
    lam = jnp.exp(la) - jnp.exp(lb) + lam_init
    a1 = acc[0]
    a2 = acc[1]
    o = a1[0:DV_A] / a1[DV_A:DV_A + 1] - lam * (a2[0:DV_A] / a2[DV_A:DV_A + 1])
    ms = jnp.mean(o * o, axis=0, keepdims=True)
    y = o * lax.rsqrt(ms + LN_EPS) * g_ref[...] * (1.0 - lam_init)
    o_ref[0] = y.T.astype(o_ref.dtype)


def _attn(q, k, vt, lambda_qk, subln_g, lam_init, tq, tk, sub):
    B, S, _ = q.shape
    assert S // tk >= 2 and tk % sub == 0, (S, tk, sub)
    kern = functools.partial(_attn_kernel, tk=tk, sub=sub, lam_init=lam_init)
    return pl.pallas_call(
        kern,
        grid=(B, N_HEADS_A, S // tq),
        in_specs=[pl.BlockSpec((4, DQK_A), lambda b, h, i: (0, 0)),
                  pl.BlockSpec((1, tq, DV_A), lambda b, h, i: (b, i, h)),
                  pl.BlockSpec((1, S, DV_A), lambda b, h, i: (b, 0, h)),
                  pl.BlockSpec((1, DV_A, S), lambda b, h, i: (b, h, 0)),
                  pl.BlockSpec((DV_A, 1), lambda b, h, i: (0, 0))],
        out_specs=pl.BlockSpec((1, tq, DV_A), lambda b, h, i: (b, i, h)),
        out_shape=jax.ShapeDtypeStruct((B, S, N_HEADS_A * DV_A), BF16),
        scratch_shapes=[pltpu.VMEM((2, tk, tq), F32), pltpu.VMEM((2, tk, tq), BF16),
                        pltpu.VMEM((2, DV_A + ONES_ROWS, tq), F32),
                        pltpu.VMEM((2, 1, tq), F32), pltpu.VMEM((2, 1, tq), F32),
                        pltpu.VMEM((2, 1, tq), F32)],
        compiler_params=_params(("parallel", "parallel", "parallel")),
        name="attn",
    )(lambda_qk.astype(F32), q, k, vt, subln_g.reshape(DV_A, 1).astype(F32))


N_STREAMS = 2 * N_HEADS_M
MLSTM_GROUP = 4
DV_AUG = DV_M + LANES


def _mlstm_kernel(qf_ref, kf_ref, vf_ref, gcf_ref, grf_ref,
                  qb_ref, kb_ref, vb_ref, gcb_ref, grb_ref,
                  hf_ref, hb_ref, ct_ref, m_ref):
    L = CHUNK

    @pl.when(pl.program_id(1) == 0)
    def _():
        ct_ref[...] = jnp.zeros_like(ct_ref)
        m_ref[...] = jnp.zeros_like(m_ref)

    row = lax.broadcasted_iota(jnp.int32, (L, L), 0)
    col = lax.broadcasted_iota(jnp.int32, (L, L), 1)
    assert L == LANES
    ones_blk = jnp.ones((L, LANES), BF16)

    def lane_tile(a, width):
        return jnp.concatenate([a] * (width // LANES), axis=1)

    dirs = ((qf_ref, kf_ref, vf_ref, gcf_ref, grf_ref, hf_ref),
            (qb_ref, kb_ref, vb_ref, gcb_ref, grb_ref, hb_ref))
    m_all = m_ref[...]
    m_out = []
    nb = qf_ref.shape[0]
    streams = [(bb, d, hd) for bb in range(nb) for d in range(2) for hd in range(N_HEADS_M)]

    def operands(bb, d, hd):
        q_ref, k_ref, v_ref = dirs[d][0:3]
        qh = q_ref[bb, :, hd * DQK_M:(hd + 1) * DQK_M]
        kh = k_ref[bb, :, hd * DQK_M:(hd + 1) * DQK_M]
        va = jnp.concatenate([v_ref[bb, :, hd * DV_M:(hd + 1) * DV_M], ones_blk], axis=1)
        return qh, kh, va

    for g0 in range(0, len(streams), MLSTM_GROUP):
        group = list(enumerate(streams))[g0:g0 + MLSTM_GROUP]
        gate = {}
        for st, (bb, d, hd) in group:
            tri = (col <= row) if d == 0 else (col >= row)
            e = L - 1 if d == 0 else 0
            gc = dirs[d][3][bb]
            gr = dirs[d][4][bb]
            ci = d * 2 * N_HEADS_M + hd
            cf = ci + N_HEADS_M
            i_row = gr[ci:ci + 1, :]
            b_row = gr[cf:cf + 1, :]
            b_t = jnp.broadcast_to(gc[:, cf:cf + 1], (L, LANES))
            m_prev = m_all[st:st + 1, :]
            dm = jnp.where(tri, b_t - b_row + i_row, -jnp.inf)
            inter = b_t + m_prev
            m_t = jnp.maximum(inter, jnp.broadcast_to(jnp.max(dm, axis=1, keepdims=True), (L, LANES)))
            w = jnp.exp(dm - m_t)
            w_inter = jnp.exp(inter - m_t)
            b_end = b_row[:, e:e + 1]
            m_new = m_t[e:e + 1, :]
            ws = jnp.exp(b_end - b_row + i_row - m_new)
            decay = jnp.exp(b_end + m_prev - m_new)
            gate[st] = (w, w_inter, jnp.exp(-m_t), ws, decay)
            m_out.append(m_new)
        sqk = {}
        for st, sid in group:
            qh, kh, _ = operands(*sid)
            sqk[st] = (_dot_nt(qh, kh) * gate[st][0]).astype(BF16)
        for st, (bb, d, hd) in group:
            qh, _, va = operands(bb, d, hd)
            _, w_inter, floor, _, _ = gate[st]
            numa = _dot(sqk[st], va) + lane_tile(w_inter, DV_AUG) * _dot(qh, ct_ref[st].astype(BF16))
            den = numa[:, DV_M:DV_AUG]
            inv = 1.0 / jnp.maximum(jnp.abs(den), floor)
            h = numa[:, 0:DV_M] * lane_tile(inv, DV_M)
            dirs[d][5][bb, :, hd * DV_M:(hd + 1) * DV_M] = h.astype(BF16)
        for st, sid in group:
            _, kh, va = operands(*sid)
            _, _, _, ws, decay = gate[st]
            kst = (kh.astype(F32).T * ws).astype(BF16)
            ct_ref[st] = lane_tile(decay, DV_AUG) * ct_ref[st] + _dot(kst, va)
    m_ref[...] = jnp.concatenate(m_out, axis=0)


def _mlstm(mq, mk, mv, gcol, grow):
    B, S, _ = mq.shape
    nc = S // CHUNK
    nb = 2 if B % 2 == 0 else 1
    fwd = lambda n: pl.BlockSpec((nb, CHUNK, n), lambda b, c: (b, c, 0))
    bwd = lambda n: pl.BlockSpec((nb, CHUNK, n), lambda b, c: (b, nc - 1 - c, 0))
    rowf = pl.BlockSpec((nb, W_MG, CHUNK), lambda b, c: (b, 0, c))
    rowb = pl.BlockSpec((nb, W_MG, CHUNK), lambda b, c: (b, 0, nc - 1 - c))
    return pl.pallas_call(
        _mlstm_kernel,
        grid=(B // nb, nc),
        in_specs=[fwd(W_MQ), fwd(W_MK), fwd(W_MV), fwd(W_MG), rowf,
                  bwd(W_MQ), bwd(W_MK), bwd(W_MV), bwd(W_MG), rowb],
        out_specs=(fwd(W_MV), bwd(W_MV)),
        out_shape=(jax.ShapeDtypeStruct((B, S, W_MV), BF16),
                   jax.ShapeDtypeStruct((B, S, W_MV), BF16)),
        scratch_shapes=[pltpu.VMEM((nb * N_STREAMS, DQK_M, DV_AUG), F32),
                        pltpu.VMEM((nb * N_STREAMS, LANES), F32)],
        compiler_params=_params(("parallel", "arbitrary")),
        name="mlstm",
    )(mq, mk, mv, gcol, grow, mq, mk, mv, gcol, grow)


def _mix_kernel(x_ref, ada_ref, ya_ref, hf_ref, hb_ref, og_ref, gab_ref,
                wpa_ref, wpm_ref, wout_ref, mng_ref, lg_ref, lb_ref, o_ref, *, alpha):
    hs = hf_ref[0].astype(F32) + hb_ref[0].astype(F32)
    hn = jnp.concatenate(
        [_norm(hs[:, hd * DV_M:(hd + 1) * DV_M]) for hd in range(N_HEADS_M)], axis=1)
    ym = (og_ref[0].astype(F32) * (hn * mng_ref[...])).astype(BF16)
    gab = gab_ref[0].astype(F32)
    pre = gab[:, 0:D_MODEL] * _dot(ya_ref[0], wpa_ref[...]) + gab[:, D_MODEL:] * _dot(ym, wpm_ref[...])
    mix = _dot(pre.astype(BF16), wout_ref[...])
    g1 = ada_ref[0][:, 2 * D_MODEL:3 * D_MODEL]
    o_ref[0] = _norm(alpha * x_ref[0] + g1 * mix) * lg_ref[...] + lb_ref[...]


def _mix(x, ada3, ya, hf, hb, og, gab, wpa, wpm, wout, mnorm_g, ln_g, ln_b, alpha, ts):
    B, S, D = x.shape
    tok = lambda n: pl.BlockSpec((1, ts, n), lambda b, i: (b, i, 0))
    consts = (wpa, wpm, wout, mnorm_g.reshape(1, W_MV).astype(F32),
              ln_g.reshape(1, D).astype(F32), ln_b.reshape(1, D).astype(F32))
    return pl.pallas_call(
        functools.partial(_mix_kernel, alpha=alpha),
        grid=(B, S // ts),
        in_specs=[tok(D), pl.BlockSpec((1, 1, 6 * D), lambda b, i: (b, 0, 0)),
                  tok(W_AV), tok(W_MV), tok(W_MV), tok(W_MO), tok(W_BG)]
                 + [_const_spec(w.shape) for w in consts],
        out_specs=tok(D),
        out_shape=jax.ShapeDtypeStruct((B, S, D), F32),
        compiler_params=_params(("parallel", "parallel")),
        name="mix",
    )(x, ada3, ya, hf, hb, og, gab, *consts)


def _gelu_tanh(x):
    return 0.5 * x * (1.0 + jnp.tanh(math.sqrt(2.0 / math.pi) * (x + 0.044715 * (x * x * x))))


def _ffn_kernel(xp_ref, x_ref, xn_ref, ada_ref, wug_ref, wuv_ref, cw_ref, cb_ref, wd_ref,
                lg_ref, lb_ref, o_ref, *, alpha):
    ts = x_ref.shape[1]
    H = SUBLANES
    i = pl.program_id(1)
    ada = ada_ref[0]
    sh2 = ada[:, 3 * D_MODEL:4 * D_MODEL]
    sc2 = ada[:, 4 * D_MODEL:5 * D_MODEL]
    g2 = ada[:, 5 * D_MODEL:6 * D_MODEL]
    x = x_ref[0]

    def modulated(v):
        return _norm(v) * (1.0 + sc2) + sh2

    hp = jnp.where(i == 0, 0.0, modulated(xp_ref[0]))
    hm = modulated(x)
    hn = jnp.where(i == pl.num_programs(1) - 1, 0.0, modulated(xn_ref[0]))
    ug = _dot(jnp.concatenate([hp, hm, hn], axis=0).astype(BF16), wug_ref[...])
    uv = _dot(hm.astype(BF16), wuv_ref[...])
    n = ts + 2 * H
    prev = pltpu.roll(ug, 1, 0)[H:H + ts]
    nxt = pltpu.roll(ug, n - 1, 0)[H:H + ts]
    cw = cw_ref[...]
    conv = prev * cw[0:1] + ug[H:H + ts] * cw[1:2] + nxt * cw[2:3] + cb_ref[...]
    y = _dot((_gelu_tanh(conv) * uv).astype(BF16), wd_ref[...])
    o_ref[0] = _norm(alpha * x + g2 * y) * lg_ref[...] + lb_ref[...]


def _ffn(x, ada3, wug, wuv, conv_w, conv_b, wd, ln_g, ln_b, alpha, ts):
    B, S, D = x.shape
    H = SUBLANES
    nb = ts // H
    last = S // H - 1
    tok = pl.BlockSpec((1, ts, D), lambda b, i: (b, i, 0))
    prev = pl.BlockSpec((1, H, D), lambda b, i: (b, jnp.maximum(i * nb - 1, 0), 0))
    nxt = pl.BlockSpec((1, H, D), lambda b, i: (b, jnp.minimum((i + 1) * nb, last), 0))
    consts = (wug, wuv, conv_w.astype(F32), conv_b.reshape(1, D_FF).astype(F32), wd,
              ln_g.reshape(1, D).astype(F32), ln_b.reshape(1, D).astype(F32))
    return pl.pallas_call(
        functools.partial(_ffn_kernel, alpha=alpha),
        grid=(B, S // ts),
        in_specs=[prev, tok, nxt, pl.BlockSpec((1, 1, 6 * D), lambda b, i: (b, 0, 0))]
                 + [_const_spec(w.shape) for w in consts],
        out_specs=tok,
        out_shape=jax.ShapeDtypeStruct((B, S, D), F32),
        compiler_params=_params(("parallel", "parallel")),
        name="ffn",
    )(x, x, x, ada3, *consts)


def _rope_tables(seq):
    half = ROT_DIM // 2
    inv = ROPE_THETA ** (-jnp.arange(0, ROT_DIM, 2, dtype=F32) / ROT_DIM)
    ang = jnp.arange(seq, dtype=F32)[:, None] * inv[None, :]
    cos, sin = jnp.cos(ang), jnp.sin(ang)
    rest = DQK_A - ROT_DIM
    one = jnp.ones((seq, rest), F32)
    zr = jnp.zeros((seq, rest), F32)
    zh = jnp.zeros((seq, half), F32)
    reps = LANES // DQK_A
    ct = jnp.tile(jnp.concatenate([cos, cos, one], axis=1), (1, reps))
    s1 = jnp.tile(jnp.concatenate([-sin, zh, zr], axis=1), (1, reps))
    s2 = jnp.tile(jnp.concatenate([zh, sin, zr], axis=1), (1, reps))
    return ct, s1, s2


def _layer_weights(w_in, w_pa, w_pm, w_out, w_up, w_down):
    seg = lambda j: w_in[:, SEG_STARTS[j]:SEG_STARTS[j] + SEG_WIDTHS[j]]
    proj = (seg(0).astype(BF16), seg(1).astype(BF16), seg(2).T.astype(BF16),
            seg(3).astype(BF16), seg(4).astype(BF16), seg(5).astype(BF16), seg(6).astype(BF16),
            seg(7).T.astype(BF16), seg(8).astype(BF16))
    return (proj, w_pa.astype(BF16), w_pm.astype(BF16), w_out.astype(BF16),
            w_up[:, :D_FF].astype(BF16), w_up[:, D_FF:].astype(BF16), w_down.astype(BF16))


def _tile(seq, want):
    t = min(seq, want)
    assert seq % t == 0, (seq, want)
    return t


def _default_tiles(seq):
    return (512, 1024, min(2048, max(seq // 4, 256)), 256)


def _encoder_layer(x, ada, l, depth, weights, lambda_qk, subln_g, b_mgate, mnorm_g,
                   ln1_g, ln1_b, conv_w, conv_b, ln2_g, ln2_b, tiles):
    B, S, D = x.shape
    ts, tq, tk, sub = (_tile(S, t) for t in (tiles or _default_tiles(S)))
    alpha = (2.0 * depth) ** 0.25
    lam_init = 0.8 - 0.6 * math.exp(-0.3 * l)
    proj_w, wpa, wpm, wout, wug, wuv, wd = weights
    ada3 = ada.reshape(B, 1, 6 * D)
    q, k, vt, mq, mk, mv, og, gcol, grow, gab = _proj(x, ada3, _rope_tables(S), proj_w, b_mgate, ts)
    ya = _attn(q, k, vt, lambda_qk, subln_g, lam_init, tq, tk, sub)
    hf, hb = _mlstm(mq, mk, mv, gcol, grow)
    x1 = _mix(x, ada3, ya, hf, hb, og, gab, wpa, wpm, wout, mnorm_g, ln1_g, ln1_b, alpha, ts)
    return _ffn(x1, ada3, wug, wuv, conv_w, conv_b, wd, ln2_g, ln2_b, alpha, ts)


def _trunk(xs, cs, w_ada, b_ada, w_in, lambda_qk, subln_g, b_mgate, mnorm_g, w_pa, w_pm, w_out,
           ln1_g, ln1_b, w_up, conv_w, conv_b, w_down, ln2_g, ln2_b, tiles=None):
    depth = w_in.shape[0]
    rows = [c.shape[0] for c in cs]
    c_all = jnp.concatenate(cs, axis=0)
    xs = list(xs)
    for l in range(depth):
        ada_all = _ada(c_all, w_ada[l], b_ada[l])
        weights = _layer_weights(w_in[l], w_pa[l], w_pm[l], w_out[l], w_up[l], w_down[l])
        start = 0
        for g, n in enumerate(rows):
            xs[g] = _encoder_layer(xs[g], ada_all[start:start + n], l, depth, weights,
                                   lambda_qk[l], subln_g[l], b_mgate[l], mnorm_g[l],
                                   ln1_g[l], ln1_b[l], conv_w[l], conv_b[l], ln2_g[l], ln2_b[l], tiles)
            start += n
    return tuple(xs)


def kernel(x_prompt, x_sample, c_prompt, c_sample, w_ada, b_ada, w_in, lambda_qk, subln_g, b_mgate, mnorm_g, w_pa, w_pm, w_out, ln1_g, ln1_b, w_up, conv_w, conv_b, w_down, ln2_g, ln2_b):
    return _trunk((x_prompt, x_sample), (c_prompt, c_sample), w_ada, b_ada, w_in, lambda_qk, subln_g,
                  b_mgate, mnorm_g, w_pa, w_pm, w_out, ln1_g, ln1_b, w_up, conv_w, conv_b, w_down,
                  ln2_g, ln2_b)
```

```python
import functools
import math

import jax
import jax.numpy as jnp
import numpy as np
from jax import lax
from jax.experimental import pallas as pl
from jax.experimental.pallas import tpu as pltpu

F32 = jnp.float32
BF16 = jnp.bfloat16

D_MODEL = 1024
N_HEADS_A = 8
DQK_A = 64
DV_A = 2 * DQK_A
ROT_DIM = DQK_A // 4
ROPE_THETA = 500000.0
N_HEADS_M = 4
DQK_M = 128
DV_M = 256
CHUNK = 128
D_FF = 2816
LN_EPS = 1e-5

W_AQ = N_HEADS_A * 2 * DQK_A
W_AK = N_HEADS_A * 2 * DQK_A
W_AV = N_HEADS_A * DV_A
W_MQ = N_HEADS_M * DQK_M
W_MK = N_HEADS_M * DQK_M
W_MV = N_HEADS_M * DV_M
W_MO = N_HEADS_M * DV_M
W_MG = 2 * 2 * N_HEADS_M
W_BG = 2 * D_MODEL
SEG_WIDTHS = (W_AQ, W_AK, W_AV, W_MQ, W_MK, W_MV, W_MO, W_MG, W_BG)
SEG_STARTS = [0] + [int(s) for s in np.cumsum(SEG_WIDTHS)[:-1]]

LANES = 128
SUBLANES = 8
LOG2E = 1.4426950408889634
VMEM_LIMIT = 56 * 1024 * 1024

NT_DIMS = (((1,), (1,)), ((), ()))


def _dot(a, b):
    return jnp.dot(a, b, preferred_element_type=F32)


def _dot_nt(a, b):
    return lax.dot_general(a, b, NT_DIMS, preferred_element_type=F32)


def _norm(x):
    mu = jnp.mean(x, axis=-1, keepdims=True)
    xc = x - mu
    var = jnp.mean(xc * xc, axis=-1, keepdims=True)
    return xc * lax.rsqrt(var + LN_EPS)


def _split_bf16(a):
    hi = a.astype(BF16)
    lo = (a - hi.astype(F32)).astype(BF16)
    return hi, lo


def _const_spec(shape):
    nd = len(shape)
    return pl.BlockSpec(shape, lambda *_: (0,) * nd, pipeline_mode=pl.Buffered(1))


def _params(semantics):
    return pltpu.CompilerParams(dimension_semantics=semantics, vmem_limit_bytes=VMEM_LIMIT)


def _ada_kernel(c_ref, w_ref, b_ref, o_ref):
    c = c_ref[...]
    a = c * jax.nn.sigmoid(c)
    a_hi, a_lo = _split_bf16(a)
    w_hi, w_lo = _split_bf16(w_ref[...])
    o_ref[...] = _dot(a_hi, w_hi) + _dot(a_lo, w_hi) + _dot(a_hi, w_lo) + b_ref[...]


def _ada(c, w_ada, b_ada):
    rows, d = c.shape
    n = w_ada.shape[1]
    tn = 1024
    return pl.pallas_call(
        _ada_kernel,
        grid=(n // tn,),
        in_specs=[pl.BlockSpec((rows, d), lambda j: (0, 0)),
                  pl.BlockSpec((d, tn), lambda j: (0, j)),
                  pl.BlockSpec((1, tn), lambda j: (0, j))],
        out_specs=pl.BlockSpec((rows, tn), lambda j: (0, j)),
        out_shape=jax.ShapeDtypeStruct((rows, n), F32),
        compiler_params=_params(("parallel",)),
        name="ada",
    )(c, w_ada, b_ada.reshape(1, n))


def _log_sigmoid(g):
    return jnp.minimum(g, 0.0) - jnp.log1p(jnp.exp(-jnp.abs(g)))


def _proj_kernel(x_ref, ada_ref, ct_ref, s1_ref, s2_ref,
                 wq_ref, wk_ref, wvt_ref, wmq_ref, wmk_ref, wmv_ref, wmo_ref,
                 wgr_ref, bgr_ref, wbg_ref,
                 q_ref, k_ref, vt_ref, mq_ref, mk_ref, mv_ref, og_ref, gc_ref, gr_ref, gab_ref):
    ts = x_ref.shape[1]
    ada = ada_ref[0]
    sh1 = ada[:, 0:D_MODEL]
    sc1 = ada[:, D_MODEL:2 * D_MODEL]
    h = (_norm(x_ref[0]) * (1.0 + sc1) + sh1).astype(BF16)

    ct = ct_ref[...]
    s1 = s1_ref[...]
    s2 = s2_ref[...]

    def rope_store(w_ref, o_ref, scale):
        a = _dot(h, w_ref[...])
        for j in range(W_AQ // LANES):
            blk = a[:, j * LANES:(j + 1) * LANES]
            r = (blk * ct + pltpu.roll(blk, LANES - ROT_DIM // 2, 1) * s1
                 + pltpu.roll(blk, ROT_DIM // 2, 1) * s2)
            o_ref[0, :, j * LANES:(j + 1) * LANES] = (r * scale).astype(o_ref.dtype)

    rope_store(wq_ref, q_ref, (DQK_A ** -0.5) * LOG2E)
    rope_store(wk_ref, k_ref, 1.0)
    vt_ref[0] = _dot_nt(wvt_ref[...], h).astype(vt_ref.dtype)
    mq_ref[0] = _dot(h, wmq_ref[...]).astype(mq_ref.dtype)
    mk_ref[0] = (_dot(h, wmk_ref[...]) * (DQK_M ** -0.5)).astype(mk_ref.dtype)
    mv_ref[0] = _dot(h, wmv_ref[...]).astype(mv_ref.dtype)
    og_ref[0] = jax.nn.sigmoid(_dot(h, wmo_ref[...])).astype(og_ref.dtype)
    gab_ref[0] = jax.nn.sigmoid(_dot(h, wbg_ref[...])).astype(gab_ref.dtype)

    r = lax.broadcasted_iota(jnp.int32, (ts, ts), 0)
    c = lax.broadcasted_iota(jnp.int32, (ts, ts), 1)
    shift = CHUNK.bit_length() - 1
    same = jnp.right_shift(r, shift) == jnp.right_shift(c, shift)
    lower = jnp.where(same & (c <= r), 1.0, 0.0).astype(BF16)
    upper = jnp.where(same & (c >= r), 1.0, 0.0).astype(BF16)

    g = _dot_nt(wgr_ref[...], h) + bgr_ref[...]
    ch = lax.broadcasted_iota(jnp.int32, g.shape, 0)
    is_f = (ch & (2 * N_HEADS_M - 1)) >= N_HEADS_M
    val = jnp.where(is_f, _log_sigmoid(g), g)
    f_hi, f_lo = _split_bf16(jnp.where(is_f & (ch < 2 * N_HEADS_M), val, 0.0))
    b_hi, b_lo = _split_bf16(jnp.where(is_f & (ch >= 2 * N_HEADS_M), val, 0.0))
    grow = (_dot(f_hi, upper) + _dot(f_lo, upper) + _dot(b_hi, lower) + _dot(b_lo, lower)
            + jnp.where(is_f, 0.0, val))
    gr_ref[0] = grow
    padded = jnp.concatenate([grow, jnp.zeros((LANES - W_MG, ts), F32)], axis=0)
    for j in range(ts // LANES):
        gc_ref[0, j * LANES:(j + 1) * LANES, :] = padded[:, j * LANES:(j + 1) * LANES].T[:, 0:W_MG]


def _proj(x, ada3, rope, wts, b_mgate, ts):
    B, S, D = x.shape
    ct, s1, s2 = rope
    wq, wk, wvt, wmq, wmk, wmv, wmo, wgr, wbg = wts
    bgr = b_mgate.reshape(W_MG, 1).astype(F32)
    tok = lambda n: pl.BlockSpec((1, ts, n), lambda b, i: (b, i, 0))
    tab = pl.BlockSpec((ts, LANES), lambda b, i: (i, 0))
    consts = (wq, wk, wvt, wmq, wmk, wmv, wmo, wgr, bgr, wbg)
    out_shape = (
        jax.ShapeDtypeStruct((B, S, W_AQ), BF16),
        jax.ShapeDtypeStruct((B, S, W_AK), BF16),
        jax.ShapeDtypeStruct((B, W_AV, S), BF16),
        jax.ShapeDtypeStruct((B, S, W_MQ), BF16),
        jax.ShapeDtypeStruct((B, S, W_MK), BF16),
        jax.ShapeDtypeStruct((B, S, W_MV), BF16),
        jax.ShapeDtypeStruct((B, S, W_MO), BF16),
        jax.ShapeDtypeStruct((B, S, W_MG), F32),
        jax.ShapeDtypeStruct((B, W_MG, S), F32),
        jax.ShapeDtypeStruct((B, S, W_BG), BF16),
    )
    out_specs = (
        tok(W_AQ), tok(W_AK),
        pl.BlockSpec((1, W_AV, ts), lambda b, i: (b, 0, i)),
        tok(W_MQ), tok(W_MK), tok(W_MV), tok(W_MO), tok(W_MG),
        pl.BlockSpec((1, W_MG, ts), lambda b, i: (b, 0, i)),
        tok(W_BG),
    )
    return pl.pallas_call(
        _proj_kernel,
        grid=(B, S // ts),
        in_specs=[tok(D), pl.BlockSpec((1, 1, 6 * D), lambda b, i: (b, 0, 0)), tab, tab, tab]
                 + [_const_spec(w.shape) for w in consts],
        out_specs=out_specs,
        out_shape=out_shape,
        compiler_params=_params(("parallel", "parallel")),
        name="proj",
    )(x, ada3, ct, s1, s2, *consts)


ONES_ROWS = 16


def _attn_kernel(lam_ref, q_ref, k_ref, vt_ref, g_ref, o_ref,
                 s_ref, p_ref, acc, m_ref, mx_ref, alpha_ref, *, tk, sub, lam_init):
    n = k_ref.shape[1] // tk
    nsub = tk // sub
    q = q_ref[0]
    lane = lax.broadcasted_iota(jnp.int32, q.shape, 1)
    zero = jnp.zeros_like(q)
    qs = (jnp.where(lane < DQK_A, q, zero), jnp.where(lane >= DQK_A, q, zero))
    ones_rows = jnp.ones((ONES_ROWS, sub), BF16)
    maps = range(2)

    acc[...] = jnp.zeros_like(acc)
    m_ref[...] = jnp.full_like(m_ref, -jnp.inf)

    def scores_sub(j, r, mx):
        kc = k_ref[0, pl.ds(pl.multiple_of(j * tk + r * sub, sub), sub), :]
        out = []
        for mp in maps:
            s = _dot_nt(kc, qs[mp])
            s_ref[mp, r * sub:(r + 1) * sub, :] = s
            cm = jnp.max(s, axis=0, keepdims=True)
            out.append(cm if mx is None else jnp.maximum(mx[mp], cm))
        return out

    def softmax_sub(r, m_new):
        for mp in maps:
            s = s_ref[mp, r * sub:(r + 1) * sub, :]
            p_ref[mp, r * sub:(r + 1) * sub, :] = jnp.exp2(s - m_new[mp]).astype(BF16)

    def values_sub(j, r, pv):
        vc = vt_ref[0, :, pl.ds(pl.multiple_of(j * tk + r * sub, sub), sub)]
        va = jnp.concatenate([vc, ones_rows], axis=0)
        out = []
        for mp in maps:
            d = _dot(va, p_ref[mp, r * sub:(r + 1) * sub, :])
            out.append(d if pv is None else pv[mp] + d)
        return out

    def accumulate(pv):
        for mp in maps:
            acc[mp] = alpha_ref[mp] * acc[mp] + pv[mp]

    def trip(j, first, last):
        m_old = [m_ref[mp] for mp in maps]
        m_new = [jnp.maximum(m_old[mp], mx_ref[mp]) for mp in maps]
        pv = mx = None
        for r in range(-1, nsub + 1):
            if not last and 0 <= r - 1:
                mx = scores_sub(j + 1, r - 1, mx)
            if not first and r + 1 < nsub:
                pv = values_sub(j - 1, r + 1, pv)
            if 0 <= r < nsub:
                softmax_sub(r, m_new)
        if not first:
            accumulate(pv)
        for mp in maps:
            alpha_ref[mp] = jnp.exp2(m_old[mp] - m_new[mp])
            m_ref[mp] = m_new[mp]
            if not last:
                mx_ref[mp] = mx[mp]

    mx = None
    for r in range(nsub):
        mx = scores_sub(0, r, mx)
    for mp in maps:
        mx_ref[mp] = mx[mp]
    trip(0, True, False)

    def body(j, carry):
        trip(j, False, False)
        return carry

    lax.fori_loop(1, n - 1, body, 0)
    trip(n - 1, False, True)
    pv = None
    for r in range(nsub):
        pv = values_sub(n - 1, r, pv)
    accumulate(pv)

    lq = lam_ref[...]
    la = jnp.sum(lq[0:1] * lq[1:2], axis=1, keepdims=True)
    lb = jnp.sum(lq[2:3] * lq[3:4], axis=1, keepdims=True)
    lam = jnp.exp(la) - jnp.exp(lb) + lam_init
    a1 = acc[0]
    a2 = acc[1]
    o = a1[0:DV_A] / a1[DV_A:DV_A + 1] - lam * (a2[0:DV_A] / a2[DV_A:DV_A + 1])
    ms = jnp.mean(o * o, axis=0, keepdims=True)
    y = o * lax.rsqrt(ms + LN_EPS) * g_ref[...] * (1.0 - lam_init)
    o_ref[0] = y.T.astype(o_ref.dtype)


def _attn(q, k, vt, lambda_qk, subln_g, lam_init, tq, tk, sub):
    B, S, _ = q.shape
    assert S // tk >= 2 and tk % sub == 0, (S, tk, sub)
    kern = functools.partial(_attn_kernel, tk=tk, sub=sub, lam_init=lam_init)
    return pl.pallas_call(
        kern,
        grid=(B, N_HEADS_A, S // tq),
        in_specs=[pl.BlockSpec((4, DQK_A), lambda b, h, i: (0, 0)),
                  pl.BlockSpec((1, tq, DV_A), lambda b, h, i: (b, i, h)),
                  pl.BlockSpec((1, S, DV_A), lambda b, h, i: (b, 0, h)),
                  pl.BlockSpec((1, DV_A, S), lambda b, h, i: (b, h, 0)),
                  pl.BlockSpec((DV_A, 1), lambda b, h, i: (0, 0))],
        out_specs=pl.BlockSpec((1, tq, DV_A), lambda b, h, i: (b, i, h)),
        out_shape=jax.ShapeDtypeStruct((B, S, N_HEADS_A * DV_A), BF16),
        scratch_shapes=[pltpu.VMEM((2, tk, tq), F32), pltpu.VMEM((2, tk, tq), BF16),
                        pltpu.VMEM((2, DV_A + ONES_ROWS, tq), F32),
                        pltpu.VMEM((2, 1, tq), F32), pltpu.VMEM((2, 1, tq), F32),
                        pltpu.VMEM((2, 1, tq), F32)],
        compiler_params=_params(("parallel", "parallel", "parallel")),
        name="attn",
    )(lambda_qk.astype(F32), q, k, vt, subln_g.reshape(DV_A, 1).astype(F32))


N_STREAMS = 2 * N_HEADS_M
MLSTM_GROUP = 4
DV_AUG = DV_M + LANES


def _mlstm_kernel(qf_ref, kf_ref, vf_ref, gcf_ref, grf_ref,
                  qb_ref, kb_ref, vb_ref, gcb_ref, grb_ref,
                  hf_ref, hb_ref, ct_ref, m_ref):
    L = CHUNK

    @pl.when(pl.program_id(1) == 0)
    def _():
        ct_ref[...] = jnp.zeros_like(ct_ref)
        m_ref[...] = jnp.zeros_like(m_ref)

    row = lax.broadcasted_iota(jnp.int32, (L, L), 0)
    col = lax.broadcasted_iota(jnp.int32, (L, L), 1)
    assert L == LANES
    ones_blk = jnp.ones((L, LANES), BF16)

    def lane_tile(a, width):
        return jnp.concatenate([a] * (width // LANES), axis=1)

    dirs = ((qf_ref, kf_ref, vf_ref, gcf_ref, grf_ref, hf_ref),
            (qb_ref, kb_ref, vb_ref, gcb_ref, grb_ref, hb_ref))
    m_all = m_ref[...]
    m_out = []
    nb = qf_ref.shape[0]
    streams = [(bb, d, hd) for bb in range(nb) for d in range(2) for hd in range(N_HEADS_M)]

    def operands(bb, d, hd):
        q_ref, k_ref, v_ref = dirs[d][0:3]
        qh = q_ref[bb, :, hd * DQK_M:(hd + 1) * DQK_M]
        kh = k_ref[bb, :, hd * DQK_M:(hd + 1) * DQK_M]
        va = jnp.concatenate([v_ref[bb, :, hd * DV_M:(hd + 1) * DV_M], ones_blk], axis=1)
        return qh, kh, va

    for g0 in range(0, len(streams), MLSTM_GROUP):
        group = list(enumerate(streams))[g0:g0 + MLSTM_GROUP]
        gate = {}
        for st, (bb, d, hd) in group:
            tri = (col <= row) if d == 0 else (col >= row)
            e = L - 1 if d == 0 else 0
            gc = dirs[d][3][bb]
            gr = dirs[d][4][bb]
            ci = d * 2 * N_HEADS_M + hd
            cf = ci + N_HEADS_M
            i_row = gr[ci:ci + 1, :]
            b_row = gr[cf:cf + 1, :]
            b_t = jnp.broadcast_to(gc[:, cf:cf + 1], (L, LANES))
            m_prev = m_all[st:st + 1, :]
            dm = jnp.where(tri, b_t - b_row + i_row, -jnp.inf)
            inter = b_t + m_prev
            m_t = jnp.maximum(inter, jnp.broadcast_to(jnp.max(dm, axis=1, keepdims=True), (L, LANES)))
            w = jnp.exp(dm - m_t)
            w_inter = jnp.exp(inter - m_t)
            b_end = b_row[:, e:e + 1]
            m_new = m_t[e:e + 1, :]
            ws = jnp.exp(b_end - b_row + i_row - m_new)
            decay = jnp.exp(b_end + m_prev - m_new)
            gate[st] = (w, w_inter, jnp.exp(-m_t), ws, decay)
            m_out.append(m_new)
        sqk = {}
        for st, sid in group:
            qh, kh, _ = operands(*sid)
            sqk[st] = (_dot_nt(qh, kh) * gate[st][0]).astype(BF16)
        for st, (bb, d, hd) in group:
            qh, _, va = operands(bb, d, hd)
            _, w_inter, floor, _, _ = gate[st]
            numa = _dot(sqk[st], va) + lane_tile(w_inter, DV_AUG) * _dot(qh, ct_ref[st].astype(BF16))
            den = numa[:, DV_M:DV_AUG]
            inv = 1.0 / jnp.maximum(jnp.abs(den), floor)
            h = numa[:, 0:DV_M] * lane_tile(inv, DV_M)
            dirs[d][5][bb, :, hd * DV_M:(hd + 1) * DV_M] = h.astype(BF16)
        for st, sid in group:
            _, kh, va = operands(*sid)
            _, _, _, ws, decay = gate[st]
            kst = (kh.astype(F32).T * ws).astype(BF16)
            ct_ref[st] = lane_tile(decay, DV_AUG) * ct_ref[st] + _dot(kst, va)
    m_ref[...] = jnp.concatenate(m_out, axis=0)


def _mlstm(mq, mk, mv, gcol, grow):
    B, S, _ = mq.shape
    nc = S // CHUNK
    nb = 2 if B % 2 == 0 else 1
    fwd = lambda n: pl.BlockSpec((nb, CHUNK, n), lambda b, c: (b, c, 0))
    bwd = lambda n: pl.BlockSpec((nb, CHUNK, n), lambda b, c: (b, nc - 1 - c, 0))
    rowf = pl.BlockSpec((nb, W_MG, CHUNK), lambda b, c: (b, 0, c))
    rowb = pl.BlockSpec((nb, W_MG, CHUNK), lambda b, c: (b, 0, nc - 1 - c))
    return pl.pallas_call(
        _mlstm_kernel,
        grid=(B // nb, nc),
        in_specs=[fwd(W_MQ), fwd(W_MK), fwd(W_MV), fwd(W_MG), rowf,
                  bwd(W_MQ), bwd(W_MK), bwd(W_MV), bwd(W_MG), rowb],
        out_specs=(fwd(W_MV), bwd(W_MV)),
        out_shape=(jax.ShapeDtypeStruct((B, S, W_MV), BF16),
                   jax.ShapeDtypeStruct((B, S, W_MV), BF16)),
        scratch_shapes=[pltpu.VMEM((nb * N_STREAMS, DQK_M, DV_AUG), F32),
                        pltpu.VMEM((nb * N_STREAMS, LANES), F32)],
        compiler_params=_params(("parallel", "arbitrary")),
        name="mlstm",
    )(mq, mk, mv, gcol, grow, mq, mk, mv, gcol, grow)


def _mix_kernel(x_ref, ada_ref, ya_ref, hf_ref, hb_ref, og_ref, gab_ref,
                wpa_ref, wpm_ref, wout_ref, mng_ref, lg_ref, lb_ref, o_ref, *, alpha):
    hs = hf_ref[0].astype(F32) + hb_ref[0].astype(F32)
    hn = jnp.concatenate(
        [_norm(hs[:, hd * DV_M:(hd + 1) * DV_M]) for hd in range(N_HEADS_M)], axis=1)
    ym = (og_ref[0].astype(F32) * (hn * mng_ref[...])).astype(BF16)
    gab = gab_ref[0].astype(F32)
    pre = gab[:, 0:D_MODEL] * _dot(ya_ref[0], wpa_ref[...]) + gab[:, D_MODEL:] * _dot(ym, wpm_ref[...])
    mix = _dot(pre.astype(BF16), wout_ref[...])
    g1 = ada_ref[0][:, 2 * D_MODEL:3 * D_MODEL]
    o_ref[0] = _norm(alpha * x_ref[0] + g1 * mix) * lg_ref[...] + lb_ref[...]


def _mix(x, ada3, ya, hf, hb, og, gab, wpa, wpm, wout, mnorm_g, ln_g, ln_b, alpha, ts):
    B, S, D = x.shape
    tok = lambda n: pl.BlockSpec((1, ts, n), lambda b, i: (b, i, 0))
    consts = (wpa, wpm, wout, mnorm_g.reshape(1, W_MV).astype(F32),
              ln_g.reshape(1, D).astype(F32), ln_b.reshape(1, D).astype(F32))
    return pl.pallas_call(
        functools.partial(_mix_kernel, alpha=alpha),
        grid=(B, S // ts),
        in_specs=[tok(D), pl.BlockSpec((1, 1, 6 * D), lambda b, i: (b, 0, 0)),
                  tok(W_AV), tok(W_MV), tok(W_MV), tok(W_MO), tok(W_BG)]
                 + [_const_spec(w.shape) for w in consts],
        out_specs=tok(D),
        out_shape=jax.ShapeDtypeStruct((B, S, D), F32),
        compiler_params=_params(("parallel", "parallel")),
        name="mix",
    )(x, ada3, ya, hf, hb, og, gab, *consts)


def _gelu_tanh(x):
    return 0.5 * x * (1.0 + jnp.tanh(math.sqrt(2.0 / math.pi) * (x + 0.044715 * (x * x * x))))


def _ffn_kernel(xp_ref, x_ref, xn_ref, ada_ref, wug_ref, wuv_ref, cw_ref, cb_ref, wd_ref,
                lg_ref, lb_ref, o_ref, *, alpha):
    ts = x_ref.shape[1]
    H = SUBLANES
    i = pl.program_id(1)
    ada = ada_ref[0]
    sh2 = ada[:, 3 * D_MODEL:4 * D_MODEL]
    sc2 = ada[:, 4 * D_MODEL:5 * D_MODEL]
    g2 = ada[:, 5 * D_MODEL:6 * D_MODEL]
    x = x_ref[0]

    def modulated(v):
        return _norm(v) * (1.0 + sc2) + sh2

    hp = jnp.where(i == 0, 0.0, modulated(xp_ref[0]))
    hm = modulated(x)
    hn = jnp.where(i == pl.num_programs(1) - 1, 0.0, modulated(xn_ref[0]))
    ug = _dot(jnp.concatenate([hp, hm, hn], axis=0).astype(BF16), wug_ref[...])
    uv = _dot(hm.astype(BF16), wuv_ref[...])
    n = ts + 2 * H
    prev = pltpu.roll(ug, 1, 0)[H:H + ts]
    nxt = pltpu.roll(ug, n - 1, 0)[H:H + ts]
    cw = cw_ref[...]
    conv = prev * cw[0:1] + ug[H:H + ts] * cw[1:2] + nxt * cw[2:3] + cb_ref[...]
    y = _dot((_gelu_tanh(conv) * uv).astype(BF16), wd_ref[...])
    o_ref[0] = _norm(alpha * x + g2 * y) * lg_ref[...] + lb_ref[...]


def _ffn(x, ada3, wug, wuv, conv_w, conv_b, wd, ln_g, ln_b, alpha, ts):
    B, S, D = x.shape
    H = SUBLANES
    nb = ts // H
    last = S // H - 1
    tok = pl.BlockSpec((1, ts, D), lambda b, i: (b, i, 0))
    prev = pl.BlockSpec((1, H, D), lambda b, i: (b, jnp.maximum(i * nb - 1, 0), 0))
    nxt = pl.BlockSpec((1, H, D), lambda b, i: (b, jnp.minimum((i + 1) * nb, last), 0))
    consts = (wug, wuv, conv_w.astype(F32), conv_b.reshape(1, D_FF).astype(F32), wd,
              ln_g.reshape(1, D).astype(F32), ln_b.reshape(1, D).astype(F32))
    return pl.pallas_call(
        functools.partial(_ffn_kernel, alpha=alpha),
        grid=(B, S // ts),
        in_specs=[prev, tok, nxt, pl.BlockSpec((1, 1, 6 * D), lambda b, i: (b, 0, 0))]
                 + [_const_spec(w.shape) for w in consts],
        out_specs=tok,
        out_shape=jax.ShapeDtypeStruct((B, S, D), F32),
        compiler_params=_params(("parallel", "parallel")),
        name="ffn",
    )(x, x, x, ada3, *consts)


def _rope_tables(seq):
    half = ROT_DIM // 2
    inv = ROPE_THETA ** (-jnp.arange(0, ROT_DIM, 2, dtype=F32) / ROT_DIM)
    ang = jnp.arange(seq, dtype=F32)[:, None] * inv[None, :]
    cos, sin = jnp.cos(ang), jnp.sin(ang)
    rest = DQK_A - ROT_DIM
    one = jnp.ones((seq, rest), F32)
    zr = jnp.zeros((seq, rest), F32)
    zh = jnp.zeros((seq, half), F32)
    reps = LANES // DQK_A
    ct = jnp.tile(jnp.concatenate([cos, cos, one], axis=1), (1, reps))
    s1 = jnp.tile(jnp.concatenate([-sin, zh, zr], axis=1), (1, reps))
    s2 = jnp.tile(jnp.concatenate([zh, sin, zr], axis=1), (1, reps))
    return ct, s1, s2


def _layer_weights(w_in, w_pa, w_pm, w_out, w_up, w_down):
    seg = lambda j: w_in[:, SEG_STARTS[j]:SEG_STARTS[j] + SEG_WIDTHS[j]]
    proj = (seg(0).astype(BF16), seg(1).astype(BF16), seg(2).T.astype(BF16),
            seg(3).astype(BF16), seg(4).astype(BF16), seg(5).astype(BF16), seg(6).astype(BF16),
            seg(7).T.astype(BF16), seg(8).astype(BF16))
    return (proj, w_pa.astype(BF16), w_pm.astype(BF16), w_out.astype(BF16),
            w_up[:, :D_FF].astype(BF16), w_up[:, D_FF:].astype(BF16), w_down.astype(BF16))


def _tile(seq, want):
    t = min(seq, want)
    assert seq % t == 0, (seq, want)
    return t


def _default_tiles(seq):
    return (512, 1024, min(2048, max(seq // 4, 256)), 256)


def _encoder_layer(x, ada, l, depth, weights, lambda_qk, subln_g, b_mgate, mnorm_g,
                   ln1_g, ln1_b, conv_w, conv_b, ln2_g, ln2_b, tiles):
    B, S, D = x.shape
    ts, tq, tk, sub = (_tile(S, t) for t in (tiles or _default_tiles(S)))
    alpha = (2.0 * depth) ** 0.25
    lam_init = 0.8 - 0.6 * math.exp(-0.3 * l)
    proj_w, wpa, wpm, wout, wug, wuv, wd = weights
    ada3 = ada.reshape(B, 1, 6 * D)
    q, k, vt, mq, mk, mv, og, gcol, grow, gab = _proj(x, ada3, _rope_tables(S), proj_w, b_mgate, ts)
    ya = _attn(q, k, vt, lambda_qk, subln_g, lam_init, tq, tk, sub)
    hf, hb = _mlstm(mq, mk, mv, gcol, grow)
    x1 = _mix(x, ada3, ya, hf, hb, og, gab, wpa, wpm, wout, mnorm_g, ln1_g, ln1_b, alpha, ts)
    return _ffn(x1, ada3, wug, wuv, conv_w, conv_b, wd, ln2_g, ln2_b, alpha, ts)


def _trunk(xs, cs, w_ada, b_ada, w_in, lambda_qk, subln_g, b_mgate, mnorm_g, w_pa, w_pm, w_out,
           ln1_g, ln1_b, w_up, conv_w, conv_b, w_down, ln2_g, ln2_b, tiles=None):
    depth = w_in.shape[0]
    rows = [c.shape[0] for c in cs]
    c_all = jnp.concatenate(cs, axis=0)
    xs = list(xs)
    for l in range(depth):
        ada_all = _ada(c_all, w_ada[l], b_ada[l])
        weights = _layer_weights(w_in[l], w_pa[l], w_pm[l], w_out[l], w_up[l], w_down[l])
        start = 0
        for g, n in enumerate(rows):
            xs[g] = _encoder_layer(xs[g], ada_all[start:start + n], l, depth, weights,
                                   lambda_qk[l], subln_g[l], b_mgate[l], mnorm_g[l],
                                   ln1_g[l], ln1_b[l], conv_w[l], conv_b[l], ln2_g[l], ln2_b[l], tiles)
            start += n
    return tuple(xs)


def kernel(x_prompt, x_sample, c_prompt, c_sample, w_ada, b_ada, w_in, lambda_qk, subln_g, b_mgate, mnorm_g, w_pa, w_pm, w_out, ln1_g, ln1_b, w_up, conv_w, conv_b, w_down, ln2_g, ln2_b):
    return _trunk((x_prompt, x_sample), (c_prompt, c_sample), w_ada, b_ada, w_in, lambda_qk, subln_g,
                  b_mgate, mnorm_g, w_pa, w_pm, w_out, ln1_g, ln1_b, w_up, conv_w, conv_b, w_down,
                  ln2_g, ln2_b)
```

```python
import functools
import math

import jax
import jax.numpy as jnp
import numpy as np
from jax import lax
from jax.experimental import pallas as pl
from jax.experimental.pallas import tpu as pltpu

F32 = jnp.float32
BF16 = jnp.bfloat16

D_MODEL = 1024
N_HEADS_A = 8
DQK_A = 64
DV_A = 2 * DQK_A
ROT_DIM = DQK_A // 4
ROPE_THETA = 500000.0
N_HEADS_M = 4
DQK_M = 128
DV_M = 256
CHUNK = 128
D_FF = 2816
LN_EPS = 1e-5

W_AQ = N_HEADS_A * 2 * DQK_A
W_AK = N_HEADS_A * 2 * DQK_A
W_AV = N_HEADS_A * DV_A
W_MQ = N_HEADS_M * DQK_M
W_MK = N_HEADS_M * DQK_M
W_MV = N_HEADS_M * DV_M
W_MO = N_HEADS_M * DV_M
W_MG = 2 * 2 * N_HEADS_M
W_BG = 2 * D_MODEL
SEG_WIDTHS = (W_AQ, W_AK, W_AV, W_MQ, W_MK, W_MV, W_MO, W_MG, W_BG)
SEG_STARTS = [0] + [int(s) for s in np.cumsum(SEG_WIDTHS)[:-1]]

LANES = 128
SUBLANES = 8
LOG2E = 1.4426950408889634
VMEM_LIMIT = 56 * 1024 * 1024

NT_DIMS = (((1,), (1,)), ((), ()))


def _dot(a, b):
    return jnp.dot(a, b, preferred_element_type=F32)


def _dot_nt(a, b):
    return lax.dot_general(a, b, NT_DIMS, preferred_element_type=F32)


def _norm(x):
    mu = jnp.mean(x, axis=-1, keepdims=True)
    xc = x - mu
    var = jnp.mean(xc * xc, axis=-1, keepdims=True)
    return xc * lax.rsqrt(var + LN_EPS)


def _split_bf16(a):
    hi = a.astype(BF16)
    lo = (a - hi.astype(F32)).astype(BF16)
    return hi, lo


def _const_spec(shape):
    nd = len(shape)
    return pl.BlockSpec(shape, lambda *_: (0,) * nd, pipeline_mode=pl.Buffered(1))


def _params(semantics):
    return pltpu.CompilerParams(dimension_semantics=semantics, vmem_limit_bytes=VMEM_LIMIT)


def _ada_kernel(c_ref, w_ref, b_ref, o_ref):
    c = c_ref[...]
    a = c * jax.nn.sigmoid(c)
    a_hi, a_lo = _split_bf16(a)
    w_hi, w_lo = _split_bf16(w_ref[...])
    o_ref[...] = _dot(a_hi, w_hi) + _dot(a_lo, w_hi) + _dot(a_hi, w_lo) + b_ref[...]


def _ada(c, w_ada, b_ada):
    rows, d = c.shape
    n = w_ada.shape[1]
    tn = 1024
    return pl.pallas_call(
        _ada_kernel,
        grid=(n // tn,),
        in_specs=[pl.BlockSpec((rows, d), lambda j: (0, 0)),
                  pl.BlockSpec((d, tn), lambda j: (0, j)),
                  pl.BlockSpec((1, tn), lambda j: (0, j))],
        out_specs=pl.BlockSpec((rows, tn), lambda j: (0, j)),
        out_shape=jax.ShapeDtypeStruct((rows, n), F32),
        compiler_params=_params(("parallel",)),
        name="ada",
    )(c, w_ada, b_ada.reshape(1, n))


def _log_sigmoid(g):
    return jnp.minimum(g, 0.0) - jnp.log1p(jnp.exp(-jnp.abs(g)))


def _proj_kernel(x_ref, ada_ref, ct_ref, s1_ref, s2_ref,
                 wq_ref, wk_ref, wvt_ref, wmq_ref, wmk_ref, wmv_ref, wmo_ref,
                 wgr_ref, bgr_ref, wbg_ref,
                 q_ref, k_ref, vt_ref, mq_ref, mk_ref, mv_ref, og_ref, gc_ref, gr_ref, gab_ref):
    ts = x_ref.shape[1]
    ada = ada_ref[0]
    sh1 = ada[:, 0:D_MODEL]
    sc1 = ada[:, D_MODEL:2 * D_MODEL]
    h = (_norm(x_ref[0]) * (1.0 + sc1) + sh1).astype(BF16)

    ct = ct_ref[...]
    s1 = s1_ref[...]
    s2 = s2_ref[...]

    def rope_store(w_ref, o_ref, scale):
        a = _dot(h, w_ref[...])
        for j in range(W_AQ // LANES):
            blk = a[:, j * LANES:(j + 1) * LANES]
            r = (blk * ct + pltpu.roll(blk, LANES - ROT_DIM // 2, 1) * s1
                 + pltpu.roll(blk, ROT_DIM // 2, 1) * s2)
            o_ref[0, :, j * LANES:(j + 1) * LANES] = (r * scale).astype(o_ref.dtype)

    rope_store(wq_ref, q_ref, (DQK_A ** -0.5) * LOG2E)
    rope_store(wk_ref, k_ref, 1.0)
    vt_ref[0] = _dot_nt(wvt_ref[...], h).astype(vt_ref.dtype)
    mq_ref[0] = _dot(h, wmq_ref[...]).astype(mq_ref.dtype)
    mk_ref[0] = (_dot(h, wmk_ref[...]) * (DQK_M ** -0.5)).astype(mk_ref.dtype)
    mv_ref[0] = _dot(h, wmv_ref[...]).astype(mv_ref.dtype)
    og_ref[0] = jax.nn.sigmoid(_dot(h, wmo_ref[...])).astype(og_ref.dtype)
    gab_ref[0] = jax.nn.sigmoid(_dot(h, wbg_ref[...])).astype(gab_ref.dtype)

    r = lax.broadcasted_iota(jnp.int32, (ts, ts), 0)
    c = lax.broadcasted_iota(jnp.int32, (ts, ts), 1)
    shift = CHUNK.bit_length() - 1
    same = jnp.right_shift(r, shift) == jnp.right_shift(c, shift)
    lower = jnp.where(same & (c <= r), 1.0, 0.0).astype(BF16)
    upper = jnp.where(same & (c >= r), 1.0, 0.0).astype(BF16)

    g = _dot_nt(wgr_ref[...], h) + bgr_ref[...]
    ch = lax.broadcasted_iota(jnp.int32, g.shape, 0)
    is_f = (ch & (2 * N_HEADS_M - 1)) >= N_HEADS_M
    val = jnp.where(is_f, _log_sigmoid(g), g)
    f_hi, f_lo = _split_bf16(jnp.where(is_f & (ch < 2 * N_HEADS_M), val, 0.0))
    b_hi, b_lo = _split_bf16(jnp.where(is_f & (ch >= 2 * N_HEADS_M), val, 0.0))
    grow = (_dot(f_hi, upper) + _dot(f_lo, upper) + _dot(b_hi, lower) + _dot(b_lo, lower)
            + jnp.where(is_f, 0.0, val))
    gr_ref[0] = grow
    padded = jnp.concatenate([grow, jnp.zeros((LANES - W_MG, ts), F32)], axis=0)
    for j in range(ts // LANES):
        gc_ref[0, j * LANES:(j + 1) * LANES, :] = padded[:, j * LANES:(j + 1) * LANES].T[:, 0:W_MG]


def _proj(x, ada3, rope, wts, b_mgate, ts):
    B, S, D = x.shape
    ct, s1, s2 = rope
    wq, wk, wvt, wmq, wmk, wmv, wmo, wgr, wbg = wts
    bgr = b_mgate.reshape(W_MG, 1).astype(F32)
    tok = lambda n: pl.BlockSpec((1, ts, n), lambda b, i: (b, i, 0))
    tab = pl.BlockSpec((ts, LANES), lambda b, i: (i, 0))
    consts = (wq, wk, wvt, wmq, wmk, wmv, wmo, wgr, bgr, wbg)
    out_shape = (
        jax.ShapeDtypeStruct((B, S, W_AQ), BF16),
        jax.ShapeDtypeStruct((B, S, W_AK), BF16),
        jax.ShapeDtypeStruct((B, W_AV, S), BF16),
        jax.ShapeDtypeStruct((B, S, W_MQ), BF16),
        jax.ShapeDtypeStruct((B, S, W_MK), BF16),
        jax.ShapeDtypeStruct((B, S, W_MV), BF16),
        jax.ShapeDtypeStruct((B, S, W_MO), BF16),
        jax.ShapeDtypeStruct((B, S, W_MG), F32),
        jax.ShapeDtypeStruct((B, W_MG, S), F32),
        jax.ShapeDtypeStruct((B, S, W_BG), BF16),
    )
    out_specs = (
        tok(W_AQ), tok(W_AK),
        pl.BlockSpec((1, W_AV, ts), lambda b, i: (b, 0, i)),
        tok(W_MQ), tok(W_MK), tok(W_MV), tok(W_MO), tok(W_MG),
        pl.BlockSpec((1, W_MG, ts), lambda b, i: (b, 0, i)),
        tok(W_BG),
    )
    return pl.pallas_call(
        _proj_kernel,
        grid=(B, S // ts),
        in_specs=[tok(D), pl.BlockSpec((1, 1, 6 * D), lambda b, i: (b, 0, 0)), tab, tab, tab]
                 + [_const_spec(w.shape) for w in consts],
        out_specs=out_specs,
        out_shape=out_shape,
        compiler_params=_params(("parallel", "parallel")),
        name="proj",
    )(x, ada3, ct, s1, s2, *consts)


ONES_ROWS = 16


def _attn_kernel(lam_ref, q_ref, k_ref, vt_ref, g_ref, o_ref,
                 s_ref, p_ref, acc, m_ref, mx_ref, alpha_ref, *, tk, sub, lam_init):
    n = k_ref.shape[1] // tk
    nsub = tk // sub
    q = q_ref[0]
    lane = lax.broadcasted_iota(jnp.int32, q.shape, 1)
    zero = jnp.zeros_like(q)
    qs = (jnp.where(lane < DQK_A, q, zero), jnp.where(lane >= DQK_A, q, zero))
    ones_rows = jnp.ones((ONES_ROWS, sub), BF16)
    maps = range(2)

    acc[...] = jnp.zeros_like(acc)
    m_ref[...] = jnp.full_like(m_ref, -jnp.inf)

    def scores_sub(j, r, mx):
        kc = k_ref[0, pl.ds(pl.multiple_of(j * tk + r * sub, sub), sub), :]
        out = []
        for mp in maps:
            s = _dot_nt(kc, qs[mp])
            s_ref[mp, r * sub:(r + 1) * sub, :] = s
            cm = jnp.max(s, axis=0, keepdims=True)
            out.append(cm if mx is None else jnp.maximum(mx[mp], cm))
        return out

    def softmax_sub(r, m_new):
        for mp in maps:
            s = s_ref[mp, r * sub:(r + 1) * sub, :]
            p_ref[mp, r * sub:(r + 1) * sub, :] = jnp.exp2(s - m_new[mp]).astype(BF16)

    def values_sub(j, r, pv):
        vc = vt_ref[0, :, pl.ds(pl.multiple_of(j * tk + r * sub, sub), sub)]
        va = jnp.concatenate([vc, ones_rows], axis=0)
        out = []
        for mp in maps:
            d = _dot(va, p_ref[mp, r * sub:(r + 1) * sub, :])
            out.append(d if pv is None else pv[mp] + d)
        return out

    def accumulate(pv):
        for mp in maps:
            acc[mp] = alpha_ref[mp] * acc[mp] + pv[mp]

    def trip(j, first, last):
        m_old = [m_ref[mp] for mp in maps]
        m_new = [jnp.maximum(m_old[mp], mx_ref[mp]) for mp in maps]
        pv = mx = None
        for r in range(-1, nsub + 1):
            if not last and 0 <= r - 1:
                mx = scores_sub(j + 1, r - 1, mx)
            if not first and r + 1 < nsub:
                pv = values_sub(j - 1, r + 1, pv)
            if 0 <= r < nsub:
                softmax_sub(r, m_new)
        if not first:
            accumulate(pv)
        for mp in maps:
            alpha_ref[mp] = jnp.exp2(m_old[mp] - m_new[mp])
            m_ref[mp] = m_new[mp]
            if not last:
                mx_ref[mp] = mx[mp]

    mx = None
    for r in range(nsub):
        mx = scores_sub(0, r, mx)
    for mp in maps:
        mx_ref[mp] = mx[mp]
    trip(0, True, False)

    def body(j, carry):
        trip(j, False, False)
        return carry

    lax.fori_loop(1, n - 1, body, 0)
    trip(n - 1, False, True)
    pv = None
    for r in range(nsub):
        pv = values_sub(n - 1, r, pv)
    accumulate(pv)

    lq = lam_ref[...]
    la = jnp.sum(lq[0:1] * lq[1:2], axis=1, keepdims=True)
    lb = jnp.sum(lq[2:3] * lq[3:4], axis=1, keepdims=True)
    lam = jnp.exp(la) - jnp.exp(lb) + lam_init
    a1 = acc[0]
    a2 = acc[1]
    o = a1[0:DV_A] / a1[DV_A:DV_A + 1] - lam * (a2[0:DV_A] / a2[DV_A:DV_A + 1])
    ms = jnp.mean(o * o, axis=0, keepdims=True)
    y = o * lax.rsqrt(ms + LN_EPS) * g_ref[...] * (1.0 - lam_init)
    o_ref[0] = y.T.astype(o_ref.dtype)


def _attn(q, k, vt, lambda_qk, subln_g, lam_init, tq, tk, sub):
    B, S, _ = q.shape
    assert S // tk >= 2 and tk % sub == 0, (S, tk, sub)
    kern = functools.partial(_attn_kernel, tk=tk, sub=sub, lam_init=lam_init)
    return pl.pallas_call(
        kern,
        grid=(B, N_HEADS_A, S // tq),
        in_specs=[pl.BlockSpec((4, DQK_A), lambda b, h, i: (0, 0)),
                  pl.BlockSpec((1, tq, DV_A), lambda b, h, i: (b, i, h)),
                  pl.BlockSpec((1, S, DV_A), lambda b, h, i: (b, 0, h)),
                  pl.BlockSpec((1, DV_A, S), lambda b, h, i: (b, h, 0)),
                  pl.BlockSpec((DV_A, 1), lambda b, h, i: (0, 0))],
        out_specs=pl.BlockSpec((1, tq, DV_A), lambda b, h, i: (b, i, h)),
        out_shape=jax.ShapeDtypeStruct((B, S, N_HEADS_A * DV_A), BF16),
        scratch_shapes=[pltpu.VMEM((2, tk, tq), F32), pltpu.VMEM((2, tk, tq), BF16),
                        pltpu.VMEM((2, DV_A + ONES_ROWS, tq), F32),
                        pltpu.VMEM((2, 1, tq), F32), pltpu.VMEM((2, 1, tq), F32),
                        pltpu.VMEM((2, 1, tq), F32)],
        compiler_params=_params(("parallel", "parallel", "parallel")),
        name="attn",
    )(lambda_qk.astype(F32), q, k, vt, subln_g.reshape(DV_A, 1).astype(F32))


N_STREAMS = 2 * N_HEADS_M
MLSTM_GROUP = 4
DV_AUG = DV_M + LANES


def _mlstm_kernel(qf_ref, kf_ref, vf_ref, gcf_ref, grf_ref,
                  qb_ref, kb_ref, vb_ref, gcb_ref, grb_ref,
                  hf_ref, hb_ref, ct_ref, m_ref):
    L = CHUNK

    @pl.when(pl.program_id(1) == 0)
    def _():
        ct_ref[...] = jnp.zeros_like(ct_ref)
        m_ref[...] = jnp.zeros_like(m_ref)

    row = lax.broadcasted_iota(jnp.int32, (L, L), 0)
    col = lax.broadcasted_iota(jnp.int32, (L, L), 1)
    assert L == LANES
    ones_blk = jnp.ones((L, LANES), BF16)

    def lane_tile(a, width):
        return jnp.concatenate([a] * (width // LANES), axis=1)

    dirs = ((qf_ref, kf_ref, vf_ref, gcf_ref, grf_ref, hf_ref),
            (qb_ref, kb_ref, vb_ref, gcb_ref, grb_ref, hb_ref))
    m_all = m_ref[...]
    m_out = []
    nb = qf_ref.shape[0]
    streams = [(bb, d, hd) for bb in range(nb) for d in range(2) for hd in range(N_HEADS_M)]

    def operands(bb, d, hd):
        q_ref, k_ref, v_ref = dirs[d][0:3]
        qh = q_ref[bb, :, hd * DQK_M:(hd + 1) * DQK_M]
        kh = k_ref[bb, :, hd * DQK_M:(hd + 1) * DQK_M]
        va = jnp.concatenate([v_ref[bb, :, hd * DV_M:(hd + 1) * DV_M], ones_blk], axis=1)
        return qh, kh, va

    for g0 in range(0, len(streams), MLSTM_GROUP):
        group = list(enumerate(streams))[g0:g0 + MLSTM_GROUP]
        gate = {}
        for st, (bb, d, hd) in group:
            tri = (col <= row) if d == 0 else (col >= row)
            e = L - 1 if d == 0 else 0
            gc = dirs[d][3][bb]
            gr = dirs[d][4][bb]
            ci = d * 2 * N_HEADS_M + hd
            cf = ci + N_HEADS_M
            i_row = gr[ci:ci + 1, :]
            b_row = gr[cf:cf + 1, :]
            b_t = jnp.broadcast_to(gc[:, cf:cf + 1], (L, LANES))
            m_prev = m_all[st:st + 1, :]
            dm = jnp.where(tri, b_t - b_row + i_row, -jnp.inf)
            inter = b_t + m_prev
            m_t = jnp.maximum(inter, jnp.broadcast_to(jnp.max(dm, axis=1, keepdims=True), (L, LANES)))
            w = jnp.exp(dm - m_t)
            w_inter = jnp.exp(inter - m_t)
            b_end = b_row[:, e:e + 1]
            m_new = m_t[e:e + 1, :]
            ws = jnp.exp(b_end - b_row + i_row - m_new)
            decay = jnp.exp(b_end + m_prev - m_new)
            gate[st] = (w, w_inter, jnp.exp(-m_t), ws, decay)
            m_out.append(m_new)
        sqk = {}
        for st, sid in group:
            qh, kh, _ = operands(*sid)
            sqk[st] = (_dot_nt(qh, kh) * gate[st][0]).astype(BF16)
        for st, (bb, d, hd) in group:
            qh, _, va = operands(bb, d, hd)
            _, w_inter, floor, _, _ = gate[st]
            numa = _dot(sqk[st], va) + lane_tile(w_inter, DV_AUG) * _dot(qh, ct_ref[st].astype(BF16))
            den = numa[:, DV_M:DV_AUG]
            inv = 1.0 / jnp.maximum(jnp.abs(den), floor)
            h = numa[:, 0:DV_M] * lane_tile(inv, DV_M)
            dirs[d][5][bb, :, hd * DV_M:(hd + 1) * DV_M] = h.astype(BF16)
        for st, sid in group:
            _, kh, va = operands(*sid)
            _, _, _, ws, decay = gate[st]
            kst = (kh.astype(F32).T * ws).astype(BF16)
            ct_ref[st] = lane_tile(decay, DV_AUG) * ct_ref[st] + _dot(kst, va)
    m_ref[...] = jnp.concatenate(m_out, axis=0)


def _mlstm(mq, mk, mv, gcol, grow):
    B, S, _ = mq.shape
    nc = S // CHUNK
    nb = 2 if B % 2 == 0 else 1
    fwd = lambda n: pl.BlockSpec((nb, CHUNK, n), lambda b, c: (b, c, 0))
    bwd = lambda n: pl.BlockSpec((nb, CHUNK, n), lambda b, c: (b, nc - 1 - c, 0))
    rowf = pl.BlockSpec((nb, W_MG, CHUNK), lambda b, c: (b, 0, c))
    rowb = pl.BlockSpec((nb, W_MG, CHUNK), lambda b, c: (b, 0, nc - 1 - c))
    return pl.pallas_call(
        _mlstm_kernel,
        grid=(B // nb, nc),
        in_specs=[fwd(W_MQ), fwd(W_MK), fwd(W_MV), fwd(W_MG), rowf,
                  bwd(W_MQ), bwd(W_MK), bwd(W_MV), bwd(W_MG), rowb],
        out_specs=(fwd(W_MV), bwd(W_MV)),
        out_shape=(jax.ShapeDtypeStruct((B, S, W_MV), BF16),
                   jax.ShapeDtypeStruct((B, S, W_MV), BF16)),
        scratch_shapes=[pltpu.VMEM((nb * N_STREAMS, DQK_M, DV_AUG), F32),
                        pltpu.VMEM((nb * N_STREAMS, LANES), F32)],
        compiler_params=_params(("parallel", "arbitrary")),
        name="mlstm",
    )(mq, mk, mv, gcol, grow, mq, mk, mv, gcol, grow)


def _mix_kernel(x_ref, ada_ref, ya_ref, hf_ref, hb_ref, og_ref, gab_ref,
                wpa_ref, wpm_ref, wout_ref, mng_ref, lg_ref, lb_ref, o_ref, *, alpha):
    hs = hf_ref[0].astype(F32) + hb_ref[0].astype(F32)
    hn = jnp.concatenate(
        [_norm(hs[:, hd * DV_M:(hd + 1) * DV_M]) for hd in range(N_HEADS_M)], axis=1)
    ym = (og_ref[0].astype(F32) * (hn * mng_ref[...])).astype(BF16)
    gab = gab_ref[0].astype(F32)
    pre = gab[:, 0:D_MODEL] * _dot(ya_ref[0], wpa_ref[...]) + gab[:, D_MODEL:] * _dot(ym, wpm_ref[...])
    mix = _dot(pre.astype(BF16), wout_ref[...])
    g1 = ada_ref[0][:, 2 * D_MODEL:3 * D_MODEL]
    o_ref[0] = _norm(alpha * x_ref[0] + g1 * mix) * lg_ref[...] + lb_ref[...]


def _mix(x, ada3, ya, hf, hb, og, gab, wpa, wpm, wout, mnorm_g, ln_g, ln_b, alpha, ts):
    B, S, D = x.shape
    tok = lambda n: pl.BlockSpec((1, ts, n), lambda b, i: (b, i, 0))
    consts = (wpa, wpm, wout, mnorm_g.reshape(1, W_MV).astype(F32),
              ln_g.reshape(1, D).astype(F32), ln_b.reshape(1, D).astype(F32))
    return pl.pallas_call(
        functools.partial(_mix_kernel, alpha=alpha),
        grid=(B, S // ts),
        in_specs=[tok(D), pl.BlockSpec((1, 1, 6 * D), lambda b, i: (b, 0, 0)),
                  tok(W_AV), tok(W_MV), tok(W_MV), tok(W_MO), tok(W_BG)]
                 + [_const_spec(w.shape) for w in consts],
        out_specs=tok(D),
        out_shape=jax.ShapeDtypeStruct((B, S, D), F32),
        compiler_params=_params(("parallel", "parallel")),
        name="mix",
    )(x, ada3, ya, hf, hb, og, gab, *consts)


def _gelu_tanh(x):
    return 0.5 * x * (1.0 + jnp.tanh(math.sqrt(2.0 / math.pi) * (x + 0.044715 * (x * x * x))))


def _ffn_kernel(xp_ref, x_ref, xn_ref, ada_ref, wug_ref, wuv_ref, cw_ref, cb_ref, wd_ref,
                lg_ref, lb_ref, o_ref, *, alpha):
    ts = x_ref.shape[1]
    H = SUBLANES
    i = pl.program_id(1)
    ada = ada_ref[0]
    sh2 = ada[:, 3 * D_MODEL:4 * D_MODEL]
    sc2 = ada[:, 4 * D_MODEL:5 * D_MODEL]
    g2 = ada[:, 5 * D_MODEL:6 * D_MODEL]
    x = x_ref[0]

    def modulated(v):
        return _norm(v) * (1.0 + sc2) + sh2

    hp = jnp.where(i == 0, 0.0, modulated(xp_ref[0]))
    hm = modulated(x)
    hn = jnp.where(i == pl.num_programs(1) - 1, 0.0, modulated(xn_ref[0]))
    ug = _dot(jnp.concatenate([hp, hm, hn], axis=0).astype(BF16), wug_ref[...])
    uv = _dot(hm.astype(BF16), wuv_ref[...])
    n = ts + 2 * H
    prev = pltpu.roll(ug, 1, 0)[H:H + ts]
    nxt = pltpu.roll(ug, n - 1, 0)[H:H + ts]
    cw = cw_ref[...]
    conv = prev * cw[0:1] + ug[H:H + ts] * cw[1:2] + nxt * cw[2:3] + cb_ref[...]
    y = _dot((_gelu_tanh(conv) * uv).astype(BF16), wd_ref[...])
    o_ref[0] = _norm(alpha * x + g2 * y) * lg_ref[...] + lb_ref[...]


def _ffn(x, ada3, wug, wuv, conv_w, conv_b, wd, ln_g, ln_b, alpha, ts):
    B, S, D = x.shape
    H = SUBLANES
    nb = ts // H
    last = S // H - 1
    tok = pl.BlockSpec((1, ts, D), lambda b, i: (b, i, 0))
    prev = pl.BlockSpec((1, H, D), lambda b, i: (b, jnp.maximum(i * nb - 1, 0), 0))
    nxt = pl.BlockSpec((1, H, D), lambda b, i: (b, jnp.minimum((i + 1) * nb, last), 0))
    consts = (wug, wuv, conv_w.astype(F32), conv_b.reshape(1, D_FF).astype(F32), wd,
              ln_g.reshape(1, D).astype(F32), ln_b.reshape(1, D).astype(F32))
    return pl.pallas_call(
        functools.partial(_ffn_kernel, alpha=alpha),
        grid=(B, S // ts),
        in_specs=[prev, tok, nxt, pl.BlockSpec((1, 1, 6 * D), lambda b, i: (b, 0, 0))]
                 + [_const_spec(w.shape) for w in consts],
        out_specs=tok,
        out_shape=jax.ShapeDtypeStruct((B, S, D), F32),
        compiler_params=_params(("parallel", "parallel")),
        name="ffn",
    )(x, x, x, ada3, *consts)


def _rope_tables(seq):
    half = ROT_DIM // 2
    inv = ROPE_THETA ** (-jnp.arange(0, ROT_DIM, 2, dtype=F32) / ROT_DIM)
    ang = jnp.arange(seq, dtype=F32)[:, None] * inv[None, :]
    cos, sin = jnp.cos(ang), jnp.sin(ang)
    rest = DQK_A - ROT_DIM
    one = jnp.ones((seq, rest), F32)
    zr = jnp.zeros((seq, rest), F32)
    zh = jnp.zeros((seq, half), F32)
    reps = LANES // DQK_A
    ct = jnp.tile(jnp.concatenate([cos, cos, one], axis=1), (1, reps))
    s1 = jnp.tile(jnp.concatenate([-sin, zh, zr], axis=1), (1, reps))
    s2 = jnp.tile(jnp.concatenate([zh, sin, zr], axis=1), (1, reps))
    return ct, s1, s2


def _layer_weights(w_in, w_pa, w_pm, w_out, w_up, w_down):
    seg = lambda j: w_in[:, SEG_STARTS[j]:SEG_STARTS[j] + SEG_WIDTHS[j]]
    proj = (seg(0).astype(BF16), seg(1).astype(BF16), seg(2).T.astype(BF16),
            seg(3).astype(BF16), seg(4).astype(BF16), seg(5).astype(BF16), seg(6).astype(BF16),
            seg(7).T.astype(BF16), seg(8).astype(BF16))
    return (proj, w_pa.astype(BF16), w_pm.astype(BF16), w_out.astype(BF16),
            w_up[:, :D_FF].astype(BF16), w_up[:, D_FF:].astype(BF16), w_down.astype(BF16))


def _tile(seq, want):
    t = min(seq, want)
    assert seq % t == 0, (seq, want)
    return t


def _default_tiles(seq):
    return (512, 1024, min(2048, max(seq // 4, 256)), 512)


def _encoder_layer(x, ada, l, depth, weights, lambda_qk, subln_g, b_mgate, mnorm_g,
                   ln1_g, ln1_b, conv_w, conv_b, ln2_g, ln2_b, tiles):
    B, S, D = x.shape
    ts, tq, tk, sub = (_tile(S, t) for t in (tiles or _default_tiles(S)))
    alpha = (2.0 * depth) ** 0.25
    lam_init = 0.8 - 0.6 * math.exp(-0.3 * l)
    proj_w, wpa, wpm, wout, wug, wuv, wd = weights
    ada3 = ada.reshape(B, 1, 6 * D)
    q, k, vt, mq, mk, mv, og, gcol, grow, gab = _proj(x, ada3, _rope_tables(S), proj_w, b_mgate, ts)
    ya = _attn(q, k, vt, lambda_qk, subln_g, lam_init, tq, tk, sub)
    hf, hb = _mlstm(mq, mk, mv, gcol, grow)
    x1 = _mix(x, ada3, ya, hf, hb, og, gab, wpa, wpm, wout, mnorm_g, ln1_g, ln1_b, alpha, ts)
    return _ffn(x1, ada3, wug, wuv, conv_w, conv_b, wd, ln2_g, ln2_b, alpha, ts)


def _trunk(xs, cs, w_ada, b_ada, w_in, lambda_qk, subln_g, b_mgate, mnorm_g, w_pa, w_pm, w_out,
           ln1_g, ln1_b, w_up, conv_w, conv_b, w_down, ln2_g, ln2_b, tiles=None):
    depth = w_in.shape[0]
    rows = [c.shape[0] for c in cs]
    c_all = jnp.concatenate(cs, axis=0)
    xs = list(xs)
    for l in range(depth):
        ada_all = _ada(c_all, w_ada[l], b_ada[l])
        weights = _layer_weights(w_in[l], w_pa[l], w_pm[l], w_out[l], w_up[l], w_down[l])
        start = 0
        for g, n in enumerate(rows):
            xs[g] = _encoder_layer(xs[g], ada_all[start:start + n], l, depth, weights,
                                   lambda_qk[l], subln_g[l], b_mgate[l], mnorm_g[l],
                                   ln1_g[l], ln1_b[l], conv_w[l], conv_b[l], ln2_g[l], ln2_b[l], tiles)
            start += n
    return tuple(xs)


def kernel(x_prompt, x_sample, c_prompt, c_sample, w_ada, b_ada, w_in, lambda_qk, subln_g, b_mgate, mnorm_g, w_pa, w_pm, w_out, ln1_g, ln1_b, w_up, conv_w, conv_b, w_down, ln2_g, ln2_b):
    return _trunk((x_prompt, x_sample), (c_prompt, c_sample), w_ada, b_ada, w_in, lambda_qk, subln_g,
                  b_mgate, mnorm_g, w_pa, w_pm, w_out, ln1_g, ln1_b, w_up, conv_w, conv_b, w_down,
                  ln2_g, ln2_b)
```

```python
import functools
import math

import jax
import jax.numpy as jnp
import numpy as np
from jax import lax
from jax.experimental import pallas as pl
from jax.experimental.pallas import tpu as pltpu

F32 = jnp.float32
BF16 = jnp.bfloat16

D_MODEL = 1024
N_HEADS_A = 8
DQK_A = 64
DV_A = 2 * DQK_A
ROT_DIM = DQK_A // 4
ROPE_THETA = 500000.0
N_HEADS_M = 4
DQK_M = 128
DV_M = 256
CHUNK = 128
D_FF = 2816
LN_EPS = 1e-5

W_AQ = N_HEADS_A * 2 * DQK_A
W_AK = N_HEADS_A * 2 * DQK_A
W_AV = N_HEADS_A * DV_A
W_MQ = N_HEADS_M * DQK_M
W_MK = N_HEADS_M * DQK_M
W_MV = N_HEADS_M * DV_M
W_MO = N_HEADS_M * DV_M
W_MG = 2 * 2 * N_HEADS_M
W_BG = 2 * D_MODEL
SEG_WIDTHS = (W_AQ, W_AK, W_AV, W_MQ, W_MK, W_MV, W_MO, W_MG, W_BG)
SEG_STARTS = [0] + [int(s) for s in np.cumsum(SEG_WIDTHS)[:-1]]

LANES = 128
SUBLANES = 8
LOG2E = 1.4426950408889634
VMEM_LIMIT = 56 * 1024 * 1024

NT_DIMS = (((1,), (1,)), ((), ()))


def _dot(a, b):
    return jnp.dot(a, b, preferred_element_type=F32)


def _dot_nt(a, b):
    return lax.dot_general(a, b, NT_DIMS, preferred_element_type=F32)


def _norm(x):
    mu = jnp.mean(x, axis=-1, keepdims=True)
    xc = x - mu
    var = jnp.mean(xc * xc, axis=-1, keepdims=True)
    return xc * lax.rsqrt(var + LN_EPS)


def _split_bf16(a):
    hi = a.astype(BF16)
    lo = (a - hi.astype(F32)).astype(BF16)
    return hi, lo


def _const_spec(shape):
    nd = len(shape)
    return pl.BlockSpec(shape, lambda *_: (0,) * nd, pipeline_mode=pl.Buffered(1))


def _params(semantics):
    return pltpu.CompilerParams(dimension_semantics=semantics, vmem_limit_bytes=VMEM_LIMIT)


def _ada_kernel(c_ref, w_ref, b_ref, o_ref):
    c = c_ref[...]
    a = c * jax.nn.sigmoid(c)
    a_hi, a_lo = _split_bf16(a)
    w_hi, w_lo = _split_bf16(w_ref[...])
    o_ref[...] = _dot(a_hi, w_hi) + _dot(a_lo, w_hi) + _dot(a_hi, w_lo) + b_ref[...]


def _ada(c, w_ada, b_ada):
    rows, d = c.shape
    n = w_ada.shape[1]
    tn = 1024
    return pl.pallas_call(
        _ada_kernel,
        grid=(n // tn,),
        in_specs=[pl.BlockSpec((rows, d), lambda j: (0, 0)),
                  pl.BlockSpec((d, tn), lambda j: (0, j)),
                  pl.BlockSpec((1, tn), lambda j: (0, j))],
        out_specs=pl.BlockSpec((rows, tn), lambda j: (0, j)),
        out_shape=jax.ShapeDtypeStruct((rows, n), F32),
        compiler_params=_params(("parallel",)),
        name="ada",
    )(c, w_ada, b_ada.reshape(1, n))


def _log_sigmoid(g):
    return jnp.minimum(g, 0.0) - jnp.log1p(jnp.exp(-jnp.abs(g)))


def _proj_kernel(x_ref, ada_ref, ct_ref, s1_ref, s2_ref,
                 wq_ref, wk_ref, wvt_ref, wmq_ref, wmk_ref, wmv_ref, wmo_ref,
                 wgr_ref, bgr_ref, wbg_ref,
                 q_ref, k_ref, vt_ref, mq_ref, mk_ref, mv_ref, og_ref, gc_ref, gr_ref, gab_ref):
    ts = x_ref.shape[1]
    ada = ada_ref[0]
    sh1 = ada[:, 0:D_MODEL]
    sc1 = ada[:, D_MODEL:2 * D_MODEL]
    h = (_norm(x_ref[0]) * (1.0 + sc1) + sh1).astype(BF16)

    ct = ct_ref[...]
    s1 = s1_ref[...]
    s2 = s2_ref[...]

    def rope_store(w_ref, o_ref, scale):
        a = _dot(h, w_ref[...])
        for j in range(W_AQ // LANES):
            blk = a[:, j * LANES:(j + 1) * LANES]
            r = (blk * ct + pltpu.roll(blk, LANES - ROT_DIM // 2, 1) * s1
                 + pltpu.roll(blk, ROT_DIM // 2, 1) * s2)
            o_ref[0, :, j * LANES:(j + 1) * LANES] = (r * scale).astype(o_ref.dtype)

    rope_store(wq_ref, q_ref, (DQK_A ** -0.5) * LOG2E)
    rope_store(wk_ref, k_ref, 1.0)
    vt_ref[0] = _dot_nt(wvt_ref[...], h).astype(vt_ref.dtype)
    mq_ref[0] = _dot(h, wmq_ref[...]).astype(mq_ref.dtype)
    mk_ref[0] = (_dot(h, wmk_ref[...]) * (DQK_M ** -0.5)).astype(mk_ref.dtype)
    mv_ref[0] = _dot(h, wmv_ref[...]).astype(mv_ref.dtype)
    og_ref[0] = jax.nn.sigmoid(_dot(h, wmo_ref[...])).astype(og_ref.dtype)
    gab_ref[0] = jax.nn.sigmoid(_dot(h, wbg_ref[...])).astype(gab_ref.dtype)

    r = lax.broadcasted_iota(jnp.int32, (ts, ts), 0)
    c = lax.broadcasted_iota(jnp.int32, (ts, ts), 1)
    shift = CHUNK.bit_length() - 1
    same = jnp.right_shift(r, shift) == jnp.right_shift(c, shift)
    lower = jnp.where(same & (c <= r), 1.0, 0.0).astype(BF16)
    upper = jnp.where(same & (c >= r), 1.0, 0.0).astype(BF16)

    g = _dot_nt(wgr_ref[...], h) + bgr_ref[...]
    ch = lax.broadcasted_iota(jnp.int32, g.shape, 0)
    is_f = (ch & (2 * N_HEADS_M - 1)) >= N_HEADS_M
    val = jnp.where(is_f, _log_sigmoid(g), g)
    f_hi, f_lo = _split_bf16(jnp.where(is_f & (ch < 2 * N_HEADS_M), val, 0.0))
    b_hi, b_lo = _split_bf16(jnp.where(is_f & (ch >= 2 * N_HEADS_M), val, 0.0))
    grow = (_dot(f_hi, upper) + _dot(f_lo, upper) + _dot(b_hi, lower) + _dot(b_lo, lower)
            + jnp.where(is_f, 0.0, val))
    gr_ref[0] = grow
    padded = jnp.concatenate([grow, jnp.zeros((LANES - W_MG, ts), F32)], axis=0)
    for j in range(ts // LANES):
        gc_ref[0, j * LANES:(j + 1) * LANES, :] = padded[:, j * LANES:(j + 1) * LANES].T[:, 0:W_MG]


def _proj(x, ada3, rope, wts, b_mgate, ts):
    B, S, D = x.shape
    ct, s1, s2 = rope
    wq, wk, wvt, wmq, wmk, wmv, wmo, wgr, wbg = wts
    bgr = b_mgate.reshape(W_MG, 1).astype(F32)
    tok = lambda n: pl.BlockSpec((1, ts, n), lambda b, i: (b, i, 0))
    tab = pl.BlockSpec((ts, LANES), lambda b, i: (i, 0))
    consts = (wq, wk, wvt, wmq, wmk, wmv, wmo, wgr, bgr, wbg)
    out_shape = (
        jax.ShapeDtypeStruct((B, S, W_AQ), BF16),
        jax.ShapeDtypeStruct((B, S, W_AK), BF16),
        jax.ShapeDtypeStruct((B, W_AV, S), BF16),
        jax.ShapeDtypeStruct((B, S, W_MQ), BF16),
        jax.ShapeDtypeStruct((B, S, W_MK), BF16),
        jax.ShapeDtypeStruct((B, S, W_MV), BF16),
        jax.ShapeDtypeStruct((B, S, W_MO), BF16),
        jax.ShapeDtypeStruct((B, S, W_MG), F32),
        jax.ShapeDtypeStruct((B, W_MG, S), F32),
        jax.ShapeDtypeStruct((B, S, W_BG), BF16),
    )
    out_specs = (
        tok(W_AQ), tok(W_AK),
        pl.BlockSpec((1, W_AV, ts), lambda b, i: (b, 0, i)),
        tok(W_MQ), tok(W_MK), tok(W_MV), tok(W_MO), tok(W_MG),
        pl.BlockSpec((1, W_MG, ts), lambda b, i: (b, 0, i)),
        tok(W_BG),
    )
    return pl.pallas_call(
        _proj_kernel,
        grid=(B, S // ts),
        in_specs=[tok(D), pl.BlockSpec((1, 1, 6 * D), lambda b, i: (b, 0, 0)), tab, tab, tab]
                 + [_const_spec(w.shape) for w in consts],
        out_specs=out_specs,
        out_shape=out_shape,
        compiler_params=_params(("parallel", "parallel")),
        name="proj",
    )(x, ada3, ct, s1, s2, *consts)


ONES_ROWS = 16


def _attn_kernel(lam_ref, q_ref, k_ref, vt_ref, g_ref, o_ref,
                 s_sets, p_sets, acc_sets, m_sets, mx_sets, alpha_sets, *, tq, tk, sub, lam_init):
    lq = lam_ref[...]
    la = jnp.sum(lq[0:1] * lq[1:2], axis=1, keepdims=True)
    lb = jnp.sum(lq[2:3] * lq[3:4], axis=1, keepdims=True)
    lam = jnp.exp(la) - jnp.exp(lb) + lam_init
    nsets = s_sets.shape[0]
    for t in range(q_ref.shape[1] // tq):
        i = t % nsets
        _attn_query_tile(q_ref.at[0, t * tq:(t + 1) * tq], k_ref, vt_ref, g_ref,
                         o_ref.at[0, t * tq:(t + 1) * tq], lam,
                         s_sets.at[i], p_sets.at[i], acc_sets.at[i], m_sets.at[i], mx_sets.at[i],
                         alpha_sets.at[i], tk=tk, sub=sub, lam_init=lam_init)


def _attn_query_tile(q_ref, k_ref, vt_ref, g_ref, o_ref, lam,
                     s_ref, p_ref, acc, m_ref, mx_ref, alpha_ref, *, tk, sub, lam_init):
    n = k_ref.shape[1] // tk
    nsub = tk // sub
    q = q_ref[...]
    lane = lax.broadcasted_iota(jnp.int32, q.shape, 1)
    zero = jnp.zeros_like(q)
    qs = (jnp.where(lane < DQK_A, q, zero), jnp.where(lane >= DQK_A, q, zero))
    ones_rows = jnp.ones((ONES_ROWS, sub), BF16)
    maps = range(2)

    acc[...] = jnp.zeros_like(acc)
    m_ref[...] = jnp.full_like(m_ref, -jnp.inf)

    def scores_sub(j, r, mx):
        kc = k_ref[0, pl.ds(pl.multiple_of(j * tk + r * sub, sub), sub), :]
        out = []
        for mp in maps:
            s = _dot_nt(kc, qs[mp])
            s_ref[mp, r * sub:(r + 1) * sub, :] = s
            cm = jnp.max(s, axis=0, keepdims=True)
            out.append(cm if mx is None else jnp.maximum(mx[mp], cm))
        return out

    def softmax_sub(r, m_new):
        for mp in maps:
            s = s_ref[mp, r * sub:(r + 1) * sub, :]
            p_ref[mp, r * sub:(r + 1) * sub, :] = jnp.exp2(s - m_new[mp]).astype(BF16)

    def values_sub(j, r, pv):
        vc = vt_ref[0, :, pl.ds(pl.multiple_of(j * tk + r * sub, sub), sub)]
        va = jnp.concatenate([vc, ones_rows], axis=0)
        out = []
        for mp in maps:
            d = _dot(va, p_ref[mp, r * sub:(r + 1) * sub, :])
            out.append(d if pv is None else pv[mp] + d)
        return out

    def accumulate(pv):
        for mp in maps:
            acc[mp] = alpha_ref[mp] * acc[mp] + pv[mp]

    def trip(j, first, last):
        m_old = [m_ref[mp] for mp in maps]
        m_new = [jnp.maximum(m_old[mp], mx_ref[mp]) for mp in maps]
        pv = mx = None
        for r in range(-1, nsub + 1):
            if not last and 0 <= r - 1:
                mx = scores_sub(j + 1, r - 1, mx)
            if not first and r + 1 < nsub:
                pv = values_sub(j - 1, r + 1, pv)
            if 0 <= r < nsub:
                softmax_sub(r, m_new)
        if not first:
            accumulate(pv)
        for mp in maps:
            alpha_ref[mp] = jnp.exp2(m_old[mp] - m_new[mp])
            m_ref[mp] = m_new[mp]
            if not last:
                mx_ref[mp] = mx[mp]

    mx = None
    for r in range(nsub):
        mx = scores_sub(0, r, mx)
    for mp in maps:
        mx_ref[mp] = mx[mp]
    trip(0, True, False)

    def body(j, carry):
        trip(j, False, False)
        return carry

    lax.fori_loop(1, n - 1, body, 0)
    trip(n - 1, False, True)
    pv = None
    for r in range(nsub):
        pv = values_sub(n - 1, r, pv)
    accumulate(pv)

    a1 = acc[0]
    a2 = acc[1]
    o = a1[0:DV_A] / a1[DV_A:DV_A + 1] - lam * (a2[0:DV_A] / a2[DV_A:DV_A + 1])
    ms = jnp.mean(o * o, axis=0, keepdims=True)
    y = o * lax.rsqrt(ms + LN_EPS) * g_ref[...] * (1.0 - lam_init)
    o_ref[...] = y.T.astype(o_ref.dtype)


def _attn(q, k, vt, lambda_qk, subln_g, lam_init, tq, tk, sub, tiles_per_step):
    B, S, _ = q.shape
    assert S // tk >= 2 and tk % sub == 0, (S, tk, sub)
    tb = tq * tiles_per_step
    assert S % tb == 0, (S, tq, tiles_per_step)
    nsets = min(tiles_per_step, 2)
    kern = functools.partial(_attn_kernel, tq=tq, tk=tk, sub=sub, lam_init=lam_init)
    return pl.pallas_call(
        kern,
        grid=(B, N_HEADS_A, S // tb),
        in_specs=[pl.BlockSpec((4, DQK_A), lambda b, h, i: (0, 0)),
                  pl.BlockSpec((1, tb, DV_A), lambda b, h, i: (b, i, h)),
                  pl.BlockSpec((1, S, DV_A), lambda b, h, i: (b, 0, h)),
                  pl.BlockSpec((1, DV_A, S), lambda b, h, i: (b, h, 0)),
                  pl.BlockSpec((DV_A, 1), lambda b, h, i: (0, 0))],
        out_specs=pl.BlockSpec((1, tb, DV_A), lambda b, h, i: (b, i, h)),
        out_shape=jax.ShapeDtypeStruct((B, S, N_HEADS_A * DV_A), BF16),
        scratch_shapes=[pltpu.VMEM((nsets, 2, tk, tq), F32), pltpu.VMEM((nsets, 2, tk, tq), BF16),
                        pltpu.VMEM((nsets, 2, DV_A + ONES_ROWS, tq), F32),
                        pltpu.VMEM((nsets, 2, 1, tq), F32), pltpu.VMEM((nsets, 2, 1, tq), F32),
                        pltpu.VMEM((nsets, 2, 1, tq), F32)],
        compiler_params=_params(("parallel", "parallel", "parallel")),
        name="attn",
    )(lambda_qk.astype(F32), q, k, vt, subln_g.reshape(DV_A, 1).astype(F32))


N_STREAMS = 2 * N_HEADS_M
MLSTM_GROUP = 2
DV_AUG = DV_M + LANES


def _mlstm_kernel(qf_ref, kf_ref, vf_ref, gcf_ref, grf_ref,
                  qb_ref, kb_ref, vb_ref, gcb_ref, grb_ref,
                  hf_ref, hb_ref, ct_ref, m_ref):
    L = CHUNK

    @pl.when(pl.program_id(1) == 0)
    def _():
        ct_ref[...] = jnp.zeros_like(ct_ref)
        m_ref[...] = jnp.zeros_like(m_ref)

    row = lax.broadcasted_iota(jnp.int32, (L, L), 0)
    col = lax.broadcasted_iota(jnp.int32, (L, L), 1)
    assert L == LANES
    ones_blk = jnp.ones((L, LANES), BF16)

    def lane_tile(a, width):
        return jnp.concatenate([a] * (width // LANES), axis=1)

    dirs = ((qf_ref, kf_ref, vf_ref, gcf_ref, grf_ref, hf_ref),
            (qb_ref, kb_ref, vb_ref, gcb_ref, grb_ref, hb_ref))
    m_all = m_ref[...]
    m_out = []
    nb = qf_ref.shape[0]
    streams = [(bb, d, hd) for bb in range(nb) for d in range(2) for hd in range(N_HEADS_M)]

    def operands(bb, d, hd):
        q_ref, k_ref, v_ref = dirs[d][0:3]
        qh = q_ref[bb, :, hd * DQK_M:(hd + 1) * DQK_M]
        kh = k_ref[bb, :, hd * DQK_M:(hd + 1) * DQK_M]
        va = jnp.concatenate([v_ref[bb, :, hd * DV_M:(hd + 1) * DV_M], ones_blk], axis=1)
        return qh, kh, va

    groups = [list(enumerate(streams))[g0:g0 + MLSTM_GROUP]
              for g0 in range(0, len(streams), MLSTM_GROUP)]
    gate = {}
    sqk = {}

    def gate_stage(group):
        for st, (bb, d, hd) in group:
            tri = (col <= row) if d == 0 else (col >= row)
            e = L - 1 if d == 0 else 0
            gc = dirs[d][3][bb]
            gr = dirs[d][4][bb]
            ci = d * 2 * N_HEADS_M + hd
            cf = ci + N_HEADS_M
            i_row = gr[ci:ci + 1, :]
            b_row = gr[cf:cf + 1, :]
            b_t = jnp.broadcast_to(gc[:, cf:cf + 1], (L, LANES))
            m_prev = m_all[st:st + 1, :]
            dm = jnp.where(tri, b_t - b_row + i_row, -jnp.inf)
            inter = b_t + m_prev
            m_t = jnp.maximum(inter, jnp.broadcast_to(jnp.max(dm, axis=1, keepdims=True), (L, LANES)))
            w = jnp.exp(dm - m_t)
            w_inter = jnp.exp(inter - m_t)
            b_end = b_row[:, e:e + 1]
            m_new = m_t[e:e + 1, :]
            ws = jnp.exp(b_end - b_row + i_row - m_new)
            decay = jnp.exp(b_end + m_prev - m_new)
            gate[st] = (w, w_inter, jnp.exp(-m_t), ws, decay)
            m_out.append(m_new)

    def score_stage(group):
        for st, sid in group:
            qh, kh, _ = operands(*sid)
            sqk[st] = (_dot_nt(qh, kh) * gate[st][0]).astype(BF16)

    def output_stage(group):
        for st, (bb, d, hd) in group:
            qh, _, va = operands(bb, d, hd)
            _, w_inter, floor, _, _ = gate[st]
            numa = _dot(sqk[st], va) + lane_tile(w_inter, DV_AUG) * _dot(qh, ct_ref[st].astype(BF16))
            den = numa[:, DV_M:DV_AUG]
            inv = 1.0 / jnp.maximum(jnp.abs(den), floor)
            h = numa[:, 0:DV_M] * lane_tile(inv, DV_M)
            dirs[d][5][bb, :, hd * DV_M:(hd + 1) * DV_M] = h.astype(BF16)

    def state_stage(group):
        for st, sid in group:
            _, kh, va = operands(*sid)
            _, _, _, ws, decay = gate[st]
            kst = (kh.astype(F32).T * ws).astype(BF16)
            ct_ref[st] = lane_tile(decay, DV_AUG) * ct_ref[st] + _dot(kst, va)

    stages = (gate_stage, score_stage, output_stage, state_stage)
    for t in range(len(groups) + len(stages) - 1):
        for k, stage in enumerate(stages):
            if 0 <= t - k < len(groups):
                stage(groups[t - k])
    m_ref[...] = jnp.concatenate(m_out, axis=0)


def _mlstm(mq, mk, mv, gcol, grow):
    B, S, _ = mq.shape
    nc = S // CHUNK
    nb = 2 if B % 2 == 0 else 1
    fwd = lambda n: pl.BlockSpec((nb, CHUNK, n), lambda b, c: (b, c, 0))
    bwd = lambda n: pl.BlockSpec((nb, CHUNK, n), lambda b, c: (b, nc - 1 - c, 0))
    rowf = pl.BlockSpec((nb, W_MG, CHUNK), lambda b, c: (b, 0, c))
    rowb = pl.BlockSpec((nb, W_MG, CHUNK), lambda b, c: (b, 0, nc - 1 - c))
    return pl.pallas_call(
        _mlstm_kernel,
        grid=(B // nb, nc),
        in_specs=[fwd(W_MQ), fwd(W_MK), fwd(W_MV), fwd(W_MG), rowf,
                  bwd(W_MQ), bwd(W_MK), bwd(W_MV), bwd(W_MG), rowb],
        out_specs=(fwd(W_MV), bwd(W_MV)),
        out_shape=(jax.ShapeDtypeStruct((B, S, W_MV), BF16),
                   jax.ShapeDtypeStruct((B, S, W_MV), BF16)),
        scratch_shapes=[pltpu.VMEM((nb * N_STREAMS, DQK_M, DV_AUG), F32),
                        pltpu.VMEM((nb * N_STREAMS, LANES), F32)],
        compiler_params=_params(("parallel", "arbitrary")),
        name="mlstm",
    )(mq, mk, mv, gcol, grow, mq, mk, mv, gcol, grow)


def _mix_kernel(x_ref, ada_ref, ya_ref, hf_ref, hb_ref, og_ref, gab_ref,
                wpa_ref, wpm_ref, wout_ref, mng_ref, lg_ref, lb_ref, o_ref, *, alpha):
    hs = hf_ref[0].astype(F32) + hb_ref[0].astype(F32)
    hn = jnp.concatenate(
        [_norm(hs[:, hd * DV_M:(hd + 1) * DV_M]) for hd in range(N_HEADS_M)], axis=1)
    ym = (og_ref[0].astype(F32) * (hn * mng_ref[...])).astype(BF16)
    gab = gab_ref[0].astype(F32)
    pre = gab[:, 0:D_MODEL] * _dot(ya_ref[0], wpa_ref[...]) + gab[:, D_MODEL:] * _dot(ym, wpm_ref[...])
    mix = _dot(pre.astype(BF16), wout_ref[...])
    g1 = ada_ref[0][:, 2 * D_MODEL:3 * D_MODEL]
    o_ref[0] = _norm(alpha * x_ref[0] + g1 * mix) * lg_ref[...] + lb_ref[...]


def _mix(x, ada3, ya, hf, hb, og, gab, wpa, wpm, wout, mnorm_g, ln_g, ln_b, alpha, ts):
    B, S, D = x.shape
    tok = lambda n: pl.BlockSpec((1, ts, n), lambda b, i: (b, i, 0))
    consts = (wpa, wpm, wout, mnorm_g.reshape(1, W_MV).astype(F32),
              ln_g.reshape(1, D).astype(F32), ln_b.reshape(1, D).astype(F32))
    return pl.pallas_call(
        functools.partial(_mix_kernel, alpha=alpha),
        grid=(B, S // ts),
        in_specs=[tok(D), pl.BlockSpec((1, 1, 6 * D), lambda b, i: (b, 0, 0)),
                  tok(W_AV), tok(W_MV), tok(W_MV), tok(W_MO), tok(W_BG)]
                 + [_const_spec(w.shape) for w in consts],
        out_specs=tok(D),
        out_shape=jax.ShapeDtypeStruct((B, S, D), F32),
        compiler_params=_params(("parallel", "parallel")),
        name="mix",
    )(x, ada3, ya, hf, hb, og, gab, *consts)


def _gelu_tanh(x):
    return 0.5 * x * (1.0 + jnp.tanh(math.sqrt(2.0 / math.pi) * (x + 0.044715 * (x * x * x))))


def _ffn_kernel(xp_ref, x_ref, xn_ref, ada_ref, wug_ref, wuv_ref, cw_ref, cb_ref, wd_ref,
                lg_ref, lb_ref, o_ref, *, alpha):
    ts = x_ref.shape[1]
    H = SUBLANES
    i = pl.program_id(1)
    ada = ada_ref[0]
    sh2 = ada[:, 3 * D_MODEL:4 * D_MODEL]
    sc2 = ada[:, 4 * D_MODEL:5 * D_MODEL]
    g2 = ada[:, 5 * D_MODEL:6 * D_MODEL]
    x = x_ref[0]

    def modulated(v):
        return _norm(v) * (1.0 + sc2) + sh2

    hp = jnp.where(i == 0, 0.0, modulated(xp_ref[0]))
    hm = modulated(x)
    hn = jnp.where(i == pl.num_programs(1) - 1, 0.0, modulated(xn_ref[0]))
    ug = _dot(jnp.concatenate([hp, hm, hn], axis=0).astype(BF16), wug_ref[...])
    uv = _dot(hm.astype(BF16), wuv_ref[...])
    n = ts + 2 * H
    prev = pltpu.roll(ug, 1, 0)[H:H + ts]
    nxt = pltpu.roll(ug, n - 1, 0)[H:H + ts]
    cw = cw_ref[...]
    conv = prev * cw[0:1] + ug[H:H + ts] * cw[1:2] + nxt * cw[2:3] + cb_ref[...]
    y = _dot((_gelu_tanh(conv) * uv).astype(BF16), wd_ref[...])
    o_ref[0] = _norm(alpha * x + g2 * y) * lg_ref[...] + lb_ref[...]


def _ffn(x, ada3, wug, wuv, conv_w, conv_b, wd, ln_g, ln_b, alpha, ts):
    B, S, D = x.shape
    H = SUBLANES
    nb = ts // H
    last = S // H - 1
    tok = pl.BlockSpec((1, ts, D), lambda b, i: (b, i, 0))
    prev = pl.BlockSpec((1, H, D), lambda b, i: (b, jnp.maximum(i * nb - 1, 0), 0))
    nxt = pl.BlockSpec((1, H, D), lambda b, i: (b, jnp.minimum((i + 1) * nb, last), 0))
    consts = (wug, wuv, conv_w.astype(F32), conv_b.reshape(1, D_FF).astype(F32), wd,
              ln_g.reshape(1, D).astype(F32), ln_b.reshape(1, D).astype(F32))
    return pl.pallas_call(
        functools.partial(_ffn_kernel, alpha=alpha),
        grid=(B, S // ts),
        in_specs=[prev, tok, nxt, pl.BlockSpec((1, 1, 6 * D), lambda b, i: (b, 0, 0))]
                 + [_const_spec(w.shape) for w in consts],
        out_specs=tok,
        out_shape=jax.ShapeDtypeStruct((B, S, D), F32),
        compiler_params=_params(("parallel", "parallel")),
        name="ffn",
    )(x, x, x, ada3, *consts)


def _rope_tables(seq):
    half = ROT_DIM // 2
    inv = ROPE_THETA ** (-jnp.arange(0, ROT_DIM, 2, dtype=F32) / ROT_DIM)
    ang = jnp.arange(seq, dtype=F32)[:, None] * inv[None, :]
    cos, sin = jnp.cos(ang), jnp.sin(ang)
    rest = DQK_A - ROT_DIM
    one = jnp.ones((seq, rest), F32)
    zr = jnp.zeros((seq, rest), F32)
    zh = jnp.zeros((seq, half), F32)
    reps = LANES // DQK_A
    ct = jnp.tile(jnp.concatenate([cos, cos, one], axis=1), (1, reps))
    s1 = jnp.tile(jnp.concatenate([-sin, zh, zr], axis=1), (1, reps))
    s2 = jnp.tile(jnp.concatenate([zh, sin, zr], axis=1), (1, reps))
    return ct, s1, s2


def _layer_weights(w_in, w_pa, w_pm, w_out, w_up, w_down):
    seg = lambda j: w_in[:, SEG_STARTS[j]:SEG_STARTS[j] + SEG_WIDTHS[j]]
    proj = (seg(0).astype(BF16), seg(1).astype(BF16), seg(2).T.astype(BF16),
            seg(3).astype(BF16), seg(4).astype(BF16), seg(5).astype(BF16), seg(6).astype(BF16),
            seg(7).T.astype(BF16), seg(8).astype(BF16))
    return (proj, w_pa.astype(BF16), w_pm.astype(BF16), w_out.astype(BF16),
            w_up[:, :D_FF].astype(BF16), w_up[:, D_FF:].astype(BF16), w_down.astype(BF16))


def _tile(seq, want):
    t = min(seq, want)
    assert seq % t == 0, (seq, want)
    return t


def _default_tiles(seq):
    tq, tk = 1024, min(2048, max(seq // 4, 256))
    per_step = max(seq // tq, 1) if seq // tk <= 4 else 1
    return (512, tq, tk, 512, per_step)


def _encoder_layer(x, ada, l, depth, weights, lambda_qk, subln_g, b_mgate, mnorm_g,
                   ln1_g, ln1_b, conv_w, conv_b, ln2_g, ln2_b, tiles):
    B, S, D = x.shape
    tiles = tiles or _default_tiles(S)
    ts, tq, tk, sub = (_tile(S, t) for t in tiles[:4])
    tiles_per_step = tiles[4]
    alpha = (2.0 * depth) ** 0.25
    lam_init = 0.8 - 0.6 * math.exp(-0.3 * l)
    proj_w, wpa, wpm, wout, wug, wuv, wd = weights
    ada3 = ada.reshape(B, 1, 6 * D)
    q, k, vt, mq, mk, mv, og, gcol, grow, gab = _proj(x, ada3, _rope_tables(S), proj_w, b_mgate, ts)
    ya = _attn(q, k, vt, lambda_qk, subln_g, lam_init, tq, tk, sub, tiles_per_step)
    hf, hb = _mlstm(mq, mk, mv, gcol, grow)
    x1 = _mix(x, ada3, ya, hf, hb, og, gab, wpa, wpm, wout, mnorm_g, ln1_g, ln1_b, alpha, ts)
    return _ffn(x1, ada3, wug, wuv, conv_w, conv_b, wd, ln2_g, ln2_b, alpha, ts)


def _trunk(xs, cs, w_ada, b_ada, w_in, lambda_qk, subln_g, b_mgate, mnorm_g, w_pa, w_pm, w_out,
           ln1_g, ln1_b, w_up, conv_w, conv_b, w_down, ln2_g, ln2_b, tiles=None):
    depth = w_in.shape[0]
    rows = [c.shape[0] for c in cs]
    c_all = jnp.concatenate(cs, axis=0)
    xs = list(xs)
    for l in range(depth):
        ada_all = _ada(c_all, w_ada[l], b_ada[l])
        weights = _layer_weights(w_in[l], w_pa[l], w_pm[l], w_out[l], w_up[l], w_down[l])
        start = 0
        for g, n in enumerate(rows):
            xs[g] = _encoder_layer(xs[g], ada_all[start:start + n], l, depth, weights,
                                   lambda_qk[l], subln_g[l], b_mgate[l], mnorm_g[l],
                                   ln1_g[l], ln1_b[l], conv_w[l], conv_b[l], ln2_g[l], ln2_b[l], tiles)
            start += n
    return tuple(xs)


def kernel(x_prompt, x_sample, c_prompt, c_sample, w_ada, b_ada, w_in, lambda_qk, subln_g, b_mgate, mnorm_g, w_pa, w_pm, w_out, ln1_g, ln1_b, w_up, conv_w, conv_b, w_down, ln2_g, ln2_b):
    return _trunk((x_prompt, x_sample), (c_prompt, c_sample), w_ada, b_ada, w_in, lambda_qk, subln_g,
                  b_mgate, mnorm_g, w_pa, w_pm, w_out, ln1_g, ln1_b, w_up, conv_w, conv_b, w_down,
                  ln2_g, ln2_b)
```

```python
import functools
import math

import jax
import jax.numpy as jnp
import numpy as np
from jax import lax
from jax.experimental import pallas as pl
from jax.experimental.pallas import tpu as pltpu

F32 = jnp.float32
BF16 = jnp.bfloat16

D_MODEL = 1024
N_HEADS_A = 8
DQK_A = 64
DV_A = 2 * DQK_A
ROT_DIM = DQK_A // 4
ROPE_THETA = 500000.0
N_HEADS_M = 4
DQK_M = 128
DV_M = 256
CHUNK = 128
D_FF = 2816
LN_EPS = 1e-5

W_AQ = N_HEADS_A * 2 * DQK_A
W_AK = N_HEADS_A * 2 * DQK_A
W_AV = N_HEADS_A * DV_A
W_MQ = N_HEADS_M * DQK_M
W_MK = N_HEADS_M * DQK_M
W_MV = N_HEADS_M * DV_M
W_MO = N_HEADS_M * DV_M
W_MG = 2 * 2 * N_HEADS_M
W_BG = 2 * D_MODEL
SEG_WIDTHS = (W_AQ, W_AK, W_AV, W_MQ, W_MK, W_MV, W_MO, W_MG, W_BG)
SEG_STARTS = [0] + [int(s) for s in np.cumsum(SEG_WIDTHS)[:-1]]

LANES = 128
SUBLANES = 8
LOG2E = 1.4426950408889634
VMEM_LIMIT = 56 * 1024 * 1024

NT_DIMS = (((1,), (1,)), ((), ()))


def _dot(a, b):
    return jnp.dot(a, b, preferred_element_type=F32)


def _dot_nt(a, b):
    return lax.dot_general(a, b, NT_DIMS, preferred_element_type=F32)


def _norm(x):
    mu = jnp.mean(x, axis=-1, keepdims=True)
    xc = x - mu
    var = jnp.mean(xc * xc, axis=-1, keepdims=True)
    return xc * lax.rsqrt(var + LN_EPS)


def _split_bf16(a):
    hi = a.astype(BF16)
    lo = (a - hi.astype(F32)).astype(BF16)
    return hi, lo


def _const_spec(shape):
    nd = len(shape)
    return pl.BlockSpec(shape, lambda *_: (0,) * nd, pipeline_mode=pl.Buffered(1))


def _params(semantics, fuse_inputs=None):
    return pltpu.CompilerParams(dimension_semantics=semantics, vmem_limit_bytes=VMEM_LIMIT,
                                allow_input_fusion=fuse_inputs)


def _ada_kernel(c_ref, w_ref, b_ref, o_ref):
    c = c_ref[...]
    a = c * jax.nn.sigmoid(c)
    a_hi, a_lo = _split_bf16(a)
    w_hi, w_lo = _split_bf16(w_ref[...])
    o_ref[...] = _dot(a_hi, w_hi) + _dot(a_lo, w_hi) + _dot(a_hi, w_lo) + b_ref[...]


def _ada(c, w_ada, b_ada):
    rows, d = c.shape
    n = w_ada.shape[1]
    tn = 1024
    return pl.pallas_call(
        _ada_kernel,
        grid=(n // tn,),
        in_specs=[pl.BlockSpec((rows, d), lambda j: (0, 0)),
                  pl.BlockSpec((d, tn), lambda j: (0, j)),
                  pl.BlockSpec((1, tn), lambda j: (0, j))],
        out_specs=pl.BlockSpec((rows, tn), lambda j: (0, j)),
        out_shape=jax.ShapeDtypeStruct((rows, n), F32),
        compiler_params=_params(("parallel",)),
        name="ada",
    )(c, w_ada, b_ada.reshape(1, n))


def _log_sigmoid(g):
    return jnp.minimum(g, 0.0) - jnp.log1p(jnp.exp(-jnp.abs(g)))


def _proj_kernel(x_ref, ada_ref, ct_ref, s1_ref, s2_ref,
                 wq_ref, wk_ref, wvt_ref, wmq_ref, wmk_ref, wmv_ref, wmo_ref,
                 wgr_ref, bgr_ref, wbg_ref,
                 q_ref, k_ref, vt_ref, mq_ref, mk_ref, mv_ref, og_ref, gc_ref, gr_ref, gab_ref):
    ts = x_ref.shape[1]
    ada = ada_ref[0]
    sh1 = ada[:, 0:D_MODEL]
    sc1 = ada[:, D_MODEL:2 * D_MODEL]
    h = (_norm(x_ref[0]) * (1.0 + sc1) + sh1).astype(BF16)

    ct = ct_ref[...]
    s1 = s1_ref[...]
    s2 = s2_ref[...]

    def rope_store(w_ref, o_ref, scale):
        a = _dot(h, w_ref[...])
        for j in range(W_AQ // LANES):
            blk = a[:, j * LANES:(j + 1) * LANES]
            r = (blk * ct + pltpu.roll(blk, LANES - ROT_DIM // 2, 1) * s1
                 + pltpu.roll(blk, ROT_DIM // 2, 1) * s2)
            o_ref[0, :, j * LANES:(j + 1) * LANES] = (r * scale).astype(o_ref.dtype)

    rope_store(wq_ref, q_ref, (DQK_A ** -0.5) * LOG2E)
    rope_store(wk_ref, k_ref, 1.0)
    vt_ref[0] = _dot_nt(wvt_ref[...], h).astype(vt_ref.dtype)
    mq_ref[0] = _dot(h, wmq_ref[...]).astype(mq_ref.dtype)
    mk_ref[0] = (_dot(h, wmk_ref[...]) * (DQK_M ** -0.5)).astype(mk_ref.dtype)
    mv_ref[0] = _dot(h, wmv_ref[...]).astype(mv_ref.dtype)
    og_ref[0] = jax.nn.sigmoid(_dot(h, wmo_ref[...])).astype(og_ref.dtype)
    gab_ref[0] = jax.nn.sigmoid(_dot(h, wbg_ref[...])).astype(gab_ref.dtype)

    r = lax.broadcasted_iota(jnp.int32, (ts, ts), 0)
    c = lax.broadcasted_iota(jnp.int32, (ts, ts), 1)
    shift = CHUNK.bit_length() - 1
    same = jnp.right_shift(r, shift) == jnp.right_shift(c, shift)
    lower = jnp.where(same & (c <= r), 1.0, 0.0).astype(BF16)
    upper = jnp.where(same & (c >= r), 1.0, 0.0).astype(BF16)

    g = _dot_nt(wgr_ref[...], h) + bgr_ref[...]
    ch = lax.broadcasted_iota(jnp.int32, g.shape, 0)
    is_f = (ch & (2 * N_HEADS_M - 1)) >= N_HEADS_M
    val = jnp.where(is_f, _log_sigmoid(g), g)
    f_hi, f_lo = _split_bf16(jnp.where(is_f & (ch < 2 * N_HEADS_M), val, 0.0))
    b_hi, b_lo = _split_bf16(jnp.where(is_f & (ch >= 2 * N_HEADS_M), val, 0.0))
    grow = (_dot(f_hi, upper) + _dot(f_lo, upper) + _dot(b_hi, lower) + _dot(b_lo, lower)
            + jnp.where(is_f, 0.0, val))
    gr_ref[0] = grow
    padded = jnp.concatenate([grow, jnp.zeros((LANES - W_MG, ts), F32)], axis=0)
    for j in range(ts // LANES):
        gc_ref[0, j * LANES:(j + 1) * LANES, :] = padded[:, j * LANES:(j + 1) * LANES].T[:, 0:W_MG]


def _proj(x, ada3, rope, wts, b_mgate, ts):
    B, S, D = x.shape
    ct, s1, s2 = rope
    wq, wk, wvt, wmq, wmk, wmv, wmo, wgr, wbg = wts
    bgr = b_mgate.reshape(W_MG, 1).astype(F32)
    tok = lambda n: pl.BlockSpec((1, ts, n), lambda b, i: (b, i, 0))
    tab = pl.BlockSpec((ts, LANES), lambda b, i: (i, 0))
    consts = (wq, wk, wvt, wmq, wmk, wmv, wmo, wgr, bgr, wbg)
    out_shape = (
        jax.ShapeDtypeStruct((B, S, W_AQ), BF16),
        jax.ShapeDtypeStruct((B, S, W_AK), BF16),
        jax.ShapeDtypeStruct((B, W_AV, S), BF16),
        jax.ShapeDtypeStruct((B, S, W_MQ), BF16),
        jax.ShapeDtypeStruct((B, S, W_MK), BF16),
        jax.ShapeDtypeStruct((B, S, W_MV), BF16),
        jax.ShapeDtypeStruct((B, S, W_MO), BF16),
        jax.ShapeDtypeStruct((B, S, W_MG), F32),
        jax.ShapeDtypeStruct((B, W_MG, S), F32),
        jax.ShapeDtypeStruct((B, S, W_BG), BF16),
    )
    out_specs = (
        tok(W_AQ), tok(W_AK),
        pl.BlockSpec((1, W_AV, ts), lambda b, i: (b, 0, i)),
        tok(W_MQ), tok(W_MK), tok(W_MV), tok(W_MO), tok(W_MG),
        pl.BlockSpec((1, W_MG, ts), lambda b, i: (b, 0, i)),
        tok(W_BG),
    )
    return pl.pallas_call(
        _proj_kernel,
        grid=(B, S // ts),
        in_specs=[tok(D), pl.BlockSpec((1, 1, 6 * D), lambda b, i: (b, 0, 0)), tab, tab, tab]
                 + [_const_spec(w.shape) for w in consts],
        out_specs=out_specs,
        out_shape=out_shape,
        compiler_params=_params(("parallel", "parallel"), [False] * 5 + [True] * len(consts)),
        name="proj",
    )(x, ada3, ct, s1, s2, *consts)


ONES_ROWS = 16


def _attn_kernel(lam_ref, q_ref, k_ref, vt_ref, g_ref, o_ref,
                 s_sets, p_sets, acc_sets, m_sets, mx_sets, alpha_sets, *, tq, tk, sub, lam_init):
    lq = lam_ref[...]
    la = jnp.sum(lq[0:1] * lq[1:2], axis=1, keepdims=True)
    lb = jnp.sum(lq[2:3] * lq[3:4], axis=1, keepdims=True)
    lam = jnp.exp(la) - jnp.exp(lb) + lam_init
    nsets = s_sets.shape[0]
    for t in range(q_ref.shape[1] // tq):
        i = t % nsets
        _attn_query_tile(q_ref.at[0, t * tq:(t + 1) * tq], k_ref, vt_ref, g_ref,
                         o_ref.at[0, t * tq:(t + 1) * tq], lam,
                         s_sets.at[i], p_sets.at[i], acc_sets.at[i], m_sets.at[i], mx_sets.at[i],
                         alpha_sets.at[i], tk=tk, sub=sub, lam_init=lam_init)


def _attn_query_tile(q_ref, k_ref, vt_ref, g_ref, o_ref, lam,
                     s_ref, p_ref, acc, m_ref, mx_ref, alpha_ref, *, tk, sub, lam_init):
    n = k_ref.shape[1] // tk
    nsub = tk // sub
    q = q_ref[...]
    lane = lax.broadcasted_iota(jnp.int32, q.shape, 1)
    zero = jnp.zeros_like(q)
    qs = (jnp.where(lane < DQK_A, q, zero), jnp.where(lane >= DQK_A, q, zero))
    ones_rows = jnp.ones((ONES_ROWS, sub), BF16)
    maps = range(2)

    acc[...] = jnp.zeros_like(acc)
    m_ref[...] = jnp.full_like(m_ref, -jnp.inf)

    def scores_sub(j, r, mx):
        kc = k_ref[0, pl.ds(pl.multiple_of(j * tk + r * sub, sub), sub), :]
        out = []
        for mp in maps:
            s = _dot_nt(kc, qs[mp])
            s_ref[mp, r * sub:(r + 1) * sub, :] = s
            cm = jnp.max(s, axis=0, keepdims=True)
            out.append(cm if mx is None else jnp.maximum(mx[mp], cm))
        return out

    def softmax_sub(r, m_new):
        for mp in maps:
            s = s_ref[mp, r * sub:(r + 1) * sub, :]
            p_ref[mp, r * sub:(r + 1) * sub, :] = jnp.exp2(s - m_new[mp]).astype(BF16)

    def values_sub(j, r, pv):
        vc = vt_ref[0, :, pl.ds(pl.multiple_of(j * tk + r * sub, sub), sub)]
        va = jnp.concatenate([vc, ones_rows], axis=0)
        out = []
        for mp in maps:
            d = _dot(va, p_ref[mp, r * sub:(r + 1) * sub, :])
            out.append(d if pv is None else pv[mp] + d)
        return out

    def accumulate(pv):
        for mp in maps:
            acc[mp] = alpha_ref[mp] * acc[mp] + pv[mp]

    def trip(j, first, last):
        m_old = [m_ref[mp] for mp in maps]
        m_new = [jnp.maximum(m_old[mp], mx_ref[mp]) for mp in maps]
        pv = mx = None
        for r in range(-1, nsub + 1):
            if not last and 0 <= r - 1:
                mx = scores_sub(j + 1, r - 1, mx)
            if not first and r + 1 < nsub:
                pv = values_sub(j - 1, r + 1, pv)
            if 0 <= r < nsub:
                softmax_sub(r, m_new)
        if not first:
            accumulate(pv)
        for mp in maps:
            alpha_ref[mp] = jnp.exp2(m_old[mp] - m_new[mp])
            m_ref[mp] = m_new[mp]
            if not last:
                mx_ref[mp] = mx[mp]

    mx = None
    for r in range(nsub):
        mx = scores_sub(0, r, mx)
    for mp in maps:
        mx_ref[mp] = mx[mp]
    trip(0, True, False)

    def body(j, carry):
        trip(j, False, False)
        return carry

    lax.fori_loop(1, n - 1, body, 0)
    trip(n - 1, False, True)
    pv = None
    for r in range(nsub):
        pv = values_sub(n - 1, r, pv)
    accumulate(pv)

    a1 = acc[0]
    a2 = acc[1]
    o = a1[0:DV_A] / a1[DV_A:DV_A + 1] - lam * (a2[0:DV_A] / a2[DV_A:DV_A + 1])
    ms = jnp.mean(o * o, axis=0, keepdims=True)
    y = o * lax.rsqrt(ms + LN_EPS) * g_ref[...] * (1.0 - lam_init)
    o_ref[...] = y.T.astype(o_ref.dtype)


def _attn(q, k, vt, lambda_qk, subln_g, lam_init, tq, tk, sub, tiles_per_step):
    B, S, _ = q.shape
    assert S // tk >= 2 and tk % sub == 0, (S, tk, sub)
    tb = tq * tiles_per_step
    assert S % tb == 0, (S, tq, tiles_per_step)
    nsets = min(tiles_per_step, 2)
    kern = functools.partial(_attn_kernel, tq=tq, tk=tk, sub=sub, lam_init=lam_init)
    return pl.pallas_call(
        kern,
        grid=(B, N_HEADS_A, S // tb),
        in_specs=[pl.BlockSpec((4, DQK_A), lambda b, h, i: (0, 0)),
                  pl.BlockSpec((1, tb, DV_A), lambda b, h, i: (b, i, h)),
                  pl.BlockSpec((1, S, DV_A), lambda b, h, i: (b, 0, h)),
                  pl.BlockSpec((1, DV_A, S), lambda b, h, i: (b, h, 0)),
                  pl.BlockSpec((DV_A, 1), lambda b, h, i: (0, 0))],
        out_specs=pl.BlockSpec((1, tb, DV_A), lambda b, h, i: (b, i, h)),
        out_shape=jax.ShapeDtypeStruct((B, S, N_HEADS_A * DV_A), BF16),
        scratch_shapes=[pltpu.VMEM((nsets, 2, tk, tq), F32), pltpu.VMEM((nsets, 2, tk, tq), BF16),
                        pltpu.VMEM((nsets, 2, DV_A + ONES_ROWS, tq), F32),
                        pltpu.VMEM((nsets, 2, 1, tq), F32), pltpu.VMEM((nsets, 2, 1, tq), F32),
                        pltpu.VMEM((nsets, 2, 1, tq), F32)],
        compiler_params=_params(("parallel", "parallel", "parallel")),
        name="attn",
    )(lambda_qk.astype(F32), q, k, vt, subln_g.reshape(DV_A, 1).astype(F32))


N_STREAMS = 2 * N_HEADS_M
MLSTM_GROUP = 2
DV_AUG = DV_M + LANES


def _mlstm_kernel(qf_ref, kf_ref, vf_ref, gcf_ref, grf_ref,
                  qb_ref, kb_ref, vb_ref, gcb_ref, grb_ref,
                  hf_ref, hb_ref, ct_ref, m_ref):
    L = CHUNK

    @pl.when(pl.program_id(1) == 0)
    def _():
        ct_ref[...] = jnp.zeros_like(ct_ref)
        m_ref[...] = jnp.zeros_like(m_ref)

    row = lax.broadcasted_iota(jnp.int32, (L, L), 0)
    col = lax.broadcasted_iota(jnp.int32, (L, L), 1)
    assert L == LANES
    ones_blk = jnp.ones((L, LANES), BF16)

    def lane_tile(a, width):
        return jnp.concatenate([a] * (width // LANES), axis=1)

    dirs = ((qf_ref, kf_ref, vf_ref, gcf_ref, grf_ref, hf_ref),
            (qb_ref, kb_ref, vb_ref, gcb_ref, grb_ref, hb_ref))
    m_all = m_ref[...]
    m_out = []
    nb = qf_ref.shape[0]
    streams = [(bb, d, hd) for bb in range(nb) for d in range(2) for hd in range(N_HEADS_M)]

    def operands(bb, d, hd):
        q_ref, k_ref, v_ref = dirs[d][0:3]
        qh = q_ref[bb, :, hd * DQK_M:(hd + 1) * DQK_M]
        kh = k_ref[bb, :, hd * DQK_M:(hd + 1) * DQK_M]
        va = jnp.concatenate([v_ref[bb, :, hd * DV_M:(hd + 1) * DV_M], ones_blk], axis=1)
        return qh, kh, va

    groups = [list(enumerate(streams))[g0:g0 + MLSTM_GROUP]
              for g0 in range(0, len(streams), MLSTM_GROUP)]
    gate = {}
    sqk = {}

    def gate_stage(group):
        for st, (bb, d, hd) in group:
            tri = (col <= row) if d == 0 else (col >= row)
            e = L - 1 if d == 0 else 0
            gc = dirs[d][3][bb]
            gr = dirs[d][4][bb]
            ci = d * 2 * N_HEADS_M + hd
            cf = ci + N_HEADS_M
            i_row = gr[ci:ci + 1, :]
            b_row = gr[cf:cf + 1, :]
            b_t = jnp.broadcast_to(gc[:, cf:cf + 1], (L, LANES))
            m_prev = m_all[st:st + 1, :]
            dm = jnp.where(tri, b_t - b_row + i_row, -jnp.inf)
            inter = b_t + m_prev
            m_t = jnp.maximum(inter, jnp.broadcast_to(jnp.max(dm, axis=1, keepdims=True), (L, LANES)))
            w = jnp.exp(dm - m_t)
            w_inter = jnp.exp(inter - m_t)
            b_end = b_row[:, e:e + 1]
            m_new = m_t[e:e + 1, :]
            ws = jnp.exp(b_end - b_row + i_row - m_new)
            decay = jnp.exp(b_end + m_prev - m_new)
            gate[st] = (w, w_inter, jnp.exp(-m_t), ws, decay)
            m_out.append(m_new)

    def score_stage(group):
        for st, sid in group:
            qh, kh, _ = operands(*sid)
            sqk[st] = (_dot_nt(qh, kh) * gate[st][0]).astype(BF16)

    def output_stage(group):
        for st, (bb, d, hd) in group:
            qh, _, va = operands(bb, d, hd)
            _, w_inter, floor, _, _ = gate[st]
            numa = _dot(sqk[st], va) + lane_tile(w_inter, DV_AUG) * _dot(qh, ct_ref[st].astype(BF16))
            den = numa[:, DV_M:DV_AUG]
            inv = 1.0 / jnp.maximum(jnp.abs(den), floor)
            h = numa[:, 0:DV_M] * lane_tile(inv, DV_M)
            dirs[d][5][bb, :, hd * DV_M:(hd + 1) * DV_M] = h.astype(BF16)

    def state_stage(group):
        for st, sid in group:
            _, kh, va = operands(*sid)
            _, _, _, ws, decay = gate[st]
            kst = (kh.astype(F32).T * ws).astype(BF16)
            ct_ref[st] = lane_tile(decay, DV_AUG) * ct_ref[st] + _dot(kst, va)

    stages = (gate_stage, score_stage, output_stage, state_stage)
    for t in range(len(groups) + len(stages) - 1):
        for k, stage in enumerate(stages):
            if 0 <= t - k < len(groups):
                stage(groups[t - k])
    m_ref[...] = jnp.concatenate(m_out, axis=0)


def _mlstm(mq, mk, mv, gcol, grow):
    B, S, _ = mq.shape
    nc = S // CHUNK
    nb = 2 if B % 2 == 0 else 1
    fwd = lambda n: pl.BlockSpec((nb, CHUNK, n), lambda b, c: (b, c, 0))
    bwd = lambda n: pl.BlockSpec((nb, CHUNK, n), lambda b, c: (b, nc - 1 - c, 0))
    rowf = pl.BlockSpec((nb, W_MG, CHUNK), lambda b, c: (b, 0, c))
    rowb = pl.BlockSpec((nb, W_MG, CHUNK), lambda b, c: (b, 0, nc - 1 - c))
    return pl.pallas_call(
        _mlstm_kernel,
        grid=(B // nb, nc),
        in_specs=[fwd(W_MQ), fwd(W_MK), fwd(W_MV), fwd(W_MG), rowf,
                  bwd(W_MQ), bwd(W_MK), bwd(W_MV), bwd(W_MG), rowb],
        out_specs=(fwd(W_MV), bwd(W_MV)),
        out_shape=(jax.ShapeDtypeStruct((B, S, W_MV), BF16),
                   jax.ShapeDtypeStruct((B, S, W_MV), BF16)),
        scratch_shapes=[pltpu.VMEM((nb * N_STREAMS, DQK_M, DV_AUG), F32),
                        pltpu.VMEM((nb * N_STREAMS, LANES), F32)],
        compiler_params=_params(("parallel", "arbitrary")),
        name="mlstm",
    )(mq, mk, mv, gcol, grow, mq, mk, mv, gcol, grow)


def _mix_kernel(x_ref, ada_ref, ya_ref, hf_ref, hb_ref, og_ref, gab_ref,
                wpa_ref, wpm_ref, wout_ref, mng_ref, lg_ref, lb_ref, o_ref, *, alpha):
    hs = hf_ref[0].astype(F32) + hb_ref[0].astype(F32)
    hn = jnp.concatenate(
        [_norm(hs[:, hd * DV_M:(hd + 1) * DV_M]) for hd in range(N_HEADS_M)], axis=1)
    ym = (og_ref[0].astype(F32) * (hn * mng_ref[...])).astype(BF16)
    gab = gab_ref[0].astype(F32)
    pre = gab[:, 0:D_MODEL] * _dot(ya_ref[0], wpa_ref[...]) + gab[:, D_MODEL:] * _dot(ym, wpm_ref[...])
    mix = _dot(pre.astype(BF16), wout_ref[...])
    g1 = ada_ref[0][:, 2 * D_MODEL:3 * D_MODEL]
    o_ref[0] = _norm(alpha * x_ref[0] + g1 * mix) * lg_ref[...] + lb_ref[...]


def _mix(x, ada3, ya, hf, hb, og, gab, wpa, wpm, wout, mnorm_g, ln_g, ln_b, alpha, ts):
    B, S, D = x.shape
    tok = lambda n: pl.BlockSpec((1, ts, n), lambda b, i: (b, i, 0))
    consts = (wpa, wpm, wout, mnorm_g.reshape(1, W_MV).astype(F32),
              ln_g.reshape(1, D).astype(F32), ln_b.reshape(1, D).astype(F32))
    return pl.pallas_call(
        functools.partial(_mix_kernel, alpha=alpha),
        grid=(B, S // ts),
        in_specs=[tok(D), pl.BlockSpec((1, 1, 6 * D), lambda b, i: (b, 0, 0)),
                  tok(W_AV), tok(W_MV), tok(W_MV), tok(W_MO), tok(W_BG)]
                 + [_const_spec(w.shape) for w in consts],
        out_specs=tok(D),
        out_shape=jax.ShapeDtypeStruct((B, S, D), F32),
        compiler_params=_params(("parallel", "parallel"), [False] * 7 + [True] * len(consts)),
        name="mix",
    )(x, ada3, ya, hf, hb, og, gab, *consts)


def _gelu_tanh(x):
    return 0.5 * x * (1.0 + jnp.tanh(math.sqrt(2.0 / math.pi) * (x + 0.044715 * (x * x * x))))


def _ffn_kernel(xp_ref, x_ref, xn_ref, ada_ref, wug_ref, wuv_ref, cw_ref, cb_ref, wd_ref,
                lg_ref, lb_ref, o_ref, *, alpha):
    ts = x_ref.shape[1]
    H = SUBLANES
    i = pl.program_id(1)
    ada = ada_ref[0]
    sh2 = ada[:, 3 * D_MODEL:4 * D_MODEL]
    sc2 = ada[:, 4 * D_MODEL:5 * D_MODEL]
    g2 = ada[:, 5 * D_MODEL:6 * D_MODEL]
    x = x_ref[0]

    def modulated(v):
        return _norm(v) * (1.0 + sc2) + sh2

    hp = jnp.where(i == 0, 0.0, modulated(xp_ref[0]))
    hm = modulated(x)
    hn = jnp.where(i == pl.num_programs(1) - 1, 0.0, modulated(xn_ref[0]))
    ug = _dot(jnp.concatenate([hp, hm, hn], axis=0).astype(BF16), wug_ref[...])
    uv = _dot(hm.astype(BF16), wuv_ref[...])
    n = ts + 2 * H
    prev = pltpu.roll(ug, 1, 0)[H:H + ts]
    nxt = pltpu.roll(ug, n - 1, 0)[H:H + ts]
    cw = cw_ref[...]
    conv = prev * cw[0:1] + ug[H:H + ts] * cw[1:2] + nxt * cw[2:3] + cb_ref[...]
    y = _dot((_gelu_tanh(conv) * uv).astype(BF16), wd_ref[...])
    o_ref[0] = _norm(alpha * x + g2 * y) * lg_ref[...] + lb_ref[...]


def _ffn(x, ada3, wug, wuv, conv_w, conv_b, wd, ln_g, ln_b, alpha, ts):
    B, S, D = x.shape
    H = SUBLANES
    nb = ts // H
    last = S // H - 1
    tok = pl.BlockSpec((1, ts, D), lambda b, i: (b, i, 0))
    prev = pl.BlockSpec((1, H, D), lambda b, i: (b, jnp.maximum(i * nb - 1, 0), 0))
    nxt = pl.BlockSpec((1, H, D), lambda b, i: (b, jnp.minimum((i + 1) * nb, last), 0))
    consts = (wug, wuv, conv_w.astype(F32), conv_b.reshape(1, D_FF).astype(F32), wd,
              ln_g.reshape(1, D).astype(F32), ln_b.reshape(1, D).astype(F32))
    return pl.pallas_call(
        functools.partial(_ffn_kernel, alpha=alpha),
        grid=(B, S // ts),
        in_specs=[prev, tok, nxt, pl.BlockSpec((1, 1, 6 * D), lambda b, i: (b, 0, 0))]
                 + [_const_spec(w.shape) for w in consts],
        out_specs=tok,
        out_shape=jax.ShapeDtypeStruct((B, S, D), F32),
        compiler_params=_params(("parallel", "parallel"), [False] * 4 + [True] * len(consts)),
        name="ffn",
    )(x, x, x, ada3, *consts)


def _rope_tables(seq):
    half = ROT_DIM // 2
    inv = ROPE_THETA ** (-jnp.arange(0, ROT_DIM, 2, dtype=F32) / ROT_DIM)
    ang = jnp.arange(seq, dtype=F32)[:, None] * inv[None, :]
    cos, sin = jnp.cos(ang), jnp.sin(ang)
    rest = DQK_A - ROT_DIM
    one = jnp.ones((seq, rest), F32)
    zr = jnp.zeros((seq, rest), F32)
    zh = jnp.zeros((seq, half), F32)
    reps = LANES // DQK_A
    ct = jnp.tile(jnp.concatenate([cos, cos, one], axis=1), (1, reps))
    s1 = jnp.tile(jnp.concatenate([-sin, zh, zr], axis=1), (1, reps))
    s2 = jnp.tile(jnp.concatenate([zh, sin, zr], axis=1), (1, reps))
    return ct, s1, s2


def _layer_weights(w_in, w_pa, w_pm, w_out, w_up, w_down):
    seg = lambda j: w_in[:, SEG_STARTS[j]:SEG_STARTS[j] + SEG_WIDTHS[j]]
    proj = (seg(0).astype(BF16), seg(1).astype(BF16), seg(2).T.astype(BF16),
            seg(3).astype(BF16), seg(4).astype(BF16), seg(5).astype(BF16), seg(6).astype(BF16),
            seg(7).T.astype(BF16), seg(8).astype(BF16))
    return (proj, w_pa.astype(BF16), w_pm.astype(BF16), w_out.astype(BF16),
            w_up[:, :D_FF].astype(BF16), w_up[:, D_FF:].astype(BF16), w_down.astype(BF16))


def _tile(seq, want):
    t = min(seq, want)
    assert seq % t == 0, (seq, want)
    return t


def _default_tiles(seq):
    tq, tk = 1024, min(2048, max(seq // 4, 256))
    per_step = max(seq // tq, 1) if seq // tk <= 4 else 1
    return (512, tq, tk, 512, per_step)


def _encoder_layer(x, ada, l, depth, weights, lambda_qk, subln_g, b_mgate, mnorm_g,
                   ln1_g, ln1_b, conv_w, conv_b, ln2_g, ln2_b, tiles):
    B, S, D = x.shape
    tiles = tiles or _default_tiles(S)
    ts, tq, tk, sub = (_tile(S, t) for t in tiles[:4])
    tiles_per_step = tiles[4]
    alpha = (2.0 * depth) ** 0.25
    lam_init = 0.8 - 0.6 * math.exp(-0.3 * l)
    proj_w, wpa, wpm, wout, wug, wuv, wd = weights
    ada3 = ada.reshape(B, 1, 6 * D)
    q, k, vt, mq, mk, mv, og, gcol, grow, gab = _proj(x, ada3, _rope_tables(S), proj_w, b_mgate, ts)
    ya = _attn(q, k, vt, lambda_qk, subln_g, lam_init, tq, tk, sub, tiles_per_step)
    hf, hb = _mlstm(mq, mk, mv, gcol, grow)
    x1 = _mix(x, ada3, ya, hf, hb, og, gab, wpa, wpm, wout, mnorm_g, ln1_g, ln1_b, alpha, ts)
    return _ffn(x1, ada3, wug, wuv, conv_w, conv_b, wd, ln2_g, ln2_b, alpha, ts)


def _trunk(xs, cs, w_ada, b_ada, w_in, lambda_qk, subln_g, b_mgate, mnorm_g, w_pa, w_pm, w_out,
           ln1_g, ln1_b, w_up, conv_w, conv_b, w_down, ln2_g, ln2_b, tiles=None):
    depth = w_in.shape[0]
    rows = [c.shape[0] for c in cs]
    c_all = jnp.concatenate(cs, axis=0)
    xs = list(xs)
    for l in range(depth):
        ada_all = _ada(c_all, w_ada[l], b_ada[l])
        weights = _layer_weights(w_in[l], w_pa[l], w_pm[l], w_out[l], w_up[l], w_down[l])
        start = 0
        for g, n in enumerate(rows):
            xs[g] = _encoder_layer(xs[g], ada_all[start:start + n], l, depth, weights,
                                   lambda_qk[l], subln_g[l], b_mgate[l], mnorm_g[l],
                                   ln1_g[l], ln1_b[l], conv_w[l], conv_b[l], ln2_g[l], ln2_b[l], tiles)
            start += n
    return tuple(xs)


def kernel(x_prompt, x_sample, c_prompt, c_sample, w_ada, b_ada, w_in, lambda_qk, subln_g, b_mgate, mnorm_g, w_pa, w_pm, w_out, ln1_g, ln1_b, w_up, conv_w, conv_b, w_down, ln2_g, ln2_b):
    return _trunk((x_prompt, x_sample), (c_prompt, c_sample), w_ada, b_ada, w_in, lambda_qk, subln_g,
                  b_mgate, mnorm_g, w_pa, w_pm, w_out, ln1_g, ln1_b, w_up, conv_w, conv_b, w_down,
                  ln2_g, ln2_b)
```

```python
import functools
import math

import jax
import jax.numpy as jnp
import numpy as np
from jax import lax
from jax.experimental import pallas as pl
from jax.experimental.pallas import tpu as pltpu

F32 = jnp.float32
BF16 = jnp.bfloat16

D_MODEL = 1024
N_HEADS_A = 8
DQK_A = 64
DV_A = 2 * DQK_A
ROT_DIM = DQK_A // 4
ROPE_THETA = 500000.0
N_HEADS_M = 4
DQK_M = 128
DV_M = 256
CHUNK = 128
D_FF = 2816
LN_EPS = 1e-5

W_AQ = N_HEADS_A * 2 * DQK_A
W_AK = N_HEADS_A * 2 * DQK_A
W_AV = N_HEADS_A * DV_A
W_MQ = N_HEADS_M * DQK_M
W_MK = N_HEADS_M * DQK_M
W_MV = N_HEADS_M * DV_M
W_MO = N_HEADS_M * DV_M
W_MG = 2 * 2 * N_HEADS_M
W_BG = 2 * D_MODEL
SEG_WIDTHS = (W_AQ, W_AK, W_AV, W_MQ, W_MK, W_MV, W_MO, W_MG, W_BG)
SEG_STARTS = [0] + [int(s) for s in np.cumsum(SEG_WIDTHS)[:-1]]

LANES = 128
SUBLANES = 8
LOG2E = 1.4426950408889634
VMEM_LIMIT = 56 * 1024 * 1024

NT_DIMS = (((1,), (1,)), ((), ()))


def _dot(a, b):
    return jnp.dot(a, b, preferred_element_type=F32)


def _dot_nt(a, b):
    return lax.dot_general(a, b, NT_DIMS, preferred_element_type=F32)


def _norm(x):
    mu = jnp.mean(x, axis=-1, keepdims=True)
    xc = x - mu
    var = jnp.mean(xc * xc, axis=-1, keepdims=True)
    return xc * lax.rsqrt(var + LN_EPS)


def _split_bf16(a):
    hi = a.astype(BF16)
    lo = (a - hi.astype(F32)).astype(BF16)
    return hi, lo


def _const_spec(shape):
    nd = len(shape)
    return pl.BlockSpec(shape, lambda *_: (0,) * nd, pipeline_mode=pl.Buffered(1))


def _params(semantics):
    return pltpu.CompilerParams(dimension_semantics=semantics, vmem_limit_bytes=VMEM_LIMIT)


def _ada_kernel(c_ref, w_ref, b_ref, o_ref):
    c = c_ref[...]
    a = c * jax.nn.sigmoid(c)
    a_hi, a_lo = _split_bf16(a)
    w_hi, w_lo = _split_bf16(w_ref[...])
    o_ref[...] = _dot(a_hi, w_hi) + _dot(a_lo, w_hi) + _dot(a_hi, w_lo) + b_ref[...]


def _ada(c, w_ada, b_ada):
    rows, d = c.shape
    n = w_ada.shape[1]
    tn = 1024
    return pl.pallas_call(
        _ada_kernel,
        grid=(n // tn,),
        in_specs=[pl.BlockSpec((rows, d), lambda j: (0, 0)),
                  pl.BlockSpec((d, tn), lambda j: (0, j)),
                  pl.BlockSpec((1, tn), lambda j: (0, j))],
        out_specs=pl.BlockSpec((rows, tn), lambda j: (0, j)),
        out_shape=jax.ShapeDtypeStruct((rows, n), F32),
        compiler_params=_params(("parallel",)),
        name="ada",
    )(c, w_ada, b_ada.reshape(1, n))


def _log_sigmoid(g):
    return jnp.minimum(g, 0.0) - jnp.log1p(jnp.exp(-jnp.abs(g)))


def _proj_kernel(x_ref, ada_ref, ct_ref, s1_ref, s2_ref,
                 wq_ref, wk_ref, wvt_ref, wmq_ref, wmk_ref, wmv_ref, wmo_ref,
                 wgr_ref, bgr_ref, wbg_ref,
                 q_ref, k_ref, vt_ref, mq_ref, mk_ref, mv_ref, og_ref, gc_ref, gr_ref, gab_ref):
    ts = x_ref.shape[1]
    ada = ada_ref[0]
    sh1 = ada[:, 0:D_MODEL]
    sc1 = ada[:, D_MODEL:2 * D_MODEL]
    h = (_norm(x_ref[0]) * (1.0 + sc1) + sh1).astype(BF16)

    ct = ct_ref[...]
    s1 = s1_ref[...]
    s2 = s2_ref[...]

    def rope_store(w_ref, o_ref, scale):
        a = _dot(h, w_ref[...])
        for j in range(W_AQ // LANES):
            blk = a[:, j * LANES:(j + 1) * LANES]
            r = (blk * ct + pltpu.roll(blk, LANES - ROT_DIM // 2, 1) * s1
                 + pltpu.roll(blk, ROT_DIM // 2, 1) * s2)
            o_ref[0, :, j * LANES:(j + 1) * LANES] = (r * scale).astype(o_ref.dtype)

    rope_store(wq_ref, q_ref, (DQK_A ** -0.5) * LOG2E)
    rope_store(wk_ref, k_ref, 1.0)
    vt_ref[0] = _dot_nt(wvt_ref[...], h).astype(vt_ref.dtype)
    mq_ref[0] = _dot(h, wmq_ref[...]).astype(mq_ref.dtype)
    mk_ref[0] = (_dot(h, wmk_ref[...]) * (DQK_M ** -0.5)).astype(mk_ref.dtype)
    mv_ref[0] = _dot(h, wmv_ref[...]).astype(mv_ref.dtype)
    og_ref[0] = jax.nn.sigmoid(_dot(h, wmo_ref[...])).astype(og_ref.dtype)
    gab_ref[0] = jax.nn.sigmoid(_dot(h, wbg_ref[...])).astype(gab_ref.dtype)

    r = lax.broadcasted_iota(jnp.int32, (ts, ts), 0)
    c = lax.broadcasted_iota(jnp.int32, (ts, ts), 1)
    shift = CHUNK.bit_length() - 1
    same = jnp.right_shift(r, shift) == jnp.right_shift(c, shift)
    lower = jnp.where(same & (c <= r), 1.0, 0.0).astype(BF16)
    upper = jnp.where(same & (c >= r), 1.0, 0.0).astype(BF16)

    g = _dot_nt(wgr_ref[...], h) + bgr_ref[...]
    ch = lax.broadcasted_iota(jnp.int32, g.shape, 0)
    is_f = (ch & (2 * N_HEADS_M - 1)) >= N_HEADS_M
    val = jnp.where(is_f, _log_sigmoid(g), g)
    f_hi, f_lo = _split_bf16(jnp.where(is_f & (ch < 2 * N_HEADS_M), val, 0.0))
    b_hi, b_lo = _split_bf16(jnp.where(is_f & (ch >= 2 * N_HEADS_M), val, 0.0))
    grow = (_dot(f_hi, upper) + _dot(f_lo, upper) + _dot(b_hi, lower) + _dot(b_lo, lower)
            + jnp.where(is_f, 0.0, val))
    gr_ref[0] = grow
    padded = jnp.concatenate([grow, jnp.zeros((LANES - W_MG, ts), F32)], axis=0)
    for j in range(ts // LANES):
        gc_ref[0, j * LANES:(j + 1) * LANES, :] = padded[:, j * LANES:(j + 1) * LANES].T[:, 0:W_MG]


def _proj(x, ada3, rope, wts, b_mgate, ts):
    B, S, D = x.shape
    ct, s1, s2 = rope
    wq, wk, wvt, wmq, wmk, wmv, wmo, wgr, wbg = wts
    bgr = b_mgate.reshape(W_MG, 1).astype(F32)
    tok = lambda n: pl.BlockSpec((1, ts, n), lambda b, i: (b, i, 0))
    tab = pl.BlockSpec((ts, LANES), lambda b, i: (i, 0))
    consts = (wq, wk, wvt, wmq, wmk, wmv, wmo, wgr, bgr, wbg)
    out_shape = (
        jax.ShapeDtypeStruct((B, S, W_AQ), BF16),
        jax.ShapeDtypeStruct((B, S, W_AK), BF16),
        jax.ShapeDtypeStruct((B, W_AV, S), BF16),
        jax.ShapeDtypeStruct((B, S, W_MQ), BF16),
        jax.ShapeDtypeStruct((B, S, W_MK), BF16),
        jax.ShapeDtypeStruct((B, S, W_MV), BF16),
        jax.ShapeDtypeStruct((B, S, W_MO), BF16),
        jax.ShapeDtypeStruct((B, S, W_MG), F32),
        jax.ShapeDtypeStruct((B, W_MG, S), F32),
        jax.ShapeDtypeStruct((B, S, W_BG), BF16),
    )
    out_specs = (
        tok(W_AQ), tok(W_AK),
        pl.BlockSpec((1, W_AV, ts), lambda b, i: (b, 0, i)),
        tok(W_MQ), tok(W_MK), tok(W_MV), tok(W_MO), tok(W_MG),
        pl.BlockSpec((1, W_MG, ts), lambda b, i: (b, 0, i)),
        tok(W_BG),
    )
    return pl.pallas_call(
        _proj_kernel,
        grid=(B, S // ts),
        in_specs=[tok(D), pl.BlockSpec((1, 1, 6 * D), lambda b, i: (b, 0, 0)), tab, tab, tab]
                 + [_const_spec(w.shape) for w in consts],
        out_specs=out_specs,
        out_shape=out_shape,
        compiler_params=_params(("parallel", "parallel")),
        name="proj",
    )(x, ada3, ct, s1, s2, *consts)


ONES_ROWS = 16


def _attn_kernel(lam_ref, q_ref, k_ref, vt_ref, g_ref, o_ref,
                 s_sets, p_sets, acc_sets, m_sets, mx_sets, alpha_sets, *, tq, tk, sub, lam_init):
    lq = lam_ref[...]
    la = jnp.sum(lq[0:1] * lq[1:2], axis=1, keepdims=True)
    lb = jnp.sum(lq[2:3] * lq[3:4], axis=1, keepdims=True)
    lam = jnp.exp(la) - jnp.exp(lb) + lam_init
    nsets = s_sets.shape[0]
    for t in range(q_ref.shape[1] // tq):
        i = t % nsets
        _attn_query_tile(q_ref.at[0, t * tq:(t + 1) * tq], k_ref, vt_ref, g_ref,
                         o_ref.at[0, t * tq:(t + 1) * tq], lam,
                         s_sets.at[i], p_sets.at[i], acc_sets.at[i], m_sets.at[i], mx_sets.at[i],
                         alpha_sets.at[i], tk=tk, sub=sub, lam_init=lam_init)


def _attn_query_tile(q_ref, k_ref, vt_ref, g_ref, o_ref, lam,
                     s_ref, p_ref, acc, m_ref, mx_ref, alpha_ref, *, tk, sub, lam_init):
    n = k_ref.shape[1] // tk
    nsub = tk // sub
    q = q_ref[...]
    lane = lax.broadcasted_iota(jnp.int32, q.shape, 1)
    zero = jnp.zeros_like(q)
    qs = (jnp.where(lane < DQK_A, q, zero), jnp.where(lane >= DQK_A, q, zero))
    ones_rows = jnp.ones((ONES_ROWS, sub), BF16)
    maps = range(2)

    acc[...] = jnp.zeros_like(acc)
    m_ref[...] = jnp.full_like(m_ref, -jnp.inf)

    def scores_sub(j, r, mx):
        kc = k_ref[0, pl.ds(pl.multiple_of(j * tk + r * sub, sub), sub), :]
        out = []
        for mp in maps:
            s = _dot_nt(kc, qs[mp])
            s_ref[mp, r * sub:(r + 1) * sub, :] = s
            cm = jnp.max(s, axis=0, keepdims=True)
            out.append(cm if mx is None else jnp.maximum(mx[mp], cm))
        return out

    def softmax_sub(r, m_new):
        for mp in maps:
            s = s_ref[mp, r * sub:(r + 1) * sub, :]
            p_ref[mp, r * sub:(r + 1) * sub, :] = jnp.exp2(s - m_new[mp]).astype(BF16)

    def values_sub(j, r, pv):
        vc = vt_ref[0, :, pl.ds(pl.multiple_of(j * tk + r * sub, sub), sub)]
        va = jnp.concatenate([vc, ones_rows], axis=0)
        out = []
        for mp in maps:
            d = _dot(va, p_ref[mp, r * sub:(r + 1) * sub, :])
            out.append(d if pv is None else pv[mp] + d)
        return out

    def accumulate(pv):
        for mp in maps:
            acc[mp] = alpha_ref[mp] * acc[mp] + pv[mp]

    def trip(j, first, last):
        m_old = [m_ref[mp] for mp in maps]
        m_new = [jnp.maximum(m_old[mp], mx_ref[mp]) for mp in maps]
        pv = mx = None
        for r in range(-1, nsub + 1):
            if not last and 0 <= r - 1:
                mx = scores_sub(j + 1, r - 1, mx)
            if not first and r + 1 < nsub:
                pv = values_sub(j - 1, r + 1, pv)
            if 0 <= r < nsub:
                softmax_sub(r, m_new)
        if not first:
            accumulate(pv)
        for mp in maps:
            alpha_ref[mp] = jnp.exp2(m_old[mp] - m_new[mp])
            m_ref[mp] = m_new[mp]
            if not last:
                mx_ref[mp] = mx[mp]

    mx = None
    for r in range(nsub):
        mx = scores_sub(0, r, mx)
    for mp in maps:
        mx_ref[mp] = mx[mp]
    trip(0, True, False)

    def body(j, carry):
        trip(j, False, False)
        return carry

    lax.fori_loop(1, n - 1, body, 0)
    trip(n - 1, False, True)
    pv = None
    for r in range(nsub):
        pv = values_sub(n - 1, r, pv)
    accumulate(pv)

    a1 = acc[0]
    a2 = acc[1]
    o = a1[0:DV_A] / a1[DV_A:DV_A + 1] - lam * (a2[0:DV_A] / a2[DV_A:DV_A + 1])
    ms = jnp.mean(o * o, axis=0, keepdims=True)
    y = o * lax.rsqrt(ms + LN_EPS) * g_ref[...] * (1.0 - lam_init)
    o_ref[...] = y.T.astype(o_ref.dtype)


def _attn(q, k, vt, lambda_qk, subln_g, lam_init, tq, tk, sub, tiles_per_step):
    B, S, _ = q.shape
    assert S // tk >= 2 and tk % sub == 0, (S, tk, sub)
    tb = tq * tiles_per_step
    assert S % tb == 0, (S, tq, tiles_per_step)
    nsets = min(tiles_per_step, 2)
    kern = functools.partial(_attn_kernel, tq=tq, tk=tk, sub=sub, lam_init=lam_init)
    return pl.pallas_call(
        kern,
        grid=(B, N_HEADS_A, S // tb),
        in_specs=[pl.BlockSpec((4, DQK_A), lambda b, h, i: (0, 0)),
                  pl.BlockSpec((1, tb, DV_A), lambda b, h, i: (b, i, h)),
                  pl.BlockSpec((1, S, DV_A), lambda b, h, i: (b, 0, h)),
                  pl.BlockSpec((1, DV_A, S), lambda b, h, i: (b, h, 0)),
                  pl.BlockSpec((DV_A, 1), lambda b, h, i: (0, 0))],
        out_specs=pl.BlockSpec((1, tb, DV_A), lambda b, h, i: (b, i, h)),
        out_shape=jax.ShapeDtypeStruct((B, S, N_HEADS_A * DV_A), BF16),
        scratch_shapes=[pltpu.VMEM((nsets, 2, tk, tq), F32), pltpu.VMEM((nsets, 2, tk, tq), BF16),
                        pltpu.VMEM((nsets, 2, DV_A + ONES_ROWS, tq), F32),
                        pltpu.VMEM((nsets, 2, 1, tq), F32), pltpu.VMEM((nsets, 2, 1, tq), F32),
                        pltpu.VMEM((nsets, 2, 1, tq), F32)],
        compiler_params=_params(("parallel", "parallel", "parallel")),
        name="attn",
    )(lambda_qk.astype(F32), q, k, vt, subln_g.reshape(DV_A, 1).astype(F32))


N_STREAMS = 2 * N_HEADS_M
MLSTM_GROUP = 2
DV_AUG = DV_M + LANES


def _mlstm_kernel(qf_ref, kf_ref, vf_ref, gcf_ref, grf_ref,
                  qb_ref, kb_ref, vb_ref, gcb_ref, grb_ref,
                  hf_ref, hb_ref, ct_ref, m_ref):
    L = CHUNK

    @pl.when(pl.program_id(1) == 0)
    def _():
        ct_ref[...] = jnp.zeros_like(ct_ref)
        m_ref[...] = jnp.zeros_like(m_ref)

    row = lax.broadcasted_iota(jnp.int32, (L, L), 0)
    col = lax.broadcasted_iota(jnp.int32, (L, L), 1)
    assert L == LANES
    ones_blk = jnp.ones((L, LANES), BF16)

    def lane_tile(a, width):
        return jnp.concatenate([a] * (width // LANES), axis=1)

    dirs = ((qf_ref, kf_ref, vf_ref, gcf_ref, grf_ref, hf_ref),
            (qb_ref, kb_ref, vb_ref, gcb_ref, grb_ref, hb_ref))
    m_all = m_ref[...]
    m_out = []
    nb = qf_ref.shape[0]
    streams = [(bb, d, hd) for bb in range(nb) for d in range(2) for hd in range(N_HEADS_M)]

    def operands(bb, d, hd):
        q_ref, k_ref, v_ref = dirs[d][0:3]
        qh = q_ref[bb, :, hd * DQK_M:(hd + 1) * DQK_M]
        kh = k_ref[bb, :, hd * DQK_M:(hd + 1) * DQK_M]
        va = jnp.concatenate([v_ref[bb, :, hd * DV_M:(hd + 1) * DV_M], ones_blk], axis=1)
        return qh, kh, va

    groups = [list(enumerate(streams))[g0:g0 + MLSTM_GROUP]
              for g0 in range(0, len(streams), MLSTM_GROUP)]
    gate = {}
    sqk = {}

    def gate_stage(group):
        for st, (bb, d, hd) in group:
            tri = (col <= row) if d == 0 else (col >= row)
            e = L - 1 if d == 0 else 0
            gc = dirs[d][3][bb]
            gr = dirs[d][4][bb]
            ci = d * 2 * N_HEADS_M + hd
            cf = ci + N_HEADS_M
            i_row = gr[ci:ci + 1, :]
            b_row = gr[cf:cf + 1, :]
            b_t = jnp.broadcast_to(gc[:, cf:cf + 1], (L, LANES))
            m_prev = m_all[st:st + 1, :]
            dm = jnp.where(tri, b_t - b_row + i_row, -jnp.inf)
            inter = b_t + m_prev
            m_t = jnp.maximum(inter, jnp.broadcast_to(jnp.max(dm, axis=1, keepdims=True), (L, LANES)))
            w = jnp.exp(dm - m_t)
            w_inter = jnp.exp(inter - m_t)
            b_end = b_row[:, e:e + 1]
            m_new = m_t[e:e + 1, :]
            ws = jnp.exp(b_end - b_row + i_row - m_new)
            decay = jnp.exp(b_end + m_prev - m_new)
            gate[st] = (w, w_inter, jnp.exp(-m_t), ws, decay)
            m_out.append(m_new)

    def score_stage(group):
        for st, sid in group:
            qh, kh, _ = operands(*sid)
            sqk[st] = (_dot_nt(qh, kh) * gate[st][0]).astype(BF16)

    def output_stage(group):
        for st, (bb, d, hd) in group:
            qh, _, va = operands(bb, d, hd)
            _, w_inter, floor, _, _ = gate[st]
            numa = _dot(sqk[st], va) + lane_tile(w_inter, DV_AUG) * _dot(qh, ct_ref[st].astype(BF16))
            den = numa[:, DV_M:DV_AUG]
            inv = 1.0 / jnp.maximum(jnp.abs(den), floor)
            h = numa[:, 0:DV_M] * lane_tile(inv, DV_M)
            dirs[d][5][bb, :, hd * DV_M:(hd + 1) * DV_M] = h.astype(BF16)

    def state_stage(group):
        for st, sid in group:
            _, kh, va = operands(*sid)
            _, _, _, ws, decay = gate[st]
            kst = (kh.astype(F32).T * ws).astype(BF16)
            ct_ref[st] = lane_tile(decay, DV_AUG) * ct_ref[st] + _dot(kst, va)

    stages = (gate_stage, score_stage, output_stage, state_stage)
    for t in range(len(groups) + len(stages) - 1):
        for k, stage in enumerate(stages):
            if 0 <= t - k < len(groups):
                stage(groups[t - k])
    m_ref[...] = jnp.concatenate(m_out, axis=0)


def _mlstm(mq, mk, mv, gcol, grow):
    B, S, _ = mq.shape
    nc = S // CHUNK
    nb = next(n for n in (4, 2, 1) if B % n == 0)
    fwd = lambda n: pl.BlockSpec((nb, CHUNK, n), lambda b, c: (b, c, 0))
    bwd = lambda n: pl.BlockSpec((nb, CHUNK, n), lambda b, c: (b, nc - 1 - c, 0))
    rowf = pl.BlockSpec((nb, W_MG, CHUNK), lambda b, c: (b, 0, c))
    rowb = pl.BlockSpec((nb, W_MG, CHUNK), lambda b, c: (b, 0, nc - 1 - c))
    return pl.pallas_call(
        _mlstm_kernel,
        grid=(B // nb, nc),
        in_specs=[fwd(W_MQ), fwd(W_MK), fwd(W_MV), fwd(W_MG), rowf,
                  bwd(W_MQ), bwd(W_MK), bwd(W_MV), bwd(W_MG), rowb],
        out_specs=(fwd(W_MV), bwd(W_MV)),
        out_shape=(jax.ShapeDtypeStruct((B, S, W_MV), BF16),
                   jax.ShapeDtypeStruct((B, S, W_MV), BF16)),
        scratch_shapes=[pltpu.VMEM((nb * N_STREAMS, DQK_M, DV_AUG), F32),
                        pltpu.VMEM((nb * N_STREAMS, LANES), F32)],
        compiler_params=_params(("parallel", "arbitrary")),
        name="mlstm",
    )(mq, mk, mv, gcol, grow, mq, mk, mv, gcol, grow)


def _mix_kernel(x_ref, ada_ref, ya_ref, hf_ref, hb_ref, og_ref, gab_ref,
                wpa_ref, wpm_ref, wout_ref, mng_ref, lg_ref, lb_ref, o_ref, *, alpha):
    hs = hf_ref[0].astype(F32) + hb_ref[0].astype(F32)
    hn = jnp.concatenate(
        [_norm(hs[:, hd * DV_M:(hd + 1) * DV_M]) for hd in range(N_HEADS_M)], axis=1)
    ym = (og_ref[0].astype(F32) * (hn * mng_ref[...])).astype(BF16)
    gab = gab_ref[0].astype(F32)
    pre = gab[:, 0:D_MODEL] * _dot(ya_ref[0], wpa_ref[...]) + gab[:, D_MODEL:] * _dot(ym, wpm_ref[...])
    mix = _dot(pre.astype(BF16), wout_ref[...])
    g1 = ada_ref[0][:, 2 * D_MODEL:3 * D_MODEL]
    o_ref[0] = _norm(alpha * x_ref[0] + g1 * mix) * lg_ref[...] + lb_ref[...]


def _mix(x, ada3, ya, hf, hb, og, gab, wpa, wpm, wout, mnorm_g, ln_g, ln_b, alpha, ts):
    B, S, D = x.shape
    tok = lambda n: pl.BlockSpec((1, ts, n), lambda b, i: (b, i, 0))
    consts = (wpa, wpm, wout, mnorm_g.reshape(1, W_MV).astype(F32),
              ln_g.reshape(1, D).astype(F32), ln_b.reshape(1, D).astype(F32))
    return pl.pallas_call(
        functools.partial(_mix_kernel, alpha=alpha),
        grid=(B, S // ts),
        in_specs=[tok(D), pl.BlockSpec((1, 1, 6 * D), lambda b, i: (b, 0, 0)),
                  tok(W_AV), tok(W_MV), tok(W_MV), tok(W_MO), tok(W_BG)]
                 + [_const_spec(w.shape) for w in consts],
        out_specs=tok(D),
        out_shape=jax.ShapeDtypeStruct((B, S, D), F32),
        compiler_params=_params(("parallel", "parallel")),
        name="mix",
    )(x, ada3, ya, hf, hb, og, gab, *consts)


def _gelu_tanh(x):
    return 0.5 * x * (1.0 + jnp.tanh(math.sqrt(2.0 / math.pi) * (x + 0.044715 * (x * x * x))))


def _ffn_kernel(xp_ref, x_ref, xn_ref, ada_ref, wug_ref, wuv_ref, cw_ref, cb_ref, wd_ref,
                lg_ref, lb_ref, o_ref, *, alpha):
    ts = x_ref.shape[1]
    H = SUBLANES
    i = pl.program_id(1)
    ada = ada_ref[0]
    sh2 = ada[:, 3 * D_MODEL:4 * D_MODEL]
    sc2 = ada[:, 4 * D_MODEL:5 * D_MODEL]
    g2 = ada[:, 5 * D_MODEL:6 * D_MODEL]
    x = x_ref[0]

    def modulated(v):
        return _norm(v) * (1.0 + sc2) + sh2

    hp = jnp.where(i == 0, 0.0, modulated(xp_ref[0]))
    hm = modulated(x)
    hn = jnp.where(i == pl.num_programs(1) - 1, 0.0, modulated(xn_ref[0]))
    ug = _dot(jnp.concatenate([hp, hm, hn], axis=0).astype(BF16), wug_ref[...])
    uv = _dot(hm.astype(BF16), wuv_ref[...])
    n = ts + 2 * H
    prev = pltpu.roll(ug, 1, 0)[H:H + ts]
    nxt = pltpu.roll(ug, n - 1, 0)[H:H + ts]
    cw = cw_ref[...]
    conv = prev * cw[0:1] + ug[H:H + ts] * cw[1:2] + nxt * cw[2:3] + cb_ref[...]
    y = _dot((_gelu_tanh(conv) * uv).astype(BF16), wd_ref[...])
    o_ref[0] = _norm(alpha * x + g2 * y) * lg_ref[...] + lb_ref[...]


def _ffn(x, ada3, wug, wuv, conv_w, conv_b, wd, ln_g, ln_b, alpha, ts):
    B, S, D = x.shape
    H = SUBLANES
    nb = ts // H
    last = S // H - 1
    tok = pl.BlockSpec((1, ts, D), lambda b, i: (b, i, 0))
    prev = pl.BlockSpec((1, H, D), lambda b, i: (b, jnp.maximum(i * nb - 1, 0), 0))
    nxt = pl.BlockSpec((1, H, D), lambda b, i: (b, jnp.minimum((i + 1) * nb, last), 0))
    consts = (wug, wuv, conv_w.astype(F32), conv_b.reshape(1, D_FF).astype(F32), wd,
              ln_g.reshape(1, D).astype(F32), ln_b.reshape(1, D).astype(F32))
    return pl.pallas_call(
        functools.partial(_ffn_kernel, alpha=alpha),
        grid=(B, S // ts),
        in_specs=[prev, tok, nxt, pl.BlockSpec((1, 1, 6 * D), lambda b, i: (b, 0, 0))]
                 + [_const_spec(w.shape) for w in consts],
        out_specs=tok,
        out_shape=jax.ShapeDtypeStruct((B, S, D), F32),
        compiler_params=_params(("parallel", "parallel")),
        name="ffn",
    )(x, x, x, ada3, *consts)


def _rope_tables(seq):
    half = ROT_DIM // 2
    inv = ROPE_THETA ** (-jnp.arange(0, ROT_DIM, 2, dtype=F32) / ROT_DIM)
    ang = jnp.arange(seq, dtype=F32)[:, None] * inv[None, :]
    cos, sin = jnp.cos(ang), jnp.sin(ang)
    rest = DQK_A - ROT_DIM
    one = jnp.ones((seq, rest), F32)
    zr = jnp.zeros((seq, rest), F32)
    zh = jnp.zeros((seq, half), F32)
    reps = LANES // DQK_A
    ct = jnp.tile(jnp.concatenate([cos, cos, one], axis=1), (1, reps))
    s1 = jnp.tile(jnp.concatenate([-sin, zh, zr], axis=1), (1, reps))
    s2 = jnp.tile(jnp.concatenate([zh, sin, zr], axis=1), (1, reps))
    return ct, s1, s2


def _layer_weights(w_in, w_pa, w_pm, w_out, w_up, w_down):
    seg = lambda j: w_in[:, SEG_STARTS[j]:SEG_STARTS[j] + SEG_WIDTHS[j]]
    proj = (seg(0).astype(BF16), seg(1).astype(BF16), seg(2).T.astype(BF16),
            seg(3).astype(BF16), seg(4).astype(BF16), seg(5).astype(BF16), seg(6).astype(BF16),
            seg(7).T.astype(BF16), seg(8).astype(BF16))
    return (proj, w_pa.astype(BF16), w_pm.astype(BF16), w_out.astype(BF16),
            w_up[:, :D_FF].astype(BF16), w_up[:, D_FF:].astype(BF16), w_down.astype(BF16))


def _tile(seq, want):
    t = min(seq, want)
    assert seq % t == 0, (seq, want)
    return t


def _default_tiles(seq):
    tq, tk = 1024, min(2048, max(seq // 4, 256))
    per_step = max(seq // tq, 1) if seq // tk <= 4 else 1
    return (512, tq, tk, 512, per_step)


def _encoder_layer(x, ada, l, depth, weights, lambda_qk, subln_g, b_mgate, mnorm_g,
                   ln1_g, ln1_b, conv_w, conv_b, ln2_g, ln2_b, tiles):
    B, S, D = x.shape
    tiles = tiles or _default_tiles(S)
    ts, tq, tk, sub = (_tile(S, t) for t in tiles[:4])
    tiles_per_step = tiles[4]
    alpha = (2.0 * depth) ** 0.25
    lam_init = 0.8 - 0.6 * math.exp(-0.3 * l)
    proj_w, wpa, wpm, wout, wug, wuv, wd = weights
    ada3 = ada.reshape(B, 1, 6 * D)
    q, k, vt, mq, mk, mv, og, gcol, grow, gab = _proj(x, ada3, _rope_tables(S), proj_w, b_mgate, ts)
    ya = _attn(q, k, vt, lambda_qk, subln_g, lam_init, tq, tk, sub, tiles_per_step)
    hf, hb = _mlstm(mq, mk, mv, gcol, grow)
    x1 = _mix(x, ada3, ya, hf, hb, og, gab, wpa, wpm, wout, mnorm_g, ln1_g, ln1_b, alpha, ts)
    return _ffn(x1, ada3, wug, wuv, conv_w, conv_b, wd, ln2_g, ln2_b, alpha, ts)


def _trunk(xs, cs, w_ada, b_ada, w_in, lambda_qk, subln_g, b_mgate, mnorm_g, w_pa, w_pm, w_out,
           ln1_g, ln1_b, w_up, conv_w, conv_b, w_down, ln2_g, ln2_b, tiles=None):
    depth = w_in.shape[0]
    rows = [c.shape[0] for c in cs]
    c_all = jnp.concatenate(cs, axis=0)
    xs = list(xs)
    for l in range(depth):
        ada_all = _ada(c_all, w_ada[l], b_ada[l])
        weights = _layer_weights(w_in[l], w_pa[l], w_pm[l], w_out[l], w_up[l], w_down[l])
        start = 0
        for g, n in enumerate(rows):
            xs[g] = _encoder_layer(xs[g], ada_all[start:start + n], l, depth, weights,
                                   lambda_qk[l], subln_g[l], b_mgate[l], mnorm_g[l],
                                   ln1_g[l], ln1_b[l], conv_w[l], conv_b[l], ln2_g[l], ln2_b[l], tiles)
            start += n
    return tuple(xs)


def kernel(x_prompt, x_sample, c_prompt, c_sample, w_ada, b_ada, w_in, lambda_qk, subln_g, b_mgate, mnorm_g, w_pa, w_pm, w_out, ln1_g, ln1_b, w_up, conv_w, conv_b, w_down, ln2_g, ln2_b):
    return _trunk((x_prompt, x_sample), (c_prompt, c_sample), w_ada, b_ada, w_in, lambda_qk, subln_g,
                  b_mgate, mnorm_g, w_pa, w_pm, w_out, ln1_g, ln1_b, w_up, conv_w, conv_b, w_down,
                  ln2_g, ln2_b)
```
